```python
import jax
import jax.numpy as jnp
from jax import lax
import numpy as np

D_MODEL = 1024
BATCH = 8
SEQ = 4096
DEPTH = 4

N_MIXERS = 3
EPS = 1e-6
D_FF = 2816
A_HEADS = 16
A_KV_HEADS = 4
A_HEAD_DIM = D_MODEL // A_HEADS
A_WINDOW = 128
A_BLOCK = 128
A_ROPE_DIM = A_HEAD_DIM // 4
ROPE_THETA = 500000.0
M_HEADS = 8
M_QK_DIM = D_MODEL // (2 * M_HEADS)
M_V_DIM = D_MODEL // M_HEADS
M_CHUNK = 64
M_GATE_CAP = 15.0
R_HEADS = 4
R_QK_DIM = D_MODEL // R_HEADS
R_V_DIM = 2 * D_MODEL // R_HEADS
R_CHUNK = 128
R_THETA = 10000.0
N_ATTN = (DEPTH + 2) // N_MIXERS
N_MLSTM = (DEPTH + 1) // N_MIXERS
N_RET = DEPTH // N_MIXERS
A_QKV_DIM = (A_HEADS + 2 * A_KV_HEADS) * A_HEAD_DIM
M_IN_DIM = 2 * M_HEADS * M_QK_DIM + 2 * M_HEADS * M_V_DIM + 2 * M_HEADS
R_IN_DIM = 2 * R_HEADS * R_QK_DIM + 2 * R_HEADS * R_V_DIM

kernel_name = 'hybrid_swa_mlstm_retention_macaron'


def rms_norm(x, w):
    xf = x.astype(jnp.float32)
    y = xf * lax.rsqrt(jnp.mean(xf * xf, axis=-1, keepdims=True) + EPS)
    return (y * w.astype(jnp.float32)).astype(x.dtype)


def swiglu(x, w_in, w_out):
    gate, up = jnp.split(x @ w_in, 2, axis=-1)
    return (jax.nn.silu(gate) * up) @ w_out


def partial_rope(x, pos, rot_dim, theta):
    inv = 1.0 / (theta ** (jnp.arange(0, rot_dim, 2, dtype=jnp.float32) / rot_dim))
    ang = pos.astype(jnp.float32)[:, :, None, None] * inv
    cos, sin = jnp.cos(ang), jnp.sin(ang)
    xf = x.astype(jnp.float32)
    half = rot_dim // 2
    x1, x2, rest = xf[..., :half], xf[..., half:rot_dim], xf[..., rot_dim:]
    return jnp.concatenate([x1 * cos - x2 * sin, x2 * cos + x1 * sin, rest], axis=-1).astype(x.dtype)


def retnet_rotate(x, pos, theta):
    d = x.shape[-1]
    inv = 1.0 / (theta ** jnp.linspace(0.0, 1.0, d // 2, dtype=jnp.float32))
    ang = pos.astype(jnp.float32)[:, :, None, None] * inv
    cos, sin = jnp.cos(ang), jnp.sin(ang)
    xf = x.astype(jnp.float32)
    x1, x2 = xf[..., 0::2], xf[..., 1::2]
    return jnp.stack([x1 * cos - x2 * sin, x2 * cos + x1 * sin], axis=-1).reshape(x.shape).astype(x.dtype)


def swa_sink_attention(h, pos, w_qkv, b_qkv, w_o, b_o, sinks):
    B, S, _ = h.shape
    L, KV, G, hd = A_BLOCK, A_KV_HEADS, A_HEADS // A_KV_HEADS, A_HEAD_DIM
    nb = S // L
    qkv = h @ w_qkv + b_qkv
    q = qkv[..., :A_HEADS * hd].reshape(B, S, A_HEADS, hd)
    k = qkv[..., A_HEADS * hd:(A_HEADS + KV) * hd].reshape(B, S, KV, hd)
    v = qkv[..., (A_HEADS + KV) * hd:].reshape(B, S, KV, hd)
    q = partial_rope(q, pos, A_ROPE_DIM, ROPE_THETA)
    k = partial_rope(k, pos, A_ROPE_DIM, ROPE_THETA)
    qb = q.reshape(B, nb, L, KV, G, hd)
    kb = k.reshape(B, nb, L, KV, hd)
    vb = v.reshape(B, nb, L, KV, hd)
    kw = jnp.concatenate([jnp.concatenate([jnp.zeros_like(kb[:, :1]), kb[:, :-1]], axis=1), kb], axis=2)
    vw = jnp.concatenate([jnp.concatenate([jnp.zeros_like(vb[:, :1]), vb[:, :-1]], axis=1), vb], axis=2)
    scores = jnp.einsum('bnqkgd,bnskd->bnkgqs', qb, kw).astype(jnp.float32) * (hd ** -0.5)
    qi = jnp.arange(L)[:, None]
    kj = jnp.arange(2 * L)[None, :]
    rel = qi + L - kj
    band = (rel >= 0) & (rel < A_WINDOW)
    blk = jnp.arange(nb)[:, None, None]
    valid = band[None] & (blk * L + kj[None] - L >= 0)
    scores = jnp.where(valid[None, :, None, None], scores, -jnp.inf)
    sink = sinks.astype(jnp.float32).reshape(KV, G)[None, None, :, :, None, None]
    mx = jnp.maximum(jnp.max(scores, axis=-1, keepdims=True), sink)
    p = jnp.exp(scores - mx)
    probs = (p / (jnp.sum(p, axis=-1, keepdims=True) + jnp.exp(sink - mx))).astype(vw.dtype)
    o = jnp.einsum('bnkgqs,bnskd->bnqkgd', probs, vw).reshape(B, S, A_HEADS * hd)
    return o @ w_o + b_o


def mlstm_mixer(h, w_in, b_if, norm_w, w_out):
    B, S, _ = h.shape
    H, dk, dv, L = M_HEADS, M_QK_DIM, M_V_DIM, M_CHUNK
    N = S // L
    proj = h @ w_in
    c0, c1, c2, c3 = H * dk, 2 * H * dk, 2 * H * dk + H * dv, 2 * H * dk + 2 * H * dv
    q, k, v, o, gates = proj[..., :c0], proj[..., c0:c1], proj[..., c1:c2], proj[..., c2:c3], proj[..., c3:]
    gates = gates.astype(jnp.float32) + b_if.astype(jnp.float32)
    gates = M_GATE_CAP * jnp.tanh(gates / M_GATE_CAP)
    log_i = gates[..., :H]
    log_f = jax.nn.log_sigmoid(gates[..., H:])

    def chunks(t, d):
        return t.astype(jnp.float32).reshape(B, N, L, H, d).transpose(1, 0, 3, 2, 4)

    def gchunks(t):
        return t.reshape(B, N, L, H).transpose(1, 0, 3, 2)

    xs = (chunks(q, dk) * (dk ** -0.5), chunks(k, dk), chunks(v, dv), gchunks(log_i), gchunks(log_f))
    causal = jnp.tril(jnp.ones((L, L), dtype=bool))

    def step(carry, inp):
        C, n, m = carry
        qc, kc, vc, li, lf = inp
        b = jnp.cumsum(lf, axis=-1)
        dmat = jnp.where(causal, b[..., :, None] - b[..., None, :] + li[..., None, :], -jnp.inf)
        inter = b + m[..., None]
        m_row = jnp.maximum(inter, jnp.max(dmat, axis=-1))
        w_intra = jnp.exp(dmat - m_row[..., None])
        w_inter = jnp.exp(inter - m_row)
        s = jnp.einsum('bhtd,bhsd->bhts', qc, kc) * w_intra
        num = jnp.einsum('bhts,bhsv->bhtv', s, vc) + w_inter[..., None] * jnp.einsum('bhvd,bhtd->bhtv', C, qc)
        nq = jnp.sum(s, axis=-1) + w_inter * jnp.einsum('bhd,bhtd->bht', n, qc)
        h_out = num / jnp.maximum(jnp.abs(nq), jnp.exp(-m_row))[..., None]
        b_last = b[..., -1]
        dec = b_last[..., None] - b + li
        m_new = jnp.maximum(b_last + m, jnp.max(dec, axis=-1))
        wk = jnp.exp(dec - m_new[..., None])
        keep = jnp.exp(b_last + m - m_new)
        C_new = keep[..., None, None] * C + jnp.einsum('bhsv,bhsd->bhvd', vc * wk[..., None], kc)
        n_new = keep[..., None] * n + jnp.einsum('bhs,bhsd->bhd', wk, kc)
        return (C_new, n_new, m_new), h_out

    init = (jnp.zeros((B, H, dv, dk), jnp.float32), jnp.zeros((B, H, dk), jnp.float32), jnp.zeros((B, H), jnp.float32))
    _, hs = lax.scan(step, init, xs)
    hs = hs.transpose(1, 0, 3, 2, 4).reshape(B, S, H, dv)
    hs = hs * lax.rsqrt(jnp.mean(hs * hs, axis=-1, keepdims=True) + EPS)
    hs = hs.reshape(B, S, H * dv) * norm_w.astype(jnp.float32)
    return (jax.nn.sigmoid(o.astype(jnp.float32)) * hs).astype(h.dtype) @ w_out


def retention_mixer(h, pos, w_in, norm_w, w_out):
    B, S, _ = h.shape
    H, dk, dv, L = R_HEADS, R_QK_DIM, R_V_DIM, R_CHUNK
    N = S // L
    proj = h @ w_in
    c0, c1, c2 = H * dk, 2 * H * dk, 2 * H * dk + H * dv
    q = retnet_rotate(proj[..., :c0].reshape(B, S, H, dk), pos, R_THETA)
    k = retnet_rotate(proj[..., c0:c1].reshape(B, S, H, dk), pos, R_THETA) * (dk ** -0.5)
    v = proj[..., c1:c2].reshape(B, S, H, dv)
    g = proj[..., c2:]
    log_gamma = jnp.log(1.0 - 2.0 ** (-5.0 - jnp.arange(H, dtype=jnp.float32)))
    j = jnp.arange(L, dtype=jnp.float32)
    diff = j[:, None] - j[None, :]
    causal = diff >= 0
    dmask = jnp.where(causal, jnp.exp(jnp.where(causal, diff, 0.0) * log_gamma[:, None, None]), 0.0)
    q_decay = jnp.exp((j + 1.0) * log_gamma[:, None])
    k_decay = jnp.exp((L - 1.0 - j) * log_gamma[:, None])
    chunk_decay = jnp.exp(L * log_gamma)

    def chunks(t, d):
        return t.astype(jnp.float32).reshape(B, N, L, H, d).transpose(1, 0, 3, 2, 4)

    def step(state, inp):
        qc, kc, vc = inp
        inner = jnp.einsum('bhts,bhsv->bhtv', jnp.einsum('bhtd,bhsd->bhts', qc, kc) * dmask, vc)
        cross = jnp.einsum('bhtd,bhdv->bhtv', qc * q_decay[..., None], state)
        new_state = chunk_decay[:, None, None] * state + jnp.einsum('bhsd,bhsv->bhdv', kc * k_decay[..., None], vc)
        return new_state, inner + cross

    _, ys = lax.scan(step, jnp.zeros((B, H, dk, dv), jnp.float32), (chunks(q, dk), chunks(k, dk), chunks(v, dv)))
    y = ys.transpose(1, 0, 3, 2, 4).reshape(B, S, H, dv)
    mu = jnp.mean(y, axis=-1, keepdims=True)
    yc = y - mu
    y = yc * lax.rsqrt(jnp.mean(yc * yc, axis=-1, keepdims=True) + EPS)
    y = y.reshape(B, S, H * dv) * norm_w.astype(jnp.float32)
    return (jax.nn.silu(g.astype(jnp.float32)) * y).astype(h.dtype) @ w_out


def setup_inputs(seed: int = 0) -> dict:
    key = jax.random.key(seed)
    ks = jax.random.split(key, 20)
    f32 = jnp.float32

    def dense(k, shape, fan_in):
        return jax.random.normal(k, shape, f32) * (fan_in ** -0.5)

    x = jax.random.normal(ks[0], (BATCH, SEQ, D_MODEL), f32)
    offset = jax.random.randint(ks[1], (BATCH, 1), 0, 1024, dtype=jnp.int32)
    positions = offset + jnp.arange(SEQ, dtype=jnp.int32)[None, :]
    norm_w = 1.0 + 0.05 * jax.random.normal(ks[2], (DEPTH, 3, D_MODEL), f32)
    final_norm_w = 1.0 + 0.05 * jax.random.normal(ks[3], (D_MODEL,), f32)
    ffn_w_in = dense(ks[4], (DEPTH, 2, D_MODEL, 2 * D_FF), D_MODEL)
    ffn_w_out = dense(ks[5], (DEPTH, 2, D_FF, D_MODEL), D_FF)
    attn_w_qkv = dense(ks[6], (N_ATTN, D_MODEL, A_QKV_DIM), D_MODEL)
    attn_b_qkv = 0.02 * jax.random.normal(ks[7], (N_ATTN, A_QKV_DIM), f32)
    attn_w_o = dense(ks[8], (N_ATTN, A_HEADS * A_HEAD_DIM, D_MODEL), A_HEADS * A_HEAD_DIM)
    attn_b_o = 0.02 * jax.random.normal(ks[9], (N_ATTN, D_MODEL), f32)
    attn_sinks = 0.5 * jax.random.normal(ks[10], (N_ATTN, A_HEADS), f32)
    mlstm_w_in = dense(ks[11], (N_MLSTM, D_MODEL, M_IN_DIM), D_MODEL)
    f_bias = jnp.linspace(3.0, 6.0, M_HEADS, dtype=f32)[None, :]
    mlstm_b_if = jnp.concatenate([0.1 * jax.random.normal(ks[12], (N_MLSTM, M_HEADS), f32),
                                  f_bias + 0.1 * jax.random.normal(ks[13], (N_MLSTM, M_HEADS), f32)], axis=-1)
    mlstm_norm_w = 1.0 + 0.05 * jax.random.normal(ks[14], (N_MLSTM, M_HEADS * M_V_DIM), f32)
    mlstm_w_out = dense(ks[15], (N_MLSTM, M_HEADS * M_V_DIM, D_MODEL), M_HEADS * M_V_DIM)
    ret_w_in = dense(ks[16], (N_RET, D_MODEL, R_IN_DIM), D_MODEL)
    ret_norm_w = 1.0 + 0.05 * jax.random.normal(ks[17], (N_RET, R_HEADS * R_V_DIM), f32)
    ret_w_out = dense(ks[18], (N_RET, R_HEADS * R_V_DIM, D_MODEL), R_HEADS * R_V_DIM)
    return {'x': x, 'positions': positions, 'norm_w': norm_w, 'final_norm_w': final_norm_w,
            'ffn_w_in': ffn_w_in, 'ffn_w_out': ffn_w_out,
            'attn_w_qkv': attn_w_qkv, 'attn_b_qkv': attn_b_qkv, 'attn_w_o': attn_w_o, 'attn_b_o': attn_b_o,
            'attn_sinks': attn_sinks,
            'mlstm_w_in': mlstm_w_in, 'mlstm_b_if': mlstm_b_if, 'mlstm_norm_w': mlstm_norm_w, 'mlstm_w_out': mlstm_w_out,
            'ret_w_in': ret_w_in, 'ret_norm_w': ret_norm_w, 'ret_w_out': ret_w_out}


def reference(x, positions, norm_w, final_norm_w, ffn_w_in, ffn_w_out,
              attn_w_qkv, attn_b_qkv, attn_w_o, attn_b_o, attn_sinks,
              mlstm_w_in, mlstm_b_if, mlstm_norm_w, mlstm_w_out,
              ret_w_in, ret_norm_w, ret_w_out):
    for i in range(DEPTH):
        kind, slot = i % N_MIXERS, i // N_MIXERS
        x = x + 0.5 * swiglu(rms_norm(x, norm_w[i, 0]), ffn_w_in[i, 0], ffn_w_out[i, 0])
        hn = rms_norm(x, norm_w[i, 1])
        if kind == 0:
            mix = swa_sink_attention(hn, positions, attn_w_qkv[slot], attn_b_qkv[slot], attn_w_o[slot],
                                     attn_b_o[slot], attn_sinks[slot])
        elif kind == 1:
            mix = mlstm_mixer(hn, mlstm_w_in[slot], mlstm_b_if[slot], mlstm_norm_w[slot], mlstm_w_out[slot])
        else:
            mix = retention_mixer(hn, positions, ret_w_in[slot], ret_norm_w[slot], ret_w_out[slot])
        x = x + mix
        x = x + 0.5 * swiglu(rms_norm(x, norm_w[i, 2]), ffn_w_in[i, 1], ffn_w_out[i, 1])
    return rms_norm(x, final_norm_w)
```

```python
import functools
import math

import jax
import jax.numpy as jnp
import numpy as np
from jax import lax
from jax.experimental import pallas as pl
from jax.experimental.pallas import tpu as pltpu

F32 = jnp.float32
BF16 = jnp.bfloat16

D_MODEL = 1024
DEPTH = 4
N_MIXERS = 3
EPS = 1e-6
D_FF = 2816
A_HEADS = 16
A_KV_HEADS = 4
A_HEAD_DIM = 64
A_GROUP = A_HEADS // A_KV_HEADS
A_WINDOW = 128
A_ROPE_DIM = 16
ROPE_THETA = 500000.0
M_HEADS = 8
M_QK_DIM = 64
M_V_DIM = 128
M_GATE_CAP = 15.0
R_HEADS = 4
R_QK_DIM = 256
R_V_DIM = 512
R_THETA = 10000.0

LANES = 128
V7X_VMEM_BYTES = 64 * 1024 * 1024
VMEM_LIMIT = 56 * 1024 * 1024

ROW_TILE = 512
FF_CHUNK = 256
SEQ_CHUNK = 128


def _params(n_axes):
    return pltpu.CompilerParams(
        dimension_semantics=("arbitrary",) * n_axes,
        vmem_limit_bytes=VMEM_LIMIT,
    )


def _resident(shape):
    nd = len(shape)
    return pl.BlockSpec(shape, lambda *_: (0,) * nd, pipeline_mode=pl.Buffered(1))


def _rows(tm, width):
    return pl.BlockSpec((tm, width), lambda i: (i, 0))


def _rms(x, w):
    ms = jnp.mean(x * x, axis=-1, keepdims=True)
    return x * lax.rsqrt(ms + EPS) * w


def _dot(a, b):
    return jnp.dot(a, b, preferred_element_type=F32)


def _dot_nt(a, b):
    return lax.dot_general(a, b, (((1,), (1,)), ((), ())), preferred_element_type=F32)


def _dot_tn(a, b):
    return lax.dot_general(a, b, (((0,), (0,)), ((), ())), preferred_element_type=F32)


def _tables_kernel(pos_ref, inv_a_ref, sgn_a_ref, inv_r_ref, cos_a, sin_a, cos_r, sin_r):
    pos = pos_ref[...]
    ang_a = pos * inv_a_ref[...]
    cos_a[...] = jnp.cos(ang_a)
    sin_a[...] = jnp.sin(ang_a) * sgn_a_ref[...]
    ang_r = pos * inv_r_ref[...]
    cos_r[...] = jnp.cos(ang_r)
    sin_r[...] = jnp.sin(ang_r)


def _rope_tables(pos_b, inv_a, sgn_a, inv_r, tm):
    t = pos_b.shape[0]
    out = jax.ShapeDtypeStruct((t, LANES), F32)
    return pl.pallas_call(
        _tables_kernel,
        grid=(t // tm,),
        in_specs=[_rows(tm, LANES), _resident((1, LANES)), _resident((1, LANES)), _resident((1, LANES))],
        out_specs=[_rows(tm, LANES)] * 4,
        out_shape=[out] * 4,
        compiler_params=_params(1),
        name="rope_tables",
    )(pos_b, inv_a, sgn_a, inv_r)


def _ffn_kernel(x_ref, nw_ref, win_ref, wout_ref, *rest, final_norm):
    if final_norm:
        fw_ref, o_ref = rest
    else:
        (o_ref,) = rest
    x = x_ref[...]
    h = _rms(x, nw_ref[...]).astype(BF16)
    acc = jnp.zeros_like(x)
    for c in range(D_FF // FF_CHUNK):
        lo = c * FF_CHUNK
        g = _dot(h, win_ref[:, lo:lo + FF_CHUNK])
        u = _dot(h, win_ref[:, D_FF + lo:D_FF + lo + FF_CHUNK])
        a = (g * jax.nn.sigmoid(g) * u).astype(BF16)
        acc = acc + _dot(a, wout_ref[lo:lo + FF_CHUNK, :])
    y = x + 0.5 * acc
    if final_norm:
        y = _rms(y, fw_ref[...])
    o_ref[...] = y


def _ffn(x, nw, w_in, w_out, final_w=None, *, tm):
    t = x.shape[0]
    final_norm = final_w is not None
    in_specs = [_rows(tm, D_MODEL), _resident((1, D_MODEL)),
                _resident((D_MODEL, 2 * D_FF)), _resident((D_FF, D_MODEL))]
    args = [x, nw, w_in, w_out]
    if final_norm:
        in_specs.append(_resident((1, D_MODEL)))
        args.append(final_w)
    return pl.pallas_call(
        functools.partial(_ffn_kernel, final_norm=final_norm),
        grid=(t // tm,),
        in_specs=in_specs,
        out_specs=_rows(tm, D_MODEL),
        out_shape=jax.ShapeDtypeStruct((t, D_MODEL), F32),
        compiler_params=_params(1),
        name="ffn_final" if final_norm else "ffn",
    )(*args)


def _out_proj_kernel(x_ref, y_ref, w_ref, b_ref, o_ref):
    o_ref[...] = x_ref[...] + (_dot(y_ref[...], w_ref[...]) + b_ref[...])


def _out_proj(x, y, w, b, *, tm):
    t = x.shape[0]
    kdim = y.shape[1]
    return pl.pallas_call(
        _out_proj_kernel,
        grid=(t // tm,),
        in_specs=[_rows(tm, D_MODEL), _rows(tm, kdim), _resident((kdim, D_MODEL)), _resident((1, D_MODEL))],
        out_specs=_rows(tm, D_MODEL),
        out_shape=jax.ShapeDtypeStruct((t, D_MODEL), F32),
        compiler_params=_params(1),
        name="out_proj",
    )(x, y, w, b)


A_Q_COLS = A_HEADS * A_HEAD_DIM
A_KV_COLS = A_KV_HEADS * LANES
A_PROJ_COLS = A_Q_COLS + 2 * A_KV_COLS
A_PROJ_CHUNK = 512


def _attn_proj_kernel(x_ref, nw_ref, w_ref, b_ref, cos_ref, sin_ref, q_ref, k_ref, v_ref):
    tm = x_ref.shape[0]
    h = _rms(x_ref[...], nw_ref[...]).astype(BF16)
    cos = cos_ref[...]
    sin = sin_ref[...]
    lane = lax.broadcasted_iota(jnp.int32, (tm, LANES), 1) & (A_HEAD_DIM - 1)
    upper = (lane >= A_ROPE_DIM // 2) & (lane < A_ROPE_DIM)

    def rope(t):
        partner = jnp.where(upper, pltpu.roll(t, A_ROPE_DIM // 2, 1),
                            pltpu.roll(t, LANES - A_ROPE_DIM // 2, 1))
        return t * cos + partner * sin

    q_scale = A_HEAD_DIM ** -0.5
    for c in range(A_PROJ_COLS // A_PROJ_CHUNK):
        lo = c * A_PROJ_CHUNK
        t = _dot(h, w_ref[:, lo:lo + A_PROJ_CHUNK]) + b_ref[:, lo:lo + A_PROJ_CHUNK]
        for j in range(A_PROJ_CHUNK // LANES):
            col = lo + j * LANES
            piece = t[:, j * LANES:(j + 1) * LANES]
            if col < A_Q_COLS:
                q_ref[:, col:col + LANES] = (rope(piece) * q_scale).astype(BF16)
            elif col < A_Q_COLS + A_KV_COLS:
                kc = col - A_Q_COLS
                k_ref[:, kc:kc + LANES] = rope(piece).astype(BF16)
            else:
                vc = col - A_Q_COLS - A_KV_COLS
                v_ref[:, vc:vc + LANES] = piece.astype(BF16)


def _attn_proj(x, nw, w, b, cos_a, sin_a, *, tm):
    t = x.shape[0]
    return pl.pallas_call(
        _attn_proj_kernel,
        grid=(t // tm,),
        in_specs=[_rows(tm, D_MODEL), _resident((1, D_MODEL)), _resident((D_MODEL, A_PROJ_COLS)),
                  _resident((1, A_PROJ_COLS)), _rows(tm, LANES), _rows(tm, LANES)],
        out_specs=[_rows(tm, A_Q_COLS), _rows(tm, A_KV_COLS), _rows(tm, A_KV_COLS)],
        out_shape=[jax.ShapeDtypeStruct((t, A_Q_COLS), BF16),
                   jax.ShapeDtypeStruct((t, A_KV_COLS), BF16),
                   jax.ShapeDtypeStruct((t, A_KV_COLS), BF16)],
        compiler_params=_params(1),
        name="attn_proj",
    )(x, nw, w, b, cos_a, sin_a)


def _attn_core_kernel(sink_ref, q_ref, kp_ref, kc_ref, vp_ref, vc_ref, o_ref):
    blk = pl.program_id(1)
    L = SEQ_CHUNK
    qi = lax.broadcasted_iota(jnp.int32, (L, 2 * L), 0)
    kj = lax.broadcasted_iota(jnp.int32, (L, 2 * L), 1)
    rel = qi + L - kj
    valid = (rel >= 0) & (rel < A_WINDOW) & ((kj >= L) | (blk > 0))
    low = lax.broadcasted_iota(jnp.int32, (2 * L, LANES), 1) < A_HEAD_DIM
    zero = jnp.zeros((2 * L, LANES), BF16)
    for g in range(A_KV_HEADS):
        kk = jnp.concatenate([kp_ref[:, g * LANES:(g + 1) * LANES], kc_ref[:, g * LANES:(g + 1) * LANES]], axis=0)
        vv = jnp.concatenate([vp_ref[:, g * LANES:(g + 1) * LANES], vc_ref[:, g * LANES:(g + 1) * LANES]], axis=0)
        k_half = (jnp.where(low, kk, zero), jnp.where(low, zero, kk))
        v_half = (jnp.where(low, vv, zero), jnp.where(low, zero, vv))
        for pair in range(2):
            col = 2 * g + pair
            q_pair = q_ref[:, col * LANES:(col + 1) * LANES]
            out = None
            for half in range(2):
                head = 2 * col + half
                sink = sink_ref[head]
                s = _dot_nt(q_pair, k_half[half])
                s = jnp.where(valid, s, -jnp.inf)
                mx = jnp.maximum(jnp.max(s, axis=-1, keepdims=True), sink)
                p = jnp.exp(s - mx)
                denom = jnp.sum(p, axis=-1, keepdims=True) + jnp.exp(sink - mx)
                probs = (p / denom).astype(BF16)
                part = _dot(probs, v_half[half])
                out = part if out is None else out + part
            o_ref[:, col * LANES:(col + 1) * LANES] = out.astype(BF16)


def _attn_core(q, kd, vd, sinks, *, batch, seq):
    L = SEQ_CHUNK
    nb = seq // L
    cur = lambda b, i: (b * nb + i, 0)
    prev = lambda b, i: (b * nb + jnp.maximum(i - 1, 0), 0)
    return pl.pallas_call(
        _attn_core_kernel,
        grid=(batch, nb),
        in_specs=[pl.BlockSpec(memory_space=pltpu.SMEM),
                  pl.BlockSpec((L, A_Q_COLS), cur),
                  pl.BlockSpec((L, A_KV_COLS), prev), pl.BlockSpec((L, A_KV_COLS), cur),
                  pl.BlockSpec((L, A_KV_COLS), prev), pl.BlockSpec((L, A_KV_COLS), cur)],
        out_specs=pl.BlockSpec((L, A_Q_COLS), cur),
        out_shape=jax.ShapeDtypeStruct((batch * seq, A_Q_COLS), BF16),
        compiler_params=_params(2),
        name="attn_core",
    )(sinks, q, kd, kd, vd, vd)


M_QK_COLS = M_HEADS * M_QK_DIM
M_V_COLS = M_HEADS * M_V_DIM
M_PROJ_COLS = 2 * M_QK_COLS + 2 * M_V_COLS + LANES
M_PROJ_CHUNK = 640
M_STATE_COLS = 2 * LANES


def _mlstm_proj_kernel(x_ref, nw_ref, w_ref, bg_ref, q_ref, k_ref, v_ref, o_ref, g_ref):
    h = _rms(x_ref[...], nw_ref[...]).astype(BF16)
    q_scale = M_QK_DIM ** -0.5
    bounds = (M_QK_COLS, 2 * M_QK_COLS, 2 * M_QK_COLS + M_V_COLS, 2 * M_QK_COLS + 2 * M_V_COLS)
    for c in range(M_PROJ_COLS // M_PROJ_CHUNK):
        lo = c * M_PROJ_CHUNK
        t = _dot(h, w_ref[:, lo:lo + M_PROJ_CHUNK])
        for j in range(M_PROJ_CHUNK // LANES):
            col = lo + j * LANES
            piece = t[:, j * LANES:(j + 1) * LANES]
            if col < bounds[0]:
                q_ref[:, col:col + LANES] = (piece * q_scale).astype(BF16)
            elif col < bounds[1]:
                k_ref[:, col - bounds[0]:col - bounds[0] + LANES] = piece.astype(BF16)
            elif col < bounds[2]:
                v_ref[:, col - bounds[1]:col - bounds[1] + LANES] = piece.astype(BF16)
            elif col < bounds[3]:
                o_ref[:, col - bounds[2]:col - bounds[2] + LANES] = piece
            else:
                g_ref[...] = piece + bg_ref[...]


def _mlstm_proj(x, nw, w, bg, *, tm):
    t = x.shape[0]
    return pl.pallas_call(
        _mlstm_proj_kernel,
        grid=(t // tm,),
        in_specs=[_rows(tm, D_MODEL), _resident((1, D_MODEL)), _resident((D_MODEL, M_PROJ_COLS)),
                  _resident((1, LANES))],
        out_specs=[_rows(tm, M_QK_COLS), _rows(tm, M_QK_COLS), _rows(tm, M_V_COLS),
                   _rows(tm, M_V_COLS), _rows(tm, LANES)],
        out_shape=[jax.ShapeDtypeStruct((t, M_QK_COLS), BF16),
                   jax.ShapeDtypeStruct((t, M_QK_COLS), BF16),
                   jax.ShapeDtypeStruct((t, M_V_COLS), BF16),
                   jax.ShapeDtypeStruct((t, M_V_COLS), F32),
                   jax.ShapeDtypeStruct((t, LANES), F32)],
        compiler_params=_params(1),
        name="mlstm_proj",
    )(x, nw, w, bg)


def _mlstm_core_kernel(q_ref, k_ref, v_ref, o_ref, g_ref, nw_ref, y_ref, c_sc, m_sc):
    L = SEQ_CHUNK
    H = M_HEADS

    @pl.when(pl.program_id(1) == 0)
    def _():
        c_sc[...] = jnp.zeros_like(c_sc)
        m_sc[...] = jnp.zeros_like(m_sc)

    gates = g_ref[...]
    gates = M_GATE_CAP * jnp.tanh(gates / M_GATE_CAP)
    lane = lax.broadcasted_iota(jnp.int32, (L, LANES), 1)
    log_f = jnp.where((lane >= H) & (lane < 2 * H), jax.nn.log_sigmoid(gates), 0.0)
    row = lax.broadcasted_iota(jnp.int32, (L, L), 0)
    col = lax.broadcasted_iota(jnp.int32, (L, L), 1)
    causal = row >= col
    b_col = jnp.dot(causal.astype(F32), log_f, preferred_element_type=F32,
                    precision=lax.Precision.HIGHEST)
    g_row = gates.T
    b_row = b_col.T
    ones_col = (lax.broadcasted_iota(jnp.int32, (L, LANES), 1) == 0).astype(BF16)
    low = lax.broadcasted_iota(jnp.int32, (L, LANES), 1) < M_QK_DIM
    nw = nw_ref[...]

    for h in range(H):
        bc = b_col[:, H + h:H + h + 1]
        br = b_row[H + h:H + h + 1, :]
        li_r = g_row[h:h + 1, :]
        li_c = gates[:, h:h + 1]
        m_prev = m_sc[h][:, :1]
        dmat = jnp.where(causal, bc - br + li_r, -jnp.inf)
        inter = bc + m_prev
        m_row = jnp.maximum(inter, jnp.max(dmat, axis=-1, keepdims=True))
        w_intra = jnp.exp(dmat - m_row)
        w_inter = jnp.exp(inter - m_row)

        pair = h // 2
        q_pair = q_ref[:, pair * LANES:(pair + 1) * LANES]
        k_pair = k_ref[:, pair * LANES:(pair + 1) * LANES]
        own = low if h % 2 == 0 else jnp.logical_not(low)
        k_own = jnp.where(own, k_pair, jnp.zeros_like(k_pair))
        s = _dot_nt(q_pair, k_own) * w_intra
        v_ext = jnp.concatenate([v_ref[:, h * LANES:(h + 1) * LANES], ones_col], axis=1)
        state = c_sc[h]
        tot = _dot(s.astype(BF16), v_ext) + w_inter * _dot(q_pair, state.astype(BF16))
        num = tot[:, :LANES]
        nq = tot[:, LANES:LANES + 1]
        h_out = num / jnp.maximum(jnp.abs(nq), jnp.exp(-m_row))
        h_out = h_out * lax.rsqrt(jnp.mean(h_out * h_out, axis=-1, keepdims=True) + EPS)
        gate_o = jax.nn.sigmoid(o_ref[:, h * LANES:(h + 1) * LANES])
        y_ref[:, h * LANES:(h + 1) * LANES] = (gate_o * (h_out * nw[:, h * LANES:(h + 1) * LANES])).astype(BF16)

        b_last = bc[L - 1:L, :]
        dec = b_last - bc + li_c
        m_new = jnp.maximum(b_last + m_prev, jnp.max(dec, axis=0, keepdims=True))
        wk = jnp.exp(dec - m_new)
        keep = jnp.exp(b_last + m_prev - m_new)
        v_scaled = (v_ext.astype(F32) * wk).astype(BF16)
        c_sc[h] = keep * state + _dot_tn(k_own, v_scaled)
        m_sc[h] = jnp.broadcast_to(m_new, (1, LANES))


def _mlstm_core(q, k, v, o, g, nw, *, batch, seq):
    L = SEQ_CHUNK
    nb = seq // L
    cur = lambda b, i: (b * nb + i, 0)
    return pl.pallas_call(
        _mlstm_core_kernel,
        grid=(batch, nb),
        in_specs=[pl.BlockSpec((L, M_QK_COLS), cur), pl.BlockSpec((L, M_QK_COLS), cur),
                  pl.BlockSpec((L, M_V_COLS), cur), pl.BlockSpec((L, M_V_COLS), cur),
                  pl.BlockSpec((L, LANES), cur), _resident((1, M_V_COLS))],
        out_specs=pl.BlockSpec((L, M_V_COLS), cur),
        out_shape=jax.ShapeDtypeStruct((batch * seq, M_V_COLS), BF16),
        scratch_shapes=[pltpu.VMEM((M_HEADS, LANES, M_STATE_COLS), F32),
                        pltpu.VMEM((M_HEADS, 1, LANES), F32)],
        compiler_params=_params(2),
        name="mlstm_core",
    )(q, k, v, o, g, nw)


R_QK_COLS = R_HEADS * R_QK_DIM
R_V_COLS = R_HEADS * R_V_DIM
R_PROJ_COLS = 2 * R_QK_COLS + 2 * R_V_COLS
R_PROJ_CHUNK = 512


def _ret_proj_kernel(x_ref, nw_ref, w_ref, cos_ref, sin_ref, q_ref, k_ref, v_ref, g_ref):
    h = _rms(x_ref[...], nw_ref[...]).astype(BF16)
    cos = cos_ref[...]
    sin = sin_ref[...]
    k_scale = R_QK_DIM ** -0.5
    half = R_QK_DIM // 2

    def rotate(t, ref, lo, scale):
        for hh in range(t.shape[1] // R_QK_DIM):
            x1 = t[:, hh * R_QK_DIM:hh * R_QK_DIM + half]
            x2 = t[:, hh * R_QK_DIM + half:(hh + 1) * R_QK_DIM]
            base = lo + hh * R_QK_DIM
            ref[:, base:base + half] = ((x1 * cos - x2 * sin) * scale).astype(BF16)
            ref[:, base + half:base + R_QK_DIM] = ((x2 * cos + x1 * sin) * scale).astype(BF16)

    for c in range(R_PROJ_COLS // R_PROJ_CHUNK):
        lo = c * R_PROJ_CHUNK
        t = _dot(h, w_ref[:, lo:lo + R_PROJ_CHUNK])
        if lo < R_QK_COLS:
            rotate(t, q_ref, lo, 1.0)
        elif lo < 2 * R_QK_COLS:
            rotate(t, k_ref, lo - R_QK_COLS, k_scale)
        elif lo < 2 * R_QK_COLS + R_V_COLS:
            v_ref[:, lo - 2 * R_QK_COLS:lo - 2 * R_QK_COLS + R_PROJ_CHUNK] = t.astype(BF16)
        else:
            g_ref[:, lo - 2 * R_QK_COLS - R_V_COLS:lo - 2 * R_QK_COLS - R_V_COLS + R_PROJ_CHUNK] = t


def _ret_proj(x, nw, w, cos_r, sin_r, *, tm):
    t = x.shape[0]
    return pl.pallas_call(
        _ret_proj_kernel,
        grid=(t // tm,),
        in_specs=[_rows(tm, D_MODEL), _resident((1, D_MODEL)), _resident((D_MODEL, R_PROJ_COLS)),
                  _rows(tm, LANES), _rows(tm, LANES)],
        out_specs=[_rows(tm, R_QK_COLS), _rows(tm, R_QK_COLS), _rows(tm, R_V_COLS), _rows(tm, R_V_COLS)],
        out_shape=[jax.ShapeDtypeStruct((t, R_QK_COLS), BF16),
                   jax.ShapeDtypeStruct((t, R_QK_COLS), BF16),
                   jax.ShapeDtypeStruct((t, R_V_COLS), BF16),
                   jax.ShapeDtypeStruct((t, R_V_COLS), F32)],
        compiler_params=_params(1),
        name="ret_proj",
    )(x, nw, w, cos_r, sin_r)


def _ret_core_kernel(cdec_ref, q_ref, k_ref, v_ref, g_ref, nw_ref, dmask_ref, qdec_ref, kdec_ref, y_ref, s_sc):
    @pl.when(pl.program_id(1) == 0)
    def _():
        s_sc[...] = jnp.zeros_like(s_sc)

    nw = nw_ref[...]
    for h in range(R_HEADS):
        qh = q_ref[:, h * R_QK_DIM:(h + 1) * R_QK_DIM]
        kh = k_ref[:, h * R_QK_DIM:(h + 1) * R_QK_DIM]
        vh = v_ref[:, h * R_V_DIM:(h + 1) * R_V_DIM]
        state = s_sc[h]
        scores = _dot_nt(qh, kh) * dmask_ref[h]
        inner = _dot(scores.astype(BF16), vh)
        cross = qdec_ref[:, h:h + 1] * _dot(qh, state.astype(BF16))
        y = inner + cross
        v_scaled = (vh.astype(F32) * kdec_ref[:, h:h + 1]).astype(BF16)
        s_sc[h] = cdec_ref[h] * state + _dot_tn(kh, v_scaled)
        mu = jnp.mean(y, axis=-1, keepdims=True)
        yc = y - mu
        yn = yc * lax.rsqrt(jnp.mean(yc * yc, axis=-1, keepdims=True) + EPS)
        gate = g_ref[:, h * R_V_DIM:(h + 1) * R_V_DIM]
        out = (gate * jax.nn.sigmoid(gate)) * (yn * nw[:, h * R_V_DIM:(h + 1) * R_V_DIM])
        y_ref[:, h * R_V_DIM:(h + 1) * R_V_DIM] = out.astype(BF16)


def _ret_core(q, k, v, g, nw, dmask, qdec, kdec, cdec, *, batch, seq):
    L = SEQ_CHUNK
    nb = seq // L
    cur = lambda b, i: (b * nb + i, 0)
    return pl.pallas_call(
        _ret_core_kernel,
        grid=(batch, nb),
        in_specs=[pl.BlockSpec(memory_space=pltpu.SMEM),
                  pl.BlockSpec((L, R_QK_COLS), cur), pl.BlockSpec((L, R_QK_COLS), cur),
                  pl.BlockSpec((L, R_V_COLS), cur), pl.BlockSpec((L, R_V_COLS), cur),
                  _resident((1, R_V_COLS)), _resident((R_HEADS, L, L)),
                  _resident((L, LANES)), _resident((L, LANES))],
        out_specs=pl.BlockSpec((L, R_V_COLS), cur),
        out_shape=jax.ShapeDtypeStruct((batch * seq, R_V_COLS), BF16),
        scratch_shapes=[pltpu.VMEM((R_HEADS, R_QK_DIM, R_V_DIM), F32)],
        compiler_params=_params(2),
        name="ret_core",
    )(cdec, q, k, v, g, nw, dmask, qdec, kdec)


def _rope_constants():
    inv = 1.0 / (ROPE_THETA ** (jnp.arange(0, A_ROPE_DIM, 2, dtype=F32) / A_ROPE_DIM))
    lane = np.arange(LANES) % A_HEAD_DIM
    idx = jnp.asarray(lane % (A_ROPE_DIM // 2))
    rot = jnp.asarray(lane < A_ROPE_DIM)
    inv_a = jnp.where(rot, inv[idx], 0.0).reshape(1, LANES)
    sgn = np.where(lane < A_ROPE_DIM // 2, -1.0, np.where(lane < A_ROPE_DIM, 1.0, 0.0))
    sgn_a = jnp.asarray(sgn, F32).reshape(1, LANES)
    inv_r = (1.0 / (R_THETA ** jnp.linspace(0.0, 1.0, R_QK_DIM // 2, dtype=F32))).reshape(1, LANES)
    return inv_a, sgn_a, inv_r


def _retention_constants():
    L = SEQ_CHUNK
    log_gamma = jnp.log(1.0 - 2.0 ** (-5.0 - jnp.arange(R_HEADS, dtype=F32)))
    j = jnp.arange(L, dtype=F32)
    diff = j[:, None] - j[None, :]
    causal = diff >= 0
    dmask = jnp.where(causal, jnp.exp(jnp.where(causal, diff, 0.0) * log_gamma[:, None, None]), 0.0)
    q_decay = jnp.exp((j + 1.0) * log_gamma[:, None])
    k_decay = jnp.exp((L - 1.0 - j) * log_gamma[:, None])
    chunk_decay = jnp.exp(L * log_gamma)
    pad = lambda a: jnp.pad(a.T, ((0, 0), (0, LANES - R_HEADS)))
    return dmask, pad(q_decay), pad(k_decay), chunk_decay


def _attn_weight_layout(w_qkv, b_qkv):
    nq = A_Q_COLS
    nkv = A_KV_HEADS * A_HEAD_DIM

    def dup(m):
        lead = m.shape[:-1]
        m = m.reshape(lead + (A_KV_HEADS, 1, A_HEAD_DIM))
        return jnp.broadcast_to(m, lead + (A_KV_HEADS, 2, A_HEAD_DIM)).reshape(lead + (A_KV_COLS,))

    def layout(m):
        return jnp.concatenate([m[..., :nq], dup(m[..., nq:nq + nkv]), dup(m[..., nq + nkv:])], axis=-1)

    return layout(w_qkv).astype(BF16), layout(b_qkv).reshape(1, A_PROJ_COLS)


def _ret_weight_layout(w_in):
    head = np.concatenate([np.arange(0, R_QK_DIM, 2), np.arange(1, R_QK_DIM, 2)])
    qk = np.concatenate([h * R_QK_DIM + head for h in range(2 * R_HEADS)])
    perm = np.concatenate([qk, np.arange(2 * R_QK_COLS, R_PROJ_COLS)])
    return w_in[:, perm].astype(BF16)


def _mlstm_weight_layout(w_in, b_if):
    pad = LANES - 2 * M_HEADS
    return (jnp.pad(w_in, ((0, 0), (0, pad))).astype(BF16),
            jnp.pad(b_if, (0, pad)).reshape(1, LANES))


def kernel(x, positions, norm_w, final_norm_w, ffn_w_in, ffn_w_out, attn_w_qkv, attn_b_qkv, attn_w_o, attn_b_o,
           attn_sinks, mlstm_w_in, mlstm_b_if, mlstm_norm_w, mlstm_w_out, ret_w_in, ret_norm_w, ret_w_out):
    batch, seq, d = x.shape
    t = batch * seq
    tm = min(ROW_TILE, t)
    assert d == D_MODEL and t % tm == 0 and seq % SEQ_CHUNK == 0

    xf = x.reshape(t, d)
    pos_b = jnp.broadcast_to(positions.astype(F32).reshape(t, 1), (t, LANES))
    inv_a, sgn_a, inv_r = _rope_constants()
    cos_a, sin_a, cos_r, sin_r = _rope_tables(pos_b, inv_a, sgn_a, inv_r, tm)
    dmask, qdec, kdec, cdec = _retention_constants()
    zero_bias = jnp.zeros((1, D_MODEL), F32)

    for i in range(DEPTH):
        kind, slot = i % N_MIXERS, i // N_MIXERS
        xf = _ffn(xf, norm_w[i, 0].reshape(1, d), ffn_w_in[i, 0].astype(BF16), ffn_w_out[i, 0].astype(BF16), tm=tm)
        nw = norm_w[i, 1].reshape(1, d)
        if kind == 0:
            w, b = _attn_weight_layout(attn_w_qkv[slot], attn_b_qkv[slot])
            q, kd, vd = _attn_proj(xf, nw, w, b, cos_a, sin_a, tm=tm)
            o = _attn_core(q, kd, vd, attn_sinks[slot], batch=batch, seq=seq)
            xf = _out_proj(xf, o, attn_w_o[slot].astype(BF16), attn_b_o[slot].reshape(1, d), tm=tm)
        elif kind == 1:
            w, bg = _mlstm_weight_layout(mlstm_w_in[slot], mlstm_b_if[slot])
            q, k, v, o, g = _mlstm_proj(xf, nw, w, bg, tm=tm)
            y = _mlstm_core(q, k, v, o, g, mlstm_norm_w[slot].reshape(1, M_V_COLS), batch=batch, seq=seq)
            xf = _out_proj(xf, y, mlstm_w_out[slot].astype(BF16), zero_bias, tm=tm)
        else:
            w = _ret_weight_layout(ret_w_in[slot])
            q, k, v, g = _ret_proj(xf, nw, w, cos_r, sin_r, tm=tm)
            y = _ret_core(q, k, v, g, ret_norm_w[slot].reshape(1, R_V_COLS), dmask, qdec, kdec, cdec,
                          batch=batch, seq=seq)
            xf = _out_proj(xf, y, ret_w_out[slot].astype(BF16), zero_bias, tm=tm)
        last = i == DEPTH - 1
        xf = _ffn(xf, norm_w[i, 2].reshape(1, d), ffn_w_in[i, 1].astype(BF16), ffn_w_out[i, 1].astype(BF16),
                  final_norm_w.reshape(1, d) if last else None, tm=tm)
    return xf.reshape(batch, seq, d)
```

```python
import functools

import jax
import jax.numpy as jnp
import numpy as np
from jax import lax
from jax.experimental import pallas as pl
from jax.experimental.pallas import tpu as pltpu

F32 = jnp.float32
BF16 = jnp.bfloat16

D_MODEL = 1024
DEPTH = 4
N_MIXERS = 3
EPS = 1e-6
D_FF = 2816
A_HEADS = 16
A_KV_HEADS = 4
A_HEAD_DIM = 64
A_WINDOW = 128
A_ROPE_DIM = 16
ROPE_THETA = 500000.0
M_HEADS = 8
M_QK_DIM = 64
M_V_DIM = 128
M_GATE_CAP = 15.0
R_HEADS = 4
R_QK_DIM = 256
R_V_DIM = 512
R_THETA = 10000.0

LANES = 128
SUBLANES = 8
VMEM_LIMIT = 56 * 1024 * 1024

ROW_TILE = 512
FF_CHUNK = 256
A_CHUNK = A_WINDOW
M_CHUNK = 128
R_CHUNK = 256


def _params(n_axes):
    return pltpu.CompilerParams(
        dimension_semantics=("arbitrary",) * n_axes,
        vmem_limit_bytes=VMEM_LIMIT,
    )


def _resident(shape):
    nd = len(shape)
    return pl.BlockSpec(shape, lambda *_: (0,) * nd, pipeline_mode=pl.Buffered(1))


def _rows(tm, width):
    return pl.BlockSpec((tm, width), lambda i: (i, 0))


def _cols(height, tm):
    return pl.BlockSpec((height, tm), lambda i: (0, i))


def _rms(x, w):
    ms = jnp.mean(x * x, axis=-1, keepdims=True)
    return x * lax.rsqrt(ms + EPS) * w


def _dot(a, b):
    return jnp.dot(a, b, preferred_element_type=F32)


def _dot_nt(a, b):
    return lax.dot_general(a, b, (((1,), (1,)), ((), ())), preferred_element_type=F32)


def _tables_kernel(pos_ref, inv_a_ref, sgn_a_ref, inv_r_ref,
                   cos_a, sin_a, cos_r, sin_r, cos_at, sin_at, cos_rt, sin_rt):
    half = A_ROPE_DIM // 2
    pos = pos_ref[...]
    ang_a = pos * inv_a_ref[...]
    ca = jnp.cos(ang_a)
    sa = jnp.sin(ang_a) * sgn_a_ref[...]
    cos_a[...] = ca
    sin_a[...] = sa
    cos_at[...] = ca.T[:half]
    sin_at[...] = sa.T[half:2 * half]
    ang_r = pos * inv_r_ref[...]
    cr = jnp.cos(ang_r)
    sr = jnp.sin(ang_r)
    cos_r[...] = cr
    sin_r[...] = sr
    cos_rt[...] = cr.T
    sin_rt[...] = sr.T


def _rope_tables(pos_b, inv_a, sgn_a, inv_r, tm):
    t = pos_b.shape[0]
    half = A_ROPE_DIM // 2
    tok = jax.ShapeDtypeStruct((t, LANES), F32)
    return pl.pallas_call(
        _tables_kernel,
        grid=(t // tm,),
        in_specs=[_rows(tm, LANES), _resident((1, LANES)), _resident((1, LANES)), _resident((1, LANES))],
        out_specs=[_rows(tm, LANES)] * 4 + [_cols(half, tm), _cols(half, tm), _cols(LANES, tm), _cols(LANES, tm)],
        out_shape=[tok] * 4 + [jax.ShapeDtypeStruct((half, t), F32)] * 2 + [jax.ShapeDtypeStruct((LANES, t), F32)] * 2,
        compiler_params=_params(1),
        name="rope_tables",
    )(pos_b, inv_a, sgn_a, inv_r)


def _ffn_kernel(*refs, mix_in, final_norm):
    refs = list(refs)
    x_ref = refs.pop(0)
    if mix_in:
        y_ref, wo_ref, bo_ref = refs.pop(0), refs.pop(0), refs.pop(0)
    nw_ref, win_ref, wout_ref = refs.pop(0), refs.pop(0), refs.pop(0)
    fw_ref = refs.pop(0) if final_norm else None
    (o_ref,) = refs

    x = x_ref[...]
    if mix_in:
        x = x + (_dot(y_ref[...], wo_ref[...]) + bo_ref[...])
    h = _rms(x, nw_ref[...]).astype(BF16)
    acc = jnp.zeros_like(x)
    for c in range(D_FF // FF_CHUNK):
        lo = c * FF_CHUNK
        g = _dot(h, win_ref[:, lo:lo + FF_CHUNK])
        u = _dot(h, win_ref[:, D_FF + lo:D_FF + lo + FF_CHUNK])
        a = (g * jax.nn.sigmoid(g) * u).astype(BF16)
        acc = acc + _dot(a, wout_ref[lo:lo + FF_CHUNK, :])
    y = x + 0.5 * acc
    if final_norm:
        y = _rms(y, fw_ref[...])
    o_ref[...] = y


def _ffn(x, nw, w_in, w_out, *, tm, mix=None, final_w=None):
    t = x.shape[0]
    in_specs = [_rows(tm, D_MODEL)]
    args = [x]
    if mix is not None:
        y, w_o, b_o = mix
        kdim = y.shape[1]
        in_specs += [_rows(tm, kdim), _resident((kdim, D_MODEL)), _resident((1, D_MODEL))]
        args += [y, w_o, b_o]
    in_specs += [_resident((1, D_MODEL)), _resident((D_MODEL, 2 * D_FF)), _resident((D_FF, D_MODEL))]
    args += [nw, w_in, w_out]
    if final_w is not None:
        in_specs.append(_resident((1, D_MODEL)))
        args.append(final_w)
    return pl.pallas_call(
        functools.partial(_ffn_kernel, mix_in=mix is not None, final_norm=final_w is not None),
        grid=(t // tm,),
        in_specs=in_specs,
        out_specs=_rows(tm, D_MODEL),
        out_shape=jax.ShapeDtypeStruct((t, D_MODEL), F32),
        compiler_params=_params(1),
        name="ffn_mix" if mix is not None else "ffn",
    )(*args)


A_Q_COLS = A_HEADS * A_HEAD_DIM
A_K_ROWS = A_KV_HEADS * A_HEAD_DIM
A_V_COLS = A_KV_HEADS * LANES
A_PROJ_CHUNK = 512


def _attn_proj_kernel(x_ref, nw_ref, w_ref, b_ref, wkt_ref, bkt_ref, cos_ref, sin_ref, cost_ref, sint_ref,
                      q_ref, kt_ref, v_ref):
    tm = x_ref.shape[0]
    half = A_ROPE_DIM // 2
    h = _rms(x_ref[...], nw_ref[...]).astype(BF16)
    cos = cos_ref[...]
    sin = sin_ref[...]
    lane = lax.broadcasted_iota(jnp.int32, (tm, LANES), 1) & (A_HEAD_DIM - 1)
    upper = (lane >= half) & (lane < A_ROPE_DIM)
    q_scale = A_HEAD_DIM ** -0.5

    def rope(t):
        partner = jnp.where(upper, pltpu.roll(t, half, 1), pltpu.roll(t, LANES - half, 1))
        return t * cos + partner * sin

    for c in range((A_Q_COLS + A_V_COLS) // A_PROJ_CHUNK):
        lo = c * A_PROJ_CHUNK
        t = _dot(h, w_ref[:, lo:lo + A_PROJ_CHUNK]) + b_ref[:, lo:lo + A_PROJ_CHUNK]
        if lo < A_Q_COLS:
            for j in range(A_PROJ_CHUNK // LANES):
                piece = t[:, j * LANES:(j + 1) * LANES]
                q_ref[:, lo + j * LANES:lo + (j + 1) * LANES] = (rope(piece) * q_scale).astype(BF16)
        else:
            v_ref[...] = t.astype(BF16)

    kt = _dot_nt(wkt_ref[...], h) + bkt_ref[...]
    cos_t = cost_ref[...]
    sin_t = sint_ref[...]
    for g in range(A_KV_HEADS):
        base = g * A_HEAD_DIM
        x1 = kt[base:base + half]
        x2 = kt[base + half:base + 2 * half]
        rot = jnp.concatenate([x1 * cos_t - x2 * sin_t, x2 * cos_t + x1 * sin_t,
                               kt[base + 2 * half:base + A_HEAD_DIM]], axis=0)
        kt_ref[base:base + A_HEAD_DIM, :] = rot.astype(BF16)


def _attn_proj(x, nw, w, b, wkt, bkt, cos_a, sin_a, cos_at, sin_at, *, tm):
    t = x.shape[0]
    half = A_ROPE_DIM // 2
    ncols = A_Q_COLS + A_V_COLS
    return pl.pallas_call(
        _attn_proj_kernel,
        grid=(t // tm,),
        in_specs=[_rows(tm, D_MODEL), _resident((1, D_MODEL)), _resident((D_MODEL, ncols)), _resident((1, ncols)),
                  _resident((A_K_ROWS, D_MODEL)), _resident((A_K_ROWS, tm)),
                  _rows(tm, LANES), _rows(tm, LANES), _cols(half, tm), _cols(half, tm)],
        out_specs=[_rows(tm, A_Q_COLS), _cols(A_K_ROWS, tm), _rows(tm, A_V_COLS)],
        out_shape=[jax.ShapeDtypeStruct((t, A_Q_COLS), BF16),
                   jax.ShapeDtypeStruct((A_K_ROWS, t), BF16),
                   jax.ShapeDtypeStruct((t, A_V_COLS), BF16)],
        compiler_params=_params(1),
        name="attn_proj",
    )(x, nw, w, b, wkt, bkt, cos_a, sin_a, cos_at, sin_at)


def _attn_core_kernel(sink_ref, q_ref, ktp_ref, ktc_ref, vp_ref, vc_ref, o_ref):
    L = A_CHUNK
    blk = pl.program_id(1)
    qi = lax.broadcasted_iota(jnp.int32, (L, L), 0)
    kj = lax.broadcasted_iota(jnp.int32, (L, L), 1)
    from_prev = kj > qi
    prev_bias = jnp.where(blk > 0, 0.0, -jnp.inf).astype(F32)
    low = lax.broadcasted_iota(jnp.int32, (2 * L, LANES), 1) < A_HEAD_DIM
    zero_v = jnp.zeros((2 * L, LANES), BF16)
    zero_k = jnp.zeros((A_HEAD_DIM, 2 * L), BF16)
    for g in range(A_KV_HEADS):
        kt = jnp.concatenate([ktp_ref[g * A_HEAD_DIM:(g + 1) * A_HEAD_DIM, :],
                              ktc_ref[g * A_HEAD_DIM:(g + 1) * A_HEAD_DIM, :]], axis=1)
        vv = jnp.concatenate([vp_ref[:, g * LANES:(g + 1) * LANES], vc_ref[:, g * LANES:(g + 1) * LANES]], axis=0)
        kt_half = (jnp.concatenate([kt, zero_k], axis=0), jnp.concatenate([zero_k, kt], axis=0))
        v_half = (jnp.where(low, vv, zero_v), jnp.where(low, zero_v, vv))
        q2 = jnp.concatenate([q_ref[:, (2 * g) * LANES:(2 * g + 1) * LANES],
                              q_ref[:, (2 * g + 1) * LANES:(2 * g + 2) * LANES]], axis=0)
        acc = None
        for half in range(2):
            s = _dot(q2, kt_half[half])
            rows = []
            for pr in range(2):
                head = 2 * (2 * g + pr) + half
                sink = sink_ref[head]
                s_prev = s[pr * L:(pr + 1) * L, :L]
                s_own = s[pr * L:(pr + 1) * L, L:]
                sc = jnp.where(from_prev, s_prev + prev_bias, s_own)
                mx = jnp.maximum(jnp.max(sc, axis=-1, keepdims=True), sink)
                p = jnp.exp(sc - mx)
                denom = jnp.sum(p, axis=-1, keepdims=True) + jnp.exp(sink - mx)
                pn = p * (1.0 / denom)
                zero = jnp.zeros_like(pn)
                rows.append(jnp.concatenate([jnp.where(from_prev, pn, zero), jnp.where(from_prev, zero, pn)],
                                            axis=1).astype(BF16))
            part = _dot(jnp.concatenate(rows, axis=0), v_half[half])
            acc = part if acc is None else acc + part
        o_ref[:, (2 * g) * LANES:(2 * g + 1) * LANES] = acc[:L].astype(BF16)
        o_ref[:, (2 * g + 1) * LANES:(2 * g + 2) * LANES] = acc[L:].astype(BF16)


def _attn_core(q, kt, vd, sinks, *, batch, seq):
    L = A_CHUNK
    nb = seq // L
    cur = lambda b, i: (b * nb + i, 0)
    prev = lambda b, i: (b * nb + jnp.maximum(i - 1, 0), 0)
    cur_t = lambda b, i: (0, b * nb + i)
    prev_t = lambda b, i: (0, b * nb + jnp.maximum(i - 1, 0))
    return pl.pallas_call(
        _attn_core_kernel,
        grid=(batch, nb),
        in_specs=[pl.BlockSpec(memory_space=pltpu.SMEM),
                  pl.BlockSpec((L, A_Q_COLS), cur),
                  pl.BlockSpec((A_K_ROWS, L), prev_t), pl.BlockSpec((A_K_ROWS, L), cur_t),
                  pl.BlockSpec((L, A_V_COLS), prev), pl.BlockSpec((L, A_V_COLS), cur)],
        out_specs=pl.BlockSpec((L, A_Q_COLS), cur),
        out_shape=jax.ShapeDtypeStruct((batch * seq, A_Q_COLS), BF16),
        compiler_params=_params(2),
        name="attn_core",
    )(sinks, q, kt, kt, vd, vd)


M_QK_COLS = M_HEADS * M_QK_DIM
M_V_COLS = M_HEADS * M_V_DIM
M_PROJ_COLS = M_QK_COLS + 2 * M_V_COLS + LANES
M_PROJ_CHUNK = 512
M_STATE_COLS = 2 * LANES


def _mlstm_proj_kernel(x_ref, nw_ref, w_ref, bg_ref, wkt_ref, q_ref, kt_ref, v_ref, o_ref, g_ref):
    h = _rms(x_ref[...], nw_ref[...]).astype(BF16)
    q_ref[...] = (_dot(h, w_ref[:, :M_QK_COLS]) * (M_QK_DIM ** -0.5)).astype(BF16)
    for c in range(M_V_COLS // M_PROJ_CHUNK):
        lo = c * M_PROJ_CHUNK
        v_ref[:, lo:lo + M_PROJ_CHUNK] = _dot(h, w_ref[:, M_QK_COLS + lo:M_QK_COLS + lo + M_PROJ_CHUNK]).astype(BF16)
        o_lo = M_QK_COLS + M_V_COLS + lo
        o_ref[:, lo:lo + M_PROJ_CHUNK] = _dot(h, w_ref[:, o_lo:o_lo + M_PROJ_CHUNK])
    g_lo = M_QK_COLS + 2 * M_V_COLS
    g_ref[...] = _dot(h, w_ref[:, g_lo:g_lo + LANES]) + bg_ref[...]
    kt_ref[...] = _dot_nt(wkt_ref[...], h).astype(BF16)


def _mlstm_proj(x, nw, w, bg, wkt, *, tm):
    t = x.shape[0]
    return pl.pallas_call(
        _mlstm_proj_kernel,
        grid=(t // tm,),
        in_specs=[_rows(tm, D_MODEL), _resident((1, D_MODEL)), _resident((D_MODEL, M_PROJ_COLS)),
                  _resident((1, LANES)), _resident((M_QK_COLS, D_MODEL))],
        out_specs=[_rows(tm, M_QK_COLS), _cols(M_QK_COLS, tm), _rows(tm, M_V_COLS),
                   _rows(tm, M_V_COLS), _rows(tm, LANES)],
        out_shape=[jax.ShapeDtypeStruct((t, M_QK_COLS), BF16),
                   jax.ShapeDtypeStruct((M_QK_COLS, t), BF16),
                   jax.ShapeDtypeStruct((t, M_V_COLS), BF16),
                   jax.ShapeDtypeStruct((t, M_V_COLS), F32),
                   jax.ShapeDtypeStruct((t, LANES), F32)],
        compiler_params=_params(1),
        name="mlstm_proj",
    )(x, nw, w, bg, wkt)


def _mlstm_core_kernel(q_ref, kt_ref, v_ref, o_ref, g_ref, nw_ref, y_ref, c_sc, m_sc):
    L = M_CHUNK
    H = M_HEADS

    @pl.when(pl.program_id(1) == 0)
    def _():
        c_sc[...] = jnp.zeros_like(c_sc)
        m_sc[...] = jnp.zeros_like(m_sc)

    gates = g_ref[...]
    gates = M_GATE_CAP * jnp.tanh(gates / M_GATE_CAP)
    lane = lax.broadcasted_iota(jnp.int32, (L, LANES), 1)
    log_f = jnp.where((lane >= H) & (lane < 2 * H), jax.nn.log_sigmoid(gates), 0.0)
    row = lax.broadcasted_iota(jnp.int32, (L, L), 0)
    col = lax.broadcasted_iota(jnp.int32, (L, L), 1)
    causal = row >= col
    b_col = jnp.dot(causal.astype(F32), log_f, preferred_element_type=F32,
                    precision=lax.Precision.HIGHEST)
    g_row = gates.T
    b_row = b_col.T
    ones_col = (lane == 0).astype(BF16)
    zero_k = jnp.zeros((M_QK_DIM, L), BF16)
    nw = nw_ref[...]

    st = [dict() for _ in range(H)]

    def load_and_score(h):
        d = st[h]
        d["q"] = q_ref[:, (h // 2) * LANES:(h // 2 + 1) * LANES]
        kt_h = kt_ref[h * M_QK_DIM:(h + 1) * M_QK_DIM, :]
        d["kt"] = jnp.concatenate([kt_h, zero_k] if h % 2 == 0 else [zero_k, kt_h], axis=0)
        d["v"] = jnp.concatenate([v_ref[:, h * LANES:(h + 1) * LANES], ones_col], axis=1)
        d["state"] = c_sc[h]
        d["s"] = _dot(d["q"], d["kt"])
        d["qc"] = _dot(d["q"], d["state"].astype(BF16))

    def gating(h):
        d = st[h]
        bc = b_col[:, H + h:H + h + 1]
        d["br"] = b_row[H + h:H + h + 1, :]
        d["src"] = g_row[h:h + 1, :] - d["br"]
        d["m_prev"] = m_sc[h][:, :1]
        dmat = jnp.where(causal, bc + d["src"], -jnp.inf)
        inter = bc + d["m_prev"]
        d["m_row"] = jnp.maximum(inter, jnp.max(dmat, axis=-1, keepdims=True))
        d["w_intra"] = jnp.exp(dmat - d["m_row"])
        d["w_inter"] = jnp.exp(inter - d["m_row"])

    def mix(h):
        d = st[h]
        d["tot"] = _dot((d["s"] * d["w_intra"]).astype(BF16), d["v"]) + d["w_inter"] * d["qc"]

    def emit(h):
        d = st[h]
        num = d["tot"][:, :LANES]
        nq = d["tot"][:, LANES:LANES + 1]
        h_out = num * (1.0 / jnp.maximum(jnp.abs(nq), jnp.exp(-d["m_row"])))
        h_out = h_out * lax.rsqrt(jnp.mean(h_out * h_out, axis=-1, keepdims=True) + EPS)
        gate_o = jax.nn.sigmoid(o_ref[:, h * LANES:(h + 1) * LANES])
        y_ref[:, h * LANES:(h + 1) * LANES] = (gate_o * (h_out * nw[:, h * LANES:(h + 1) * LANES])).astype(BF16)

    def update(h):
        d = st[h]
        b_last = d["br"][:, L - 1:L]
        dec = b_last + d["src"]
        m_new = jnp.maximum(b_last + d["m_prev"], jnp.max(dec, axis=-1, keepdims=True))
        wk = jnp.exp(dec - m_new)
        keep = jnp.exp(b_last + d["m_prev"] - m_new)
        kt_scaled = (d["kt"].astype(F32) * wk).astype(BF16)
        c_sc[h] = keep * d["state"] + _dot(kt_scaled, d["v"])
        m_sc[h] = jnp.broadcast_to(m_new, (1, LANES))

    for h in range(H):
        gating(h)
        load_and_score(h)
        mix(h)
        emit(h)
        update(h)


def _mlstm_core(q, kt, v, o, g, nw, *, batch, seq):
    L = M_CHUNK
    nb = seq // L
    cur = lambda b, i: (b * nb + i, 0)
    cur_t = lambda b, i: (0, b * nb + i)
    return pl.pallas_call(
        _mlstm_core_kernel,
        grid=(batch, nb),
        in_specs=[pl.BlockSpec((L, M_QK_COLS), cur), pl.BlockSpec((M_QK_COLS, L), cur_t),
                  pl.BlockSpec((L, M_V_COLS), cur), pl.BlockSpec((L, M_V_COLS), cur),
                  pl.BlockSpec((L, LANES), cur), _resident((1, M_V_COLS))],
        out_specs=pl.BlockSpec((L, M_V_COLS), cur),
        out_shape=jax.ShapeDtypeStruct((batch * seq, M_V_COLS), BF16),
        scratch_shapes=[pltpu.VMEM((M_HEADS, LANES, M_STATE_COLS), F32),
                        pltpu.VMEM((M_HEADS, 1, LANES), F32)],
        compiler_params=_params(2),
        name="mlstm_core",
    )(q, kt, v, o, g, nw)


R_QK_COLS = R_HEADS * R_QK_DIM
R_V_COLS = R_HEADS * R_V_DIM
R_PROJ_COLS = R_QK_COLS + 2 * R_V_COLS
R_PROJ_CHUNK = 512


def _ret_proj_kernel(x_ref, nw_ref, w_ref, wkt_ref, cos_ref, sin_ref, cost_ref, sint_ref, qdec_ref, kdec_ref,
                     q_ref, kt_ref, v_ref, g_ref):
    h = _rms(x_ref[...], nw_ref[...]).astype(BF16)
    cos = cos_ref[...]
    sin = sin_ref[...]
    half = R_QK_DIM // 2
    for c in range(R_QK_COLS // R_PROJ_CHUNK):
        lo = c * R_PROJ_CHUNK
        t = _dot(h, w_ref[:, lo:lo + R_PROJ_CHUNK])
        for hh in range(R_PROJ_CHUNK // R_QK_DIM):
            head = lo // R_QK_DIM + hh
            dec = qdec_ref[head]
            x1 = t[:, hh * R_QK_DIM:hh * R_QK_DIM + half]
            x2 = t[:, hh * R_QK_DIM + half:(hh + 1) * R_QK_DIM]
            base = head * R_QK_DIM
            q_ref[:, base:base + half] = ((x1 * cos - x2 * sin) * dec).astype(BF16)
            q_ref[:, base + half:base + R_QK_DIM] = ((x2 * cos + x1 * sin) * dec).astype(BF16)
    for c in range(2 * R_V_COLS // R_PROJ_CHUNK):
        lo = c * R_PROJ_CHUNK
        t = _dot(h, w_ref[:, R_QK_COLS + lo:R_QK_COLS + lo + R_PROJ_CHUNK])
        if lo < R_V_COLS:
            v_ref[:, lo:lo + R_PROJ_CHUNK] = t.astype(BF16)
        else:
            g_ref[:, lo - R_V_COLS:lo - R_V_COLS + R_PROJ_CHUNK] = t
    cos_t = cost_ref[...]
    sin_t = sint_ref[...]
    for head in range(R_HEADS):
        base = head * R_QK_DIM
        kt = _dot_nt(wkt_ref[base:base + R_QK_DIM, :], h)
        dec = kdec_ref[head:head + 1, :]
        x1 = kt[:half]
        x2 = kt[half:]
        kt_ref[base:base + half, :] = ((x1 * cos_t - x2 * sin_t) * dec).astype(BF16)
        kt_ref[base + half:base + R_QK_DIM, :] = ((x2 * cos_t + x1 * sin_t) * dec).astype(BF16)


def _ret_proj(x, nw, w, wkt, cos_r, sin_r, cos_rt, sin_rt, qdec, kdec, *, tm):
    t = x.shape[0]
    return pl.pallas_call(
        _ret_proj_kernel,
        grid=(t // tm,),
        in_specs=[_rows(tm, D_MODEL), _resident((1, D_MODEL)), _resident((D_MODEL, R_PROJ_COLS)),
                  _resident((R_QK_COLS, D_MODEL)),
                  _rows(tm, LANES), _rows(tm, LANES), _cols(LANES, tm), _cols(LANES, tm),
                  _resident((R_HEADS, tm, LANES)), _resident((SUBLANES, tm))],
        out_specs=[_rows(tm, R_QK_COLS), _cols(R_QK_COLS, tm), _rows(tm, R_V_COLS), _rows(tm, R_V_COLS)],
        out_shape=[jax.ShapeDtypeStruct((t, R_QK_COLS), BF16),
                   jax.ShapeDtypeStruct((R_QK_COLS, t), BF16),
                   jax.ShapeDtypeStruct((t, R_V_COLS), BF16),
                   jax.ShapeDtypeStruct((t, R_V_COLS), F32)],
        compiler_params=_params(1),
        name="ret_proj",
    )(x, nw, w, wkt, cos_r, sin_r, cos_rt, sin_rt, qdec, kdec)


def _ret_core_kernel(cdec_ref, q_ref, kt_ref, v_ref, g_ref, nw_ref, y_ref, s_sc):
    L = R_CHUNK

    @pl.when(pl.program_id(1) == 0)
    def _():
        s_sc[...] = jnp.zeros_like(s_sc)

    row = lax.broadcasted_iota(jnp.int32, (L, L), 0)
    col = lax.broadcasted_iota(jnp.int32, (L, L), 1)
    causal = row >= col
    nw = nw_ref[...]
    for h in range(R_HEADS):
        qh = q_ref[:, h * R_QK_DIM:(h + 1) * R_QK_DIM]
        kth = kt_ref[h * R_QK_DIM:(h + 1) * R_QK_DIM, :]
        vh = v_ref[:, h * R_V_DIM:(h + 1) * R_V_DIM]
        state = s_sc[h]
        scores = jnp.where(causal, _dot(qh, kth), 0.0)
        y = _dot(scores.astype(BF16), vh) + _dot(qh, state.astype(BF16))
        s_sc[h] = cdec_ref[h] * (state + _dot(kth, vh))
        mu = jnp.mean(y, axis=-1, keepdims=True)
        yc = y - mu
        yn = yc * lax.rsqrt(jnp.mean(yc * yc, axis=-1, keepdims=True) + EPS)
        gate = g_ref[:, h * R_V_DIM:(h + 1) * R_V_DIM]
        out = (gate * jax.nn.sigmoid(gate)) * (yn * nw[:, h * R_V_DIM:(h + 1) * R_V_DIM])
        y_ref[:, h * R_V_DIM:(h + 1) * R_V_DIM] = out.astype(BF16)


def _ret_core(q, kt, v, g, nw, cdec, *, batch, seq):
    L = R_CHUNK
    nb = seq // L
    cur = lambda b, i: (b * nb + i, 0)
    cur_t = lambda b, i: (0, b * nb + i)
    return pl.pallas_call(
        _ret_core_kernel,
        grid=(batch, nb),
        in_specs=[pl.BlockSpec(memory_space=pltpu.SMEM),
                  pl.BlockSpec((L, R_QK_COLS), cur), pl.BlockSpec((R_QK_COLS, L), cur_t),
                  pl.BlockSpec((L, R_V_COLS), cur), pl.BlockSpec((L, R_V_COLS), cur),
                  _resident((1, R_V_COLS))],
        out_specs=pl.BlockSpec((L, R_V_COLS), cur),
        out_shape=jax.ShapeDtypeStruct((batch * seq, R_V_COLS), BF16),
        scratch_shapes=[pltpu.VMEM((R_HEADS, R_QK_DIM, R_V_DIM), F32)],
        compiler_params=_params(2),
        name="ret_core",
    )(cdec, q, kt, v, g, nw)


def _rope_constants():
    inv = 1.0 / (ROPE_THETA ** (jnp.arange(0, A_ROPE_DIM, 2, dtype=F32) / A_ROPE_DIM))
    lane = np.arange(LANES) % A_HEAD_DIM
    idx = jnp.asarray(lane % (A_ROPE_DIM // 2))
    rot = jnp.asarray(lane < A_ROPE_DIM)
    inv_a = jnp.where(rot, inv[idx], 0.0).reshape(1, LANES)
    sgn = np.where(lane < A_ROPE_DIM // 2, -1.0, np.where(lane < A_ROPE_DIM, 1.0, 0.0))
    sgn_a = jnp.asarray(sgn, F32).reshape(1, LANES)
    inv_r = (1.0 / (R_THETA ** jnp.linspace(0.0, 1.0, R_QK_DIM // 2, dtype=F32))).reshape(1, LANES)
    return inv_a, sgn_a, inv_r


def _retention_constants(tm):
    L = R_CHUNK
    log_gamma = jnp.log(1.0 - 2.0 ** (-5.0 - jnp.arange(R_HEADS, dtype=F32)))
    j = jnp.asarray(np.arange(tm) % L, F32)
    q_decay = jnp.exp((j + 1.0) * log_gamma[:, None])
    k_decay = jnp.exp(-(j + 1.0) * log_gamma[:, None]) * (R_QK_DIM ** -0.5)
    chunk_decay = jnp.exp(L * log_gamma)
    qdec = jnp.broadcast_to(q_decay[:, :, None], (R_HEADS, tm, LANES))
    kdec = jnp.pad(k_decay, ((0, SUBLANES - R_HEADS), (0, 0)))
    return qdec, kdec, chunk_decay


def _attn_weight_layout(w_qkv, b_qkv, tm):
    nq = A_Q_COLS
    nkv = A_K_ROWS

    def dup(m):
        lead = m.shape[:-1]
        m = m.reshape(lead + (A_KV_HEADS, 1, A_HEAD_DIM))
        return jnp.broadcast_to(m, lead + (A_KV_HEADS, 2, A_HEAD_DIM)).reshape(lead + (A_V_COLS,))

    w = jnp.concatenate([w_qkv[:, :nq], dup(w_qkv[:, nq + nkv:])], axis=-1).astype(BF16)
    b = jnp.concatenate([b_qkv[:nq], dup(b_qkv[nq + nkv:])], axis=-1).reshape(1, nq + A_V_COLS)
    wkt = w_qkv[:, nq:nq + nkv].T.astype(BF16)
    bkt = jnp.broadcast_to(b_qkv[nq:nq + nkv][:, None], (nkv, tm))
    return w, b, wkt, bkt


def _ret_weight_layout(w_in):
    head = np.concatenate([np.arange(0, R_QK_DIM, 2), np.arange(1, R_QK_DIM, 2)])
    perm = np.concatenate([h * R_QK_DIM + head for h in range(R_HEADS)])
    w = jnp.concatenate([w_in[:, perm], w_in[:, 2 * R_QK_COLS:]], axis=-1).astype(BF16)
    wkt = w_in[:, R_QK_COLS + perm].T.astype(BF16)
    return w, wkt


def _mlstm_weight_layout(w_in, b_if):
    pad = LANES - 2 * M_HEADS
    w = jnp.concatenate([w_in[:, :M_QK_COLS], w_in[:, 2 * M_QK_COLS:],
                         jnp.zeros((D_MODEL, pad), w_in.dtype)], axis=-1).astype(BF16)
    wkt = w_in[:, M_QK_COLS:2 * M_QK_COLS].T.astype(BF16)
    return w, jnp.pad(b_if, (0, pad)).reshape(1, LANES), wkt


def kernel(x, positions, norm_w, final_norm_w, ffn_w_in, ffn_w_out, attn_w_qkv, attn_b_qkv, attn_w_o, attn_b_o,
           attn_sinks, mlstm_w_in, mlstm_b_if, mlstm_norm_w, mlstm_w_out, ret_w_in, ret_norm_w, ret_w_out):
    batch, seq, d = x.shape
    t = batch * seq
    tm = min(ROW_TILE, t)
    assert d == D_MODEL and t % tm == 0 and tm % R_CHUNK == 0
    assert seq % R_CHUNK == 0 and seq % M_CHUNK == 0 and seq % A_CHUNK == 0

    xf = x.reshape(t, d)
    pos_b = jnp.broadcast_to(positions.astype(F32).reshape(t, 1), (t, LANES))
    inv_a, sgn_a, inv_r = _rope_constants()
    cos_a, sin_a, cos_r, sin_r, cos_at, sin_at, cos_rt, sin_rt = _rope_tables(pos_b, inv_a, sgn_a, inv_r, tm)
    qdec, kdec, cdec = _retention_constants(tm)
    zero_bias = jnp.zeros((1, D_MODEL), F32)

    for i in range(DEPTH):
        kind, slot = i % N_MIXERS, i // N_MIXERS
        xf = _ffn(xf, norm_w[i, 0].reshape(1, d), ffn_w_in[i, 0].astype(BF16), ffn_w_out[i, 0].astype(BF16), tm=tm)
        nw = norm_w[i, 1].reshape(1, d)
        if kind == 0:
            w, b, wkt, bkt = _attn_weight_layout(attn_w_qkv[slot], attn_b_qkv[slot], tm)
            q, kt, vd = _attn_proj(xf, nw, w, b, wkt, bkt, cos_a, sin_a, cos_at, sin_at, tm=tm)
            y = _attn_core(q, kt, vd, attn_sinks[slot], batch=batch, seq=seq)
            mix = (y, attn_w_o[slot].astype(BF16), attn_b_o[slot].reshape(1, d))
        elif kind == 1:
            w, bg, wkt = _mlstm_weight_layout(mlstm_w_in[slot], mlstm_b_if[slot])
            q, kt, v, o, g = _mlstm_proj(xf, nw, w, bg, wkt, tm=tm)
            y = _mlstm_core(q, kt, v, o, g, mlstm_norm_w[slot].reshape(1, M_V_COLS), batch=batch, seq=seq)
            mix = (y, mlstm_w_out[slot].astype(BF16), zero_bias)
        else:
            w, wkt = _ret_weight_layout(ret_w_in[slot])
            q, kt, v, g = _ret_proj(xf, nw, w, wkt, cos_r, sin_r, cos_rt, sin_rt, qdec, kdec, tm=tm)
            y = _ret_core(q, kt, v, g, ret_norm_w[slot].reshape(1, R_V_COLS), cdec, batch=batch, seq=seq)
            mix = (y, ret_w_out[slot].astype(BF16), zero_bias)
        last = i == DEPTH - 1
        xf = _ffn(xf, norm_w[i, 2].reshape(1, d), ffn_w_in[i, 1].astype(BF16), ffn_w_out[i, 1].astype(BF16),
                  tm=tm, mix=mix, final_w=final_norm_w.reshape(1, d) if last else None)
    return xf.reshape(batch, seq, d)
```

```python
import functools

import jax
import jax.numpy as jnp
import numpy as np
from jax import lax
from jax.experimental import pallas as pl
from jax.experimental.pallas import tpu as pltpu

F32 = jnp.float32
BF16 = jnp.bfloat16

D_MODEL = 1024
DEPTH = 4
N_MIXERS = 3
EPS = 1e-6
D_FF = 2816
A_HEADS = 16
A_KV_HEADS = 4
A_GROUP = A_HEADS // A_KV_HEADS
A_HEAD_DIM = 64
A_WINDOW = 128
A_ROPE_DIM = 16
ROPE_THETA = 500000.0
M_HEADS = 8
M_QK_DIM = 64
M_V_DIM = 128
M_GATE_CAP = 15.0
R_HEADS = 4
R_QK_DIM = 256
R_V_DIM = 512
R_THETA = 10000.0

LANES = 128
SUBLANES = 8
BF16_ROWS = 16
VMEM_LIMIT = 56 * 1024 * 1024

ROW_TILE = 512
FF_CHUNK = 256
A_CHUNK = A_WINDOW
M_CHUNK = 128
R_CHUNK = 256


def _params(n_axes):
    return pltpu.CompilerParams(
        dimension_semantics=("arbitrary",) * n_axes,
        vmem_limit_bytes=VMEM_LIMIT,
    )


def _resident(shape):
    nd = len(shape)
    return pl.BlockSpec(shape, lambda *_: (0,) * nd, pipeline_mode=pl.Buffered(1))


def _resident_slice(lead, shape):
    nd = len(shape)
    return pl.BlockSpec((None,) * len(lead) + tuple(shape), lambda *_: tuple(lead) + (0,) * nd,
                        pipeline_mode=pl.Buffered(1))


def _rows(tm, width):
    return pl.BlockSpec((tm, width), lambda i: (i, 0))


def _cols(height, tm):
    return pl.BlockSpec((height, tm), lambda i: (0, i))


def _rms(x, w):
    ms = jnp.mean(x * x, axis=-1, keepdims=True)
    return x * lax.rsqrt(ms + EPS) * w


def _dot(a, b):
    return jnp.dot(a, b, preferred_element_type=F32)


def _dot_nt(a, b):
    return lax.dot_general(a, b, (((1,), (1,)), ((), ())), preferred_element_type=F32)


def _dot_tn(a, b):
    return lax.dot_general(a, b, (((0,), (0,)), ((), ())), preferred_element_type=F32)


def _tables_kernel(pos_ref, inv_a_ref, sgn_a_ref, inv_r_ref,
                   cos_a, sin_a, cos_r, sin_r, cos_at, sin_at, cos_rt, sin_rt):
    half = A_ROPE_DIM // 2
    pos = pos_ref[...]
    ang_a = pos * inv_a_ref[...]
    ca = jnp.cos(ang_a)
    sa = jnp.sin(ang_a) * sgn_a_ref[...]
    cos_a[...] = ca
    sin_a[...] = sa
    cos_at[...] = ca.T[:half]
    sin_at[...] = sa.T[half:2 * half]
    ang_r = pos * inv_r_ref[...]
    cr = jnp.cos(ang_r)
    sr = jnp.sin(ang_r)
    cos_r[...] = cr
    sin_r[...] = sr
    cos_rt[...] = cr.T
    sin_rt[...] = sr.T


def _rope_tables(pos_b, inv_a, sgn_a, inv_r, tm):
    t = pos_b.shape[0]
    half = A_ROPE_DIM // 2
    tok = jax.ShapeDtypeStruct((t, LANES), F32)
    return pl.pallas_call(
        _tables_kernel,
        grid=(t // tm,),
        in_specs=[_rows(tm, LANES), _resident((1, LANES)), _resident((1, LANES)), _resident((1, LANES))],
        out_specs=[_rows(tm, LANES)] * 4 + [_cols(half, tm), _cols(half, tm), _cols(LANES, tm), _cols(LANES, tm)],
        out_shape=[tok] * 4 + [jax.ShapeDtypeStruct((half, t), F32)] * 2 + [jax.ShapeDtypeStruct((LANES, t), F32)] * 2,
        compiler_params=_params(1),
        name="rope_tables",
    )(pos_b, inv_a, sgn_a, inv_r)


def _ffn_kernel(*refs, mix_in, mix_feature_major, final_norm):
    refs = list(refs)
    x_ref = refs.pop(0)
    if mix_in:
        y_ref, wo_ref, bo_ref = refs.pop(0), refs.pop(0), refs.pop(0)
    nw_ref, win_ref, wout_ref = refs.pop(0), refs.pop(0), refs.pop(0)
    fw_ref = refs.pop(0) if final_norm else None
    (o_ref,) = refs

    x = x_ref[...]
    if mix_in:
        mixed = _dot_tn(y_ref[...], wo_ref[...]) if mix_feature_major else _dot(y_ref[...], wo_ref[...])
        x = x + (mixed + bo_ref[...])
    h = _rms(x, nw_ref[...]).astype(BF16)
    acc = jnp.zeros_like(x)
    for c in range(D_FF // FF_CHUNK):
        lo = c * FF_CHUNK
        g = _dot(h, win_ref[:, lo:lo + FF_CHUNK])
        u = _dot(h, win_ref[:, D_FF + lo:D_FF + lo + FF_CHUNK])
        a = (g * jax.nn.sigmoid(g) * u).astype(BF16)
        acc = acc + _dot(a, wout_ref[lo:lo + FF_CHUNK, :])
    y = x + 0.5 * acc
    if final_norm:
        y = _rms(y, fw_ref[...])
    o_ref[...] = y


def _ffn(x, norm_w, w_in, w_out, layer, which, *, tm, mix=None, final_w=None):
    t = x.shape[0]
    in_specs = [_rows(tm, D_MODEL)]
    args = [x]
    feature_major = False
    if mix is not None:
        y, w_o, b_o, feature_major = mix
        kdim = w_o.shape[0]
        assert y.shape == ((kdim, t) if feature_major else (t, kdim))
        in_specs += [_cols(kdim, tm) if feature_major else _rows(tm, kdim),
                     _resident((kdim, D_MODEL)), _resident((1, D_MODEL))]
        args += [y, w_o, b_o]
    in_specs += [_resident_slice((layer, 2 * which), (1, D_MODEL)),
                 _resident_slice((layer, which), (D_MODEL, 2 * D_FF)),
                 _resident_slice((layer, which), (D_FF, D_MODEL))]
    args += [norm_w, w_in, w_out]
    if final_w is not None:
        in_specs.append(_resident((1, D_MODEL)))
        args.append(final_w)
    return pl.pallas_call(
        functools.partial(_ffn_kernel, mix_in=mix is not None, mix_feature_major=feature_major,
                          final_norm=final_w is not None),
        grid=(t // tm,),
        in_specs=in_specs,
        out_specs=_rows(tm, D_MODEL),
        out_shape=jax.ShapeDtypeStruct((t, D_MODEL), F32),
        compiler_params=_params(1),
        name="ffn_mix" if mix is not None else "ffn",
    )(*args)


A_Q_ROWS = A_HEADS * A_HEAD_DIM
A_KV_DIM = A_KV_HEADS * A_HEAD_DIM
A_PROJ_ROWS = 256


def _attn_proj_kernel(x_ref, nw_ref, wt_ref, bt_ref, wk_ref, bk_ref, cos_ref, sin_ref, cost_ref, sint_ref,
                      qt_ref, k_ref, vt_ref):
    tm = x_ref.shape[0]
    half = A_ROPE_DIM // 2
    h = _rms(x_ref[...], nw_ref[...]).astype(BF16)

    cos = cos_ref[...]
    sin = sin_ref[...]
    lane = lax.broadcasted_iota(jnp.int32, (tm, LANES), 1) & (A_HEAD_DIM - 1)
    upper = (lane >= half) & (lane < A_ROPE_DIM)
    k = _dot(h, wk_ref[...]) + bk_ref[...]
    for j in range(A_KV_DIM // LANES):
        piece = k[:, j * LANES:(j + 1) * LANES]
        partner = jnp.where(upper, pltpu.roll(piece, half, 1), pltpu.roll(piece, LANES - half, 1))
        k_ref[:, j * LANES:(j + 1) * LANES] = (piece * cos + partner * sin).astype(BF16)

    cos_t = cost_ref[...]
    sin_t = sint_ref[...]
    q_scale = A_HEAD_DIM ** -0.5
    for c in range((A_Q_ROWS + A_KV_DIM) // A_PROJ_ROWS):
        lo = c * A_PROJ_ROWS
        part = _dot_nt(wt_ref[lo:lo + A_PROJ_ROWS, :], h) + bt_ref[lo:lo + A_PROJ_ROWS, :]
        if lo < A_Q_ROWS:
            for j in range(A_PROJ_ROWS // A_HEAD_DIM):
                base = j * A_HEAD_DIM
                x1 = part[base:base + half]
                x2 = part[base + half:base + 2 * half]
                rot = jnp.concatenate([x1 * cos_t - x2 * sin_t, x2 * cos_t + x1 * sin_t,
                                       part[base + 2 * half:base + A_HEAD_DIM]], axis=0)
                qt_ref[lo + base:lo + base + A_HEAD_DIM, :] = (rot * q_scale).astype(BF16)
        else:
            vt_ref[...] = part.astype(BF16)


def _attn_proj(x, nw, wt, bt, wk, bk, cos_a, sin_a, cos_at, sin_at, *, tm):
    t = x.shape[0]
    half = A_ROPE_DIM // 2
    rows = A_Q_ROWS + A_KV_DIM
    return pl.pallas_call(
        _attn_proj_kernel,
        grid=(t // tm,),
        in_specs=[_rows(tm, D_MODEL), _resident((1, D_MODEL)), _resident((rows, D_MODEL)), _resident((rows, tm)),
                  _resident((D_MODEL, A_KV_DIM)), _resident((1, A_KV_DIM)),
                  _rows(tm, LANES), _rows(tm, LANES), _cols(half, tm), _cols(half, tm)],
        out_specs=[_cols(A_Q_ROWS, tm), _rows(tm, A_KV_DIM), _cols(A_KV_DIM, tm)],
        out_shape=[jax.ShapeDtypeStruct((A_Q_ROWS, t), BF16),
                   jax.ShapeDtypeStruct((t, A_KV_DIM), BF16),
                   jax.ShapeDtypeStruct((A_KV_DIM, t), BF16)],
        compiler_params=_params(1),
        name="attn_proj",
    )(x, nw, wt, bt, wk, bk, cos_a, sin_a, cos_at, sin_at)


def _attn_core_kernel(sink_ref, qt_ref, kp_ref, kc_ref, vtp_ref, vtc_ref, ot_ref):
    L = A_CHUNK
    blk = pl.program_id(1)
    key = lax.broadcasted_iota(jnp.int32, (L, L), 0)
    qry = lax.broadcasted_iota(jnp.int32, (L, L), 1)
    from_prev = key > qry
    prev_bias = jnp.where(blk > 0, 0.0, -jnp.inf).astype(F32)
    zero_q = jnp.zeros((A_HEAD_DIM, A_GROUP * L), BF16)
    for g in range(A_KV_HEADS):
        col = g // 2
        kk = jnp.concatenate([kp_ref[:, col * LANES:(col + 1) * LANES],
                              kc_ref[:, col * LANES:(col + 1) * LANES]], axis=0)
        q_cat = jnp.concatenate([qt_ref[(A_GROUP * g + j) * A_HEAD_DIM:(A_GROUP * g + j + 1) * A_HEAD_DIM, :]
                                 for j in range(A_GROUP)], axis=1)
        q_ext = jnp.concatenate([q_cat, zero_q] if g % 2 == 0 else [zero_q, q_cat], axis=0)
        s = _dot(kk, q_ext)
        probs = []
        for j in range(A_GROUP):
            sink = sink_ref[A_GROUP * g + j]
            s_prev = s[:L, j * L:(j + 1) * L]
            s_own = s[L:, j * L:(j + 1) * L]
            sc = jnp.where(from_prev, s_prev + prev_bias, s_own)
            mx = jnp.maximum(jnp.max(sc, axis=0, keepdims=True), sink)
            p = jnp.exp(sc - mx)
            denom = jnp.sum(p, axis=0, keepdims=True) + jnp.exp(sink - mx)
            pn = p * (1.0 / denom)
            zero = jnp.zeros_like(pn)
            probs.append(jnp.concatenate([jnp.where(from_prev, pn, zero), jnp.where(from_prev, zero, pn)],
                                         axis=0).astype(BF16))
        vt = jnp.concatenate([vtp_ref[g * A_HEAD_DIM:(g + 1) * A_HEAD_DIM, :],
                              vtc_ref[g * A_HEAD_DIM:(g + 1) * A_HEAD_DIM, :]], axis=1)
        out = _dot(vt, jnp.concatenate(probs, axis=1))
        for j in range(A_GROUP):
            head = A_GROUP * g + j
            ot_ref[head * A_HEAD_DIM:(head + 1) * A_HEAD_DIM, :] = out[:, j * L:(j + 1) * L].astype(BF16)


def _attn_core(qt, k, vt, sinks, *, batch, seq):
    L = A_CHUNK
    nb = seq // L
    cur = lambda b, i: (b * nb + i, 0)
    prev = lambda b, i: (b * nb + jnp.maximum(i - 1, 0), 0)
    cur_t = lambda b, i: (0, b * nb + i)
    prev_t = lambda b, i: (0, b * nb + jnp.maximum(i - 1, 0))
    return pl.pallas_call(
        _attn_core_kernel,
        grid=(batch, nb),
        in_specs=[pl.BlockSpec(memory_space=pltpu.SMEM),
                  pl.BlockSpec((A_Q_ROWS, L), cur_t),
                  pl.BlockSpec((L, A_KV_DIM), prev), pl.BlockSpec((L, A_KV_DIM), cur),
                  pl.BlockSpec((A_KV_DIM, L), prev_t), pl.BlockSpec((A_KV_DIM, L), cur_t)],
        out_specs=pl.BlockSpec((A_Q_ROWS, L), cur_t),
        out_shape=jax.ShapeDtypeStruct((A_Q_ROWS, batch * seq), BF16),
        compiler_params=_params(2),
        name="attn_core",
    )(sinks, qt, k, k, vt, vt)


M_QK_COLS = M_HEADS * M_QK_DIM
M_V_COLS = M_HEADS * M_V_DIM
M_T_ROWS = M_QK_COLS + 2 * M_V_COLS
M_PROJ_ROWS = 512
M_STATE_ROWS = M_V_DIM + BF16_ROWS


def _mlstm_proj_kernel(x_ref, nw_ref, wt_ref, wk_ref, wg_ref, bg_ref,
                       qt_ref, k_ref, vt_ref, ot_ref, src_ref, gt_ref):
    tm = x_ref.shape[0]
    H = M_HEADS
    h = _rms(x_ref[...], nw_ref[...]).astype(BF16)
    k_ref[...] = _dot(h, wk_ref[...]).astype(BF16)

    gates = _dot(h, wg_ref[...]) + bg_ref[...]
    gates = M_GATE_CAP * jnp.tanh(gates / M_GATE_CAP)
    lane = lax.broadcasted_iota(jnp.int32, (tm, LANES), 1)
    b = jnp.where((lane >= H) & (lane < 2 * H), jax.nn.log_sigmoid(gates), 0.0)
    time = lax.broadcasted_iota(jnp.int32, (tm, LANES), 0) & (M_CHUNK - 1)
    shift = 1
    while shift < M_CHUNK:
        b = b + jnp.where(time >= shift, pltpu.roll(b, shift, 0), 0.0)
        shift *= 2
    src_ref[...] = gates - pltpu.roll(b, LANES - H, 1)
    gt_ref[...] = jnp.where(lane < H, gates, b).T[:2 * H]

    for c in range(M_T_ROWS // M_PROJ_ROWS):
        lo = c * M_PROJ_ROWS
        part = _dot_nt(wt_ref[lo:lo + M_PROJ_ROWS, :], h)
        if lo < M_QK_COLS:
            qt_ref[...] = (part * (M_QK_DIM ** -0.5)).astype(BF16)
        elif lo < M_QK_COLS + M_V_COLS:
            vt_ref[lo - M_QK_COLS:lo - M_QK_COLS + M_PROJ_ROWS, :] = part.astype(BF16)
        else:
            o_lo = lo - M_QK_COLS - M_V_COLS
            ot_ref[o_lo:o_lo + M_PROJ_ROWS, :] = jax.nn.sigmoid(part)


def _mlstm_proj(x, nw, wt, wk, wg, bg, *, tm):
    t = x.shape[0]
    assert tm % M_CHUNK == 0 and M_CHUNK & (M_CHUNK - 1) == 0
    return pl.pallas_call(
        _mlstm_proj_kernel,
        grid=(t // tm,),
        in_specs=[_rows(tm, D_MODEL), _resident((1, D_MODEL)), _resident((M_T_ROWS, D_MODEL)),
                  _resident((D_MODEL, M_QK_COLS)), _resident((D_MODEL, LANES)), _resident((1, LANES))],
        out_specs=[_cols(M_QK_COLS, tm), _rows(tm, M_QK_COLS), _cols(M_V_COLS, tm),
                   _cols(M_V_COLS, tm), _rows(tm, LANES), _cols(2 * M_HEADS, tm)],
        out_shape=[jax.ShapeDtypeStruct((M_QK_COLS, t), BF16),
                   jax.ShapeDtypeStruct((t, M_QK_COLS), BF16),
                   jax.ShapeDtypeStruct((M_V_COLS, t), BF16),
                   jax.ShapeDtypeStruct((M_V_COLS, t), F32),
                   jax.ShapeDtypeStruct((t, LANES), F32),
                   jax.ShapeDtypeStruct((2 * M_HEADS, t), F32)],
        compiler_params=_params(1),
        name="mlstm_proj",
    )(x, nw, wt, wk, wg, bg)


def _mlstm_core_kernel(qt_ref, k_ref, vt_ref, ot_ref, src_ref, gt_ref, nwt_ref, yt_ref, c_sc, m_sc):
    L = M_CHUNK
    H = M_HEADS

    @pl.when(pl.program_id(1) == 0)
    def _():
        c_sc[...] = jnp.zeros_like(c_sc)
        m_sc[...] = jnp.zeros_like(m_sc)

    src_i = lax.broadcasted_iota(jnp.int32, (L, L), 0)
    tgt_i = lax.broadcasted_iota(jnp.int32, (L, L), 1)
    causal = src_i <= tgt_i
    ones_rows = (lax.broadcasted_iota(jnp.int32, (BF16_ROWS, L), 0) == 0).astype(BF16)
    zero_q = jnp.zeros((M_QK_DIM, L), BF16)
    low = lax.broadcasted_iota(jnp.int32, (L, LANES), 1) < M_QK_DIM

    for h in range(H):
        src_col = src_ref[:, h:h + 1]
        b_tgt = gt_ref[H + h:H + h + 1, :]
        m_prev = m_sc[h][:, :1]
        dmat = jnp.where(causal, src_col + b_tgt, -jnp.inf)
        inter = b_tgt + m_prev
        m_row = jnp.maximum(inter, jnp.max(dmat, axis=0, keepdims=True))
        w_intra = jnp.exp(dmat - m_row)
        w_inter = jnp.exp(inter - m_row)

        pair = h // 2
        k_pair = k_ref[:, pair * LANES:(pair + 1) * LANES]
        k_own = jnp.where(low if h % 2 == 0 else jnp.logical_not(low), k_pair, jnp.zeros_like(k_pair))
        qt_h = qt_ref[h * M_QK_DIM:(h + 1) * M_QK_DIM, :]
        qt_own = jnp.concatenate([qt_h, zero_q] if h % 2 == 0 else [zero_q, qt_h], axis=0)
        s = _dot(k_own, qt_own) * w_intra
        v_ext = jnp.concatenate([vt_ref[h * M_V_DIM:(h + 1) * M_V_DIM, :], ones_rows], axis=0)
        state = c_sc[h]
        tot = _dot(v_ext, s.astype(BF16)) + w_inter * _dot(state.astype(BF16), qt_own)
        num = tot[:M_V_DIM]
        nq = tot[M_V_DIM:M_V_DIM + 1]
        h_out = num * (1.0 / jnp.maximum(jnp.abs(nq), jnp.exp(-m_row)))
        h_out = h_out * lax.rsqrt(jnp.mean(h_out * h_out, axis=0, keepdims=True) + EPS)
        rows = slice(h * M_V_DIM, (h + 1) * M_V_DIM)
        yt_ref[rows, :] = (ot_ref[rows, :] * (h_out * nwt_ref[rows, :])).astype(BF16)

        b_last = b_tgt[:, L - 1:L]
        dec = b_last + (gt_ref[h:h + 1, :] - b_tgt)
        m_new = jnp.maximum(b_last + m_prev, jnp.max(dec, axis=-1, keepdims=True))
        wk = jnp.exp(dec - m_new)
        keep = jnp.exp(b_last + m_prev - m_new)
        v_scaled = (v_ext.astype(F32) * wk).astype(BF16)
        c_sc[h] = keep * state + _dot(v_scaled, k_own)
        m_sc[h] = jnp.broadcast_to(m_new, (1, LANES))


def _mlstm_core(qt, k, vt, ot, src, gt, nwt, *, batch, seq):
    L = M_CHUNK
    nb = seq // L
    cur = lambda b, i: (b * nb + i, 0)
    cur_t = lambda b, i: (0, b * nb + i)
    return pl.pallas_call(
        _mlstm_core_kernel,
        grid=(batch, nb),
        in_specs=[pl.BlockSpec((M_QK_COLS, L), cur_t), pl.BlockSpec((L, M_QK_COLS), cur),
                  pl.BlockSpec((M_V_COLS, L), cur_t), pl.BlockSpec((M_V_COLS, L), cur_t),
                  pl.BlockSpec((L, LANES), cur), pl.BlockSpec((2 * M_HEADS, L), cur_t), _resident((M_V_COLS, L))],
        out_specs=pl.BlockSpec((M_V_COLS, L), cur_t),
        out_shape=jax.ShapeDtypeStruct((M_V_COLS, batch * seq), BF16),
        scratch_shapes=[pltpu.VMEM((M_HEADS, M_STATE_ROWS, LANES), F32),
                        pltpu.VMEM((M_HEADS, 1, LANES), F32)],
        compiler_params=_params(2),
        name="mlstm_core",
    )(qt, k, vt, ot, src, gt, nwt)


R_QK_COLS = R_HEADS * R_QK_DIM
R_V_COLS = R_HEADS * R_V_DIM
R_PROJ_COLS = R_QK_COLS + 2 * R_V_COLS
R_PROJ_CHUNK = 512


def _ret_proj_kernel(x_ref, nw_ref, w_ref, wkt_ref, cos_ref, sin_ref, cost_ref, sint_ref, qdec_ref, kdec_ref,
                     q_ref, kt_ref, v_ref, g_ref):
    h = _rms(x_ref[...], nw_ref[...]).astype(BF16)
    cos = cos_ref[...]
    sin = sin_ref[...]
    half = R_QK_DIM // 2
    for c in range(R_QK_COLS // R_PROJ_CHUNK):
        lo = c * R_PROJ_CHUNK
        t = _dot(h, w_ref[:, lo:lo + R_PROJ_CHUNK])
        for hh in range(R_PROJ_CHUNK // R_QK_DIM):
            head = lo // R_QK_DIM + hh
            dec = qdec_ref[head]
            x1 = t[:, hh * R_QK_DIM:hh * R_QK_DIM + half]
            x2 = t[:, hh * R_QK_DIM + half:(hh + 1) * R_QK_DIM]
            base = head * R_QK_DIM
            q_ref[:, base:base + half] = ((x1 * cos - x2 * sin) * dec).astype(BF16)
            q_ref[:, base + half:base + R_QK_DIM] = ((x2 * cos + x1 * sin) * dec).astype(BF16)
    for c in range(2 * R_V_COLS // R_PROJ_CHUNK):
        lo = c * R_PROJ_CHUNK
        t = _dot(h, w_ref[:, R_QK_COLS + lo:R_QK_COLS + lo + R_PROJ_CHUNK])
        if lo < R_V_COLS:
            v_ref[:, lo:lo + R_PROJ_CHUNK] = t.astype(BF16)
        else:
            g_ref[:, lo - R_V_COLS:lo - R_V_COLS + R_PROJ_CHUNK] = t * jax.nn.sigmoid(t)
    cos_t = cost_ref[...]
    sin_t = sint_ref[...]
    for head in range(R_HEADS):
        base = head * R_QK_DIM
        kt = _dot_nt(wkt_ref[base:base + R_QK_DIM, :], h)
        dec = kdec_ref[head:head + 1, :]
        x1 = kt[:half]
        x2 = kt[half:]
        kt_ref[base:base + half, :] = ((x1 * cos_t - x2 * sin_t) * dec).astype(BF16)
        kt_ref[base + half:base + R_QK_DIM, :] = ((x2 * cos_t + x1 * sin_t) * dec).astype(BF16)


def _ret_proj(x, nw, w, wkt, cos_r, sin_r, cos_rt, sin_rt, qdec, kdec, *, tm):
    t = x.shape[0]
    return pl.pallas_call(
        _ret_proj_kernel,
        grid=(t // tm,),
        in_specs=[_rows(tm, D_MODEL), _resident((1, D_MODEL)), _resident((D_MODEL, R_PROJ_COLS)),
                  _resident((R_QK_COLS, D_MODEL)),
                  _rows(tm, LANES), _rows(tm, LANES), _cols(LANES, tm), _cols(LANES, tm),
                  _resident((R_HEADS, tm, LANES)), _resident((SUBLANES, tm))],
        out_specs=[_rows(tm, R_QK_COLS), _cols(R_QK_COLS, tm), _rows(tm, R_V_COLS), _rows(tm, R_V_COLS)],
        out_shape=[jax.ShapeDtypeStruct((t, R_QK_COLS), BF16),
                   jax.ShapeDtypeStruct((R_QK_COLS, t), BF16),
                   jax.ShapeDtypeStruct((t, R_V_COLS), BF16),
                   jax.ShapeDtypeStruct((t, R_V_COLS), F32)],
        compiler_params=_params(1),
        name="ret_proj",
    )(x, nw, w, wkt, cos_r, sin_r, cos_rt, sin_rt, qdec, kdec)


def _ret_core_kernel(cdec_ref, q_ref, kt_ref, v_ref, g_ref, nw_ref, y_ref, s_sc):
    L = R_CHUNK

    @pl.when(pl.program_id(1) == 0)
    def _():
        s_sc[...] = jnp.zeros_like(s_sc)

    row = lax.broadcasted_iota(jnp.int32, (L, L), 0)
    col = lax.broadcasted_iota(jnp.int32, (L, L), 1)
    causal = row >= col
    nw = nw_ref[...]
    for h in range(R_HEADS):
        qh = q_ref[:, h * R_QK_DIM:(h + 1) * R_QK_DIM]
        kth = kt_ref[h * R_QK_DIM:(h + 1) * R_QK_DIM, :]
        vh = v_ref[:, h * R_V_DIM:(h + 1) * R_V_DIM]
        state = s_sc[h]
        scores = jnp.where(causal, _dot(qh, kth), 0.0)
        y = _dot(scores.astype(BF16), vh) + _dot(qh, state.astype(BF16))
        s_sc[h] = cdec_ref[h] * (state + _dot(kth, vh))
        mu = jnp.mean(y, axis=-1, keepdims=True)
        yc = y - mu
        yn = yc * lax.rsqrt(jnp.mean(yc * yc, axis=-1, keepdims=True) + EPS)
        gate = g_ref[:, h * R_V_DIM:(h + 1) * R_V_DIM]
        out = gate * (yn * nw[:, h * R_V_DIM:(h + 1) * R_V_DIM])
        y_ref[:, h * R_V_DIM:(h + 1) * R_V_DIM] = out.astype(BF16)


def _ret_core(q, kt, v, g, nw, cdec, *, batch, seq):
    L = R_CHUNK
    nb = seq // L
    cur = lambda b, i: (b * nb + i, 0)
    cur_t = lambda b, i: (0, b * nb + i)
    return pl.pallas_call(
        _ret_core_kernel,
        grid=(batch, nb),
        in_specs=[pl.BlockSpec(memory_space=pltpu.SMEM),
                  pl.BlockSpec((L, R_QK_COLS), cur), pl.BlockSpec((R_QK_COLS, L), cur_t),
                  pl.BlockSpec((L, R_V_COLS), cur), pl.BlockSpec((L, R_V_COLS), cur),
                  _resident((1, R_V_COLS))],
        out_specs=pl.BlockSpec((L, R_V_COLS), cur),
        out_shape=jax.ShapeDtypeStruct((batch * seq, R_V_COLS), BF16),
        scratch_shapes=[pltpu.VMEM((R_HEADS, R_QK_DIM, R_V_DIM), F32)],
        compiler_params=_params(2),
        name="ret_core",
    )(cdec, q, kt, v, g, nw)


def _rope_constants():
    inv = 1.0 / (ROPE_THETA ** (jnp.arange(0, A_ROPE_DIM, 2, dtype=F32) / A_ROPE_DIM))
    lane = np.arange(LANES) % A_HEAD_DIM
    idx = jnp.asarray(lane % (A_ROPE_DIM // 2))
    rot = jnp.asarray(lane < A_ROPE_DIM)
    inv_a = jnp.where(rot, inv[idx], 0.0).reshape(1, LANES)
    sgn = np.where(lane < A_ROPE_DIM // 2, -1.0, np.where(lane < A_ROPE_DIM, 1.0, 0.0))
    sgn_a = jnp.asarray(sgn, F32).reshape(1, LANES)
    inv_r = (1.0 / (R_THETA ** jnp.linspace(0.0, 1.0, R_QK_DIM // 2, dtype=F32))).reshape(1, LANES)
    return inv_a, sgn_a, inv_r


def _retention_constants(tm):
    L = R_CHUNK
    log_gamma = jnp.log(1.0 - 2.0 ** (-5.0 - jnp.arange(R_HEADS, dtype=F32)))
    j = jnp.asarray(np.arange(tm) % L, F32)
    q_decay = jnp.exp((j + 1.0) * log_gamma[:, None])
    k_decay = jnp.exp(-(j + 1.0) * log_gamma[:, None]) * (R_QK_DIM ** -0.5)
    chunk_decay = jnp.exp(L * log_gamma)
    qdec = jnp.broadcast_to(q_decay[:, :, None], (R_HEADS, tm, LANES))
    kdec = jnp.pad(k_decay, ((0, SUBLANES - R_HEADS), (0, 0)))
    return qdec, kdec, chunk_decay


def _attn_weight_layout(w_qkv, b_qkv, tm):
    nq = A_Q_ROWS
    qv = np.concatenate([np.arange(nq), np.arange(nq + A_KV_DIM, nq + 2 * A_KV_DIM)])
    wt = w_qkv[:, qv].T.astype(BF16)
    bt = jnp.broadcast_to(b_qkv[qv][:, None], (nq + A_KV_DIM, tm))
    wk = w_qkv[:, nq:nq + A_KV_DIM].astype(BF16)
    bk = b_qkv[nq:nq + A_KV_DIM].reshape(1, A_KV_DIM)
    return wt, bt, wk, bk


def _ret_weight_layout(w_in):
    head = np.concatenate([np.arange(0, R_QK_DIM, 2), np.arange(1, R_QK_DIM, 2)])
    perm = np.concatenate([h * R_QK_DIM + head for h in range(R_HEADS)])
    w = jnp.concatenate([w_in[:, perm], w_in[:, 2 * R_QK_COLS:]], axis=-1).astype(BF16)
    wkt = w_in[:, R_QK_COLS + perm].T.astype(BF16)
    return w, wkt


def _mlstm_weight_layout(w_in, b_if):
    pad = LANES - 2 * M_HEADS
    gate_lo = 2 * M_QK_COLS + 2 * M_V_COLS
    wt = jnp.concatenate([w_in[:, :M_QK_COLS], w_in[:, 2 * M_QK_COLS:gate_lo]], axis=-1).T.astype(BF16)
    wk = w_in[:, M_QK_COLS:2 * M_QK_COLS].astype(BF16)
    wg = jnp.pad(w_in[:, gate_lo:], ((0, 0), (0, pad))).astype(BF16)
    return wt, wk, wg, jnp.pad(b_if, (0, pad)).reshape(1, LANES)


def kernel(x, positions, norm_w, final_norm_w, ffn_w_in, ffn_w_out, attn_w_qkv, attn_b_qkv, attn_w_o, attn_b_o,
           attn_sinks, mlstm_w_in, mlstm_b_if, mlstm_norm_w, mlstm_w_out, ret_w_in, ret_norm_w, ret_w_out):
    batch, seq, d = x.shape
    t = batch * seq
    tm = min(ROW_TILE, t)
    assert d == D_MODEL and t % tm == 0 and tm % R_CHUNK == 0
    assert seq % R_CHUNK == 0 and seq % M_CHUNK == 0 and seq % A_CHUNK == 0

    xf = x.reshape(t, d)
    pos_b = jnp.broadcast_to(positions.astype(F32).reshape(t, 1), (t, LANES))
    inv_a, sgn_a, inv_r = _rope_constants()
    cos_a, sin_a, cos_r, sin_r, cos_at, sin_at, cos_rt, sin_rt = _rope_tables(pos_b, inv_a, sgn_a, inv_r, tm)
    qdec, kdec, cdec = _retention_constants(tm)
    zero_bias = jnp.zeros((1, D_MODEL), F32)
    norm_w4 = norm_w.reshape(DEPTH, 3, 1, d)
    w_in_bf = ffn_w_in.astype(BF16)
    w_out_bf = ffn_w_out.astype(BF16)

    for i in range(DEPTH):
        kind, slot = i % N_MIXERS, i // N_MIXERS
        xf = _ffn(xf, norm_w4, w_in_bf, w_out_bf, i, 0, tm=tm)
        nw = norm_w[i, 1].reshape(1, d)
        if kind == 0:
            wt, bt, wk, bk = _attn_weight_layout(attn_w_qkv[slot], attn_b_qkv[slot], tm)
            qt, k, vt = _attn_proj(xf, nw, wt, bt, wk, bk, cos_a, sin_a, cos_at, sin_at, tm=tm)
            yt = _attn_core(qt, k, vt, attn_sinks[slot], batch=batch, seq=seq)
            mix = (yt, attn_w_o[slot].astype(BF16), attn_b_o[slot].reshape(1, d), True)
        elif kind == 1:
            wt, wk, wg, bg = _mlstm_weight_layout(mlstm_w_in[slot], mlstm_b_if[slot])
            qt, k, vt, ot, src, gt = _mlstm_proj(xf, nw, wt, wk, wg, bg, tm=tm)
            nwt = jnp.broadcast_to(mlstm_norm_w[slot][:, None], (M_V_COLS, M_CHUNK))
            yt = _mlstm_core(qt, k, vt, ot, src, gt, nwt, batch=batch, seq=seq)
            mix = (yt, mlstm_w_out[slot].astype(BF16), zero_bias, True)
        else:
            w, wkt = _ret_weight_layout(ret_w_in[slot])
            q, kt, v, g = _ret_proj(xf, nw, w, wkt, cos_r, sin_r, cos_rt, sin_rt, qdec, kdec, tm=tm)
            y = _ret_core(q, kt, v, g, ret_norm_w[slot].reshape(1, R_V_COLS), cdec, batch=batch, seq=seq)
            mix = (y, ret_w_out[slot].astype(BF16), zero_bias, False)
        last = i == DEPTH - 1
        xf = _ffn(xf, norm_w4, w_in_bf, w_out_bf, i, 1, tm=tm, mix=mix,
                  final_w=final_norm_w.reshape(1, d) if last else None)
    return xf.reshape(batch, seq, d)
```

```python
import functools

import jax
import jax.numpy as jnp
import numpy as np
from jax import lax
from jax.experimental import pallas as pl
from jax.experimental.pallas import tpu as pltpu

F32 = jnp.float32
BF16 = jnp.bfloat16

D_MODEL = 1024
DEPTH = 4
N_MIXERS = 3
EPS = 1e-6
D_FF = 2816
A_HEADS = 16
A_KV_HEADS = 4
A_GROUP = A_HEADS // A_KV_HEADS
A_HEAD_DIM = 64
A_WINDOW = 128
A_ROPE_DIM = 16
ROPE_THETA = 500000.0
M_HEADS = 8
M_QK_DIM = 64
M_V_DIM = 128
M_GATE_CAP = 15.0
R_HEADS = 4
R_QK_DIM = 256
R_V_DIM = 512
R_THETA = 10000.0

LANES = 128
SUBLANES = 8
BF16_ROWS = 16
VMEM_LIMIT = 56 * 1024 * 1024

ROW_TILE = 512
FF_CHUNK = 256
FFN_ROW_TILE = 1024
FFN_ROW_SPLIT = 2
A_CHUNK = A_WINDOW
M_CHUNK = 128
R_CHUNK = 256


def _params(n_axes):
    return pltpu.CompilerParams(
        dimension_semantics=("arbitrary",) * n_axes,
        vmem_limit_bytes=VMEM_LIMIT,
    )


def _resident(shape):
    nd = len(shape)
    return pl.BlockSpec(shape, lambda *_: (0,) * nd, pipeline_mode=pl.Buffered(1))


def _resident_slice(lead, shape):
    nd = len(shape)
    return pl.BlockSpec((None,) * len(lead) + tuple(shape), lambda *_: tuple(lead) + (0,) * nd,
                        pipeline_mode=pl.Buffered(1))


def _rows(tm, width):
    return pl.BlockSpec((tm, width), lambda i: (i, 0))


def _cols(height, tm):
    return pl.BlockSpec((height, tm), lambda i: (0, i))


def _rms(x, w):
    ms = jnp.mean(x * x, axis=-1, keepdims=True)
    return x * lax.rsqrt(ms + EPS) * w


def _dot(a, b):
    return jnp.dot(a, b, preferred_element_type=F32)


def _dot_nt(a, b):
    return lax.dot_general(a, b, (((1,), (1,)), ((), ())), preferred_element_type=F32)


def _dot_tn(a, b):
    return lax.dot_general(a, b, (((0,), (0,)), ((), ())), preferred_element_type=F32)


def _tables_kernel(pos_ref, inv_a_ref, sgn_a_ref, inv_r_ref,
                   cos_a, sin_a, cos_r, sin_r, cos_at, sin_at, cos_rt, sin_rt):
    half = A_ROPE_DIM // 2
    pos = pos_ref[...]
    ang_a = pos * inv_a_ref[...]
    ca = jnp.cos(ang_a)
    sa = jnp.sin(ang_a) * sgn_a_ref[...]
    cos_a[...] = ca
    sin_a[...] = sa
    cos_at[...] = ca.T[:half]
    sin_at[...] = sa.T[half:2 * half]
    ang_r = pos * inv_r_ref[...]
    cr = jnp.cos(ang_r)
    sr = jnp.sin(ang_r)
    cos_r[...] = cr
    sin_r[...] = sr
    cos_rt[...] = cr.T
    sin_rt[...] = sr.T


def _rope_tables(pos_b, inv_a, sgn_a, inv_r, tm):
    t = pos_b.shape[0]
    half = A_ROPE_DIM // 2
    tok = jax.ShapeDtypeStruct((t, LANES), F32)
    return pl.pallas_call(
        _tables_kernel,
        grid=(t // tm,),
        in_specs=[_rows(tm, LANES), _resident((1, LANES)), _resident((1, LANES)), _resident((1, LANES))],
        out_specs=[_rows(tm, LANES)] * 4 + [_cols(half, tm), _cols(half, tm), _cols(LANES, tm), _cols(LANES, tm)],
        out_shape=[tok] * 4 + [jax.ShapeDtypeStruct((half, t), F32)] * 2 + [jax.ShapeDtypeStruct((LANES, t), F32)] * 2,
        compiler_params=_params(1),
        name="rope_tables",
    )(pos_b, inv_a, sgn_a, inv_r)


def _ffn_kernel(*refs, mix_in, mix_feature_major, final_norm):
    refs = list(refs)
    x_ref = refs.pop(0)
    if mix_in:
        y_ref, wo_ref, bo_ref = refs.pop(0), refs.pop(0), refs.pop(0)
    nw_ref, win_ref, wout_ref = refs.pop(0), refs.pop(0), refs.pop(0)
    fw_ref = refs.pop(0) if final_norm else None
    (o_ref,) = refs

    tm = x_ref.shape[0]
    sub = tm // FFN_ROW_SPLIT
    for part in range(FFN_ROW_SPLIT):
        rows = slice(part * sub, (part + 1) * sub)
        x = x_ref[rows, :]
        if mix_in:
            if mix_feature_major:
                mixed = _dot_tn(y_ref[:, rows], wo_ref[...])
            else:
                mixed = _dot(y_ref[rows, :], wo_ref[...])
            x = x + (mixed + bo_ref[...])
        h = _rms(x, nw_ref[...]).astype(BF16)
        acc = jnp.zeros_like(x)
        for c in range(D_FF // FF_CHUNK):
            lo = c * FF_CHUNK
            g = _dot(h, win_ref[:, lo:lo + FF_CHUNK])
            u = _dot(h, win_ref[:, D_FF + lo:D_FF + lo + FF_CHUNK])
            a = (g * jax.nn.sigmoid(g) * u).astype(BF16)
            acc = acc + _dot(a, wout_ref[lo:lo + FF_CHUNK, :])
        y = x + 0.5 * acc
        if final_norm:
            y = _rms(y, fw_ref[...])
        o_ref[rows, :] = y


def _ffn(x, norm_w, w_in, w_out, layer, which, *, tm, mix=None, final_w=None):
    t = x.shape[0]
    in_specs = [_rows(tm, D_MODEL)]
    args = [x]
    feature_major = False
    if mix is not None:
        y, w_o, b_o, feature_major = mix
        kdim = w_o.shape[0]
        assert y.shape == ((kdim, t) if feature_major else (t, kdim))
        in_specs += [_cols(kdim, tm) if feature_major else _rows(tm, kdim),
                     _resident((kdim, D_MODEL)), _resident((1, D_MODEL))]
        args += [y, w_o, b_o]
    in_specs += [_resident_slice((layer, 2 * which), (1, D_MODEL)),
                 _resident_slice((layer, which), (D_MODEL, 2 * D_FF)),
                 _resident_slice((layer, which), (D_FF, D_MODEL))]
    args += [norm_w, w_in, w_out]
    if final_w is not None:
        in_specs.append(_resident((1, D_MODEL)))
        args.append(final_w)
    return pl.pallas_call(
        functools.partial(_ffn_kernel, mix_in=mix is not None, mix_feature_major=feature_major,
                          final_norm=final_w is not None),
        grid=(t // tm,),
        in_specs=in_specs,
        out_specs=_rows(tm, D_MODEL),
        out_shape=jax.ShapeDtypeStruct((t, D_MODEL), F32),
        compiler_params=_params(1),
        name="ffn_mix" if mix is not None else "ffn",
    )(*args)


A_Q_ROWS = A_HEADS * A_HEAD_DIM
A_KV_DIM = A_KV_HEADS * A_HEAD_DIM
A_PROJ_ROWS = 256


def _attn_proj_kernel(x_ref, nw_ref, wt_ref, bt_ref, wk_ref, bk_ref, cos_ref, sin_ref, cost_ref, sint_ref,
                      qt_ref, k_ref, vt_ref):
    tm = x_ref.shape[0]
    half = A_ROPE_DIM // 2
    h = _rms(x_ref[...], nw_ref[...]).astype(BF16)

    cos = cos_ref[...]
    sin = sin_ref[...]
    lane = lax.broadcasted_iota(jnp.int32, (tm, LANES), 1) & (A_HEAD_DIM - 1)
    upper = (lane >= half) & (lane < A_ROPE_DIM)
    k = _dot(h, wk_ref[...]) + bk_ref[...]
    for j in range(A_KV_DIM // LANES):
        piece = k[:, j * LANES:(j + 1) * LANES]
        partner = jnp.where(upper, pltpu.roll(piece, half, 1), pltpu.roll(piece, LANES - half, 1))
        k_ref[:, j * LANES:(j + 1) * LANES] = (piece * cos + partner * sin).astype(BF16)

    cos_t = cost_ref[...]
    sin_t = sint_ref[...]
    q_scale = A_HEAD_DIM ** -0.5
    for c in range((A_Q_ROWS + A_KV_DIM) // A_PROJ_ROWS):
        lo = c * A_PROJ_ROWS
        part = _dot_nt(wt_ref[lo:lo + A_PROJ_ROWS, :], h) + bt_ref[lo:lo + A_PROJ_ROWS, :]
        if lo < A_Q_ROWS:
            for j in range(A_PROJ_ROWS // A_HEAD_DIM):
                base = j * A_HEAD_DIM
                x1 = part[base:base + half]
                x2 = part[base + half:base + 2 * half]
                rot = jnp.concatenate([x1 * cos_t - x2 * sin_t, x2 * cos_t + x1 * sin_t,
                                       part[base + 2 * half:base + A_HEAD_DIM]], axis=0)
                qt_ref[lo + base:lo + base + A_HEAD_DIM, :] = (rot * q_scale).astype(BF16)
        else:
            vt_ref[...] = part.astype(BF16)


def _attn_proj(x, nw, wt, bt, wk, bk, cos_a, sin_a, cos_at, sin_at, *, tm):
    t = x.shape[0]
    half = A_ROPE_DIM // 2
    rows = A_Q_ROWS + A_KV_DIM
    return pl.pallas_call(
        _attn_proj_kernel,
        grid=(t // tm,),
        in_specs=[_rows(tm, D_MODEL), _resident((1, D_MODEL)), _resident((rows, D_MODEL)), _resident((rows, tm)),
                  _resident((D_MODEL, A_KV_DIM)), _resident((1, A_KV_DIM)),
                  _rows(tm, LANES), _rows(tm, LANES), _cols(half, tm), _cols(half, tm)],
        out_specs=[_cols(A_Q_ROWS, tm), _rows(tm, A_KV_DIM), _cols(A_KV_DIM, tm)],
        out_shape=[jax.ShapeDtypeStruct((A_Q_ROWS, t), BF16),
                   jax.ShapeDtypeStruct((t, A_KV_DIM), BF16),
                   jax.ShapeDtypeStruct((A_KV_DIM, t), BF16)],
        compiler_params=_params(1),
        name="attn_proj",
    )(x, nw, wt, bt, wk, bk, cos_a, sin_a, cos_at, sin_at)


def _attn_core_kernel(sink_ref, qt_ref, kp_ref, kc_ref, vtp_ref, vtc_ref, ot_ref):
    L = A_CHUNK
    blk = pl.program_id(1)
    key = lax.broadcasted_iota(jnp.int32, (L, L), 0)
    qry = lax.broadcasted_iota(jnp.int32, (L, L), 1)
    from_prev = key > qry
    prev_bias = jnp.where(blk > 0, 0.0, -jnp.inf).astype(F32)
    zero_q = jnp.zeros((A_HEAD_DIM, A_GROUP * L), BF16)
    scores, probs_of = {}, {}

    def score(g):
        col = g // 2
        kk = jnp.concatenate([kp_ref[:, col * LANES:(col + 1) * LANES],
                              kc_ref[:, col * LANES:(col + 1) * LANES]], axis=0)
        q_cat = jnp.concatenate([qt_ref[(A_GROUP * g + j) * A_HEAD_DIM:(A_GROUP * g + j + 1) * A_HEAD_DIM, :]
                                 for j in range(A_GROUP)], axis=1)
        q_ext = jnp.concatenate([q_cat, zero_q] if g % 2 == 0 else [zero_q, q_cat], axis=0)
        scores[g] = _dot(kk, q_ext)

    def softmax(g):
        s = scores.pop(g)
        probs = []
        for j in range(A_GROUP):
            sink = sink_ref[A_GROUP * g + j]
            s_prev = s[:L, j * L:(j + 1) * L]
            s_own = s[L:, j * L:(j + 1) * L]
            sc = jnp.where(from_prev, s_prev + prev_bias, s_own)
            mx = jnp.maximum(jnp.max(sc, axis=0, keepdims=True), sink)
            p = jnp.exp(sc - mx)
            denom = jnp.sum(p, axis=0, keepdims=True) + jnp.exp(sink - mx)
            pn = p * (1.0 / denom)
            zero = jnp.zeros_like(pn)
            probs.append(jnp.concatenate([jnp.where(from_prev, pn, zero), jnp.where(from_prev, zero, pn)],
                                         axis=0).astype(BF16))
        probs_of[g] = jnp.concatenate(probs, axis=1)

    def weighted_values(g):
        vt = jnp.concatenate([vtp_ref[g * A_HEAD_DIM:(g + 1) * A_HEAD_DIM, :],
                              vtc_ref[g * A_HEAD_DIM:(g + 1) * A_HEAD_DIM, :]], axis=1)
        out = _dot(vt, probs_of.pop(g))
        for j in range(A_GROUP):
            head = A_GROUP * g + j
            ot_ref[head * A_HEAD_DIM:(head + 1) * A_HEAD_DIM, :] = out[:, j * L:(j + 1) * L].astype(BF16)

    score(0)
    for g in range(A_KV_HEADS + 1):
        if g + 1 < A_KV_HEADS:
            score(g + 1)
        if g < A_KV_HEADS:
            softmax(g)
        if g >= 1:
            weighted_values(g - 1)


def _attn_core(qt, k, vt, sinks, *, batch, seq):
    L = A_CHUNK
    nb = seq // L
    cur = lambda b, i: (b * nb + i, 0)
    prev = lambda b, i: (b * nb + jnp.maximum(i - 1, 0), 0)
    cur_t = lambda b, i: (0, b * nb + i)
    prev_t = lambda b, i: (0, b * nb + jnp.maximum(i - 1, 0))
    return pl.pallas_call(
        _attn_core_kernel,
        grid=(batch, nb),
        in_specs=[pl.BlockSpec(memory_space=pltpu.SMEM),
                  pl.BlockSpec((A_Q_ROWS, L), cur_t),
                  pl.BlockSpec((L, A_KV_DIM), prev), pl.BlockSpec((L, A_KV_DIM), cur),
                  pl.BlockSpec((A_KV_DIM, L), prev_t), pl.BlockSpec((A_KV_DIM, L), cur_t)],
        out_specs=pl.BlockSpec((A_Q_ROWS, L), cur_t),
        out_shape=jax.ShapeDtypeStruct((A_Q_ROWS, batch * seq), BF16),
        compiler_params=_params(2),
        name="attn_core",
    )(sinks, qt, k, k, vt, vt)


M_QK_COLS = M_HEADS * M_QK_DIM
M_V_COLS = M_HEADS * M_V_DIM
M_T_ROWS = M_QK_COLS + 2 * M_V_COLS
M_PROJ_ROWS = 512
M_STATE_ROWS = M_V_DIM + BF16_ROWS


def _mlstm_proj_kernel(x_ref, nw_ref, wt_ref, wk_ref, wg_ref, bg_ref,
                       qt_ref, k_ref, vt_ref, ot_ref, src_ref, gt_ref):
    tm = x_ref.shape[0]
    H = M_HEADS
    h = _rms(x_ref[...], nw_ref[...]).astype(BF16)
    k_ref[...] = _dot(h, wk_ref[...]).astype(BF16)

    gates = _dot(h, wg_ref[...]) + bg_ref[...]
    for c in range(M_T_ROWS // M_PROJ_ROWS):
        lo = c * M_PROJ_ROWS
        part = _dot_nt(wt_ref[lo:lo + M_PROJ_ROWS, :], h)
        if lo < M_QK_COLS:
            qt_ref[...] = (part * (M_QK_DIM ** -0.5)).astype(BF16)
        elif lo < M_QK_COLS + M_V_COLS:
            vt_ref[lo - M_QK_COLS:lo - M_QK_COLS + M_PROJ_ROWS, :] = part.astype(BF16)
        else:
            o_lo = lo - M_QK_COLS - M_V_COLS
            ot_ref[o_lo:o_lo + M_PROJ_ROWS, :] = jax.nn.sigmoid(part)

    gates = M_GATE_CAP * jnp.tanh(gates / M_GATE_CAP)
    lane = lax.broadcasted_iota(jnp.int32, (tm, LANES), 1)
    b = jnp.where((lane >= H) & (lane < 2 * H), jax.nn.log_sigmoid(gates), 0.0)
    time = lax.broadcasted_iota(jnp.int32, (tm, LANES), 0) & (M_CHUNK - 1)
    shift = 1
    while shift < M_CHUNK:
        b = b + jnp.where(time >= shift, pltpu.roll(b, shift, 0), 0.0)
        shift *= 2
    src_ref[...] = gates - pltpu.roll(b, LANES - H, 1)
    gt_ref[...] = jnp.where(lane < H, gates, b).T[:2 * H]


def _mlstm_proj(x, nw, wt, wk, wg, bg, *, tm):
    t = x.shape[0]
    assert tm % M_CHUNK == 0 and M_CHUNK & (M_CHUNK - 1) == 0
    return pl.pallas_call(
        _mlstm_proj_kernel,
        grid=(t // tm,),
        in_specs=[_rows(tm, D_MODEL), _resident((1, D_MODEL)), _resident((M_T_ROWS, D_MODEL)),
                  _resident((D_MODEL, M_QK_COLS)), _resident((D_MODEL, LANES)), _resident((1, LANES))],
        out_specs=[_cols(M_QK_COLS, tm), _rows(tm, M_QK_COLS), _cols(M_V_COLS, tm),
                   _cols(M_V_COLS, tm), _rows(tm, LANES), _cols(2 * M_HEADS, tm)],
        out_shape=[jax.ShapeDtypeStruct((M_QK_COLS, t), BF16),
                   jax.ShapeDtypeStruct((t, M_QK_COLS), BF16),
                   jax.ShapeDtypeStruct((M_V_COLS, t), BF16),
                   jax.ShapeDtypeStruct((M_V_COLS, t), F32),
                   jax.ShapeDtypeStruct((t, LANES), F32),
                   jax.ShapeDtypeStruct((2 * M_HEADS, t), F32)],
        compiler_params=_params(1),
        name="mlstm_proj",
    )(x, nw, wt, wk, wg, bg)


def _mlstm_core_kernel(qt_ref, k_ref, vt_ref, ot_ref, src_ref, gt_ref, nwt_ref, yt_ref, c_sc, m_sc):
    L = M_CHUNK
    H = M_HEADS

    @pl.when(pl.program_id(1) == 0)
    def _():
        c_sc[...] = jnp.zeros_like(c_sc)
        m_sc[...] = jnp.zeros_like(m_sc)

    src_i = lax.broadcasted_iota(jnp.int32, (L, L), 0)
    tgt_i = lax.broadcasted_iota(jnp.int32, (L, L), 1)
    causal = src_i <= tgt_i
    ones_rows = (lax.broadcasted_iota(jnp.int32, (BF16_ROWS, L), 0) == 0).astype(BF16)
    zero_q = jnp.zeros((M_QK_DIM, L), BF16)
    low = lax.broadcasted_iota(jnp.int32, (L, LANES), 1) < M_QK_DIM

    st = [dict() for _ in range(H)]

    def score(h):
        d = st[h]
        pair = h // 2
        k_pair = k_ref[:, pair * LANES:(pair + 1) * LANES]
        d["k"] = jnp.where(low if h % 2 == 0 else jnp.logical_not(low), k_pair, jnp.zeros_like(k_pair))
        qt_h = qt_ref[h * M_QK_DIM:(h + 1) * M_QK_DIM, :]
        d["qt"] = jnp.concatenate([qt_h, zero_q] if h % 2 == 0 else [zero_q, qt_h], axis=0)
        d["v"] = jnp.concatenate([vt_ref[h * M_V_DIM:(h + 1) * M_V_DIM, :], ones_rows], axis=0)
        d["state"] = c_sc[h]
        d["s"] = _dot(d["k"], d["qt"])
        d["carry"] = _dot(d["state"].astype(BF16), d["qt"])

    def gate(h):
        d = st[h]
        src_col = src_ref[:, h:h + 1]
        d["b"] = gt_ref[H + h:H + h + 1, :]
        d["m_prev"] = m_sc[h][:, :1]
        dmat = jnp.where(causal, src_col + d["b"], -jnp.inf)
        inter = d["b"] + d["m_prev"]
        d["m_row"] = jnp.maximum(inter, jnp.max(dmat, axis=0, keepdims=True))
        d["w_intra"] = jnp.exp(dmat - d["m_row"])
        d["w_inter"] = jnp.exp(inter - d["m_row"])

    def emit(h):
        d = st[h]
        tot = _dot(d["v"], (d["s"] * d["w_intra"]).astype(BF16)) + d["w_inter"] * d["carry"]
        num = tot[:M_V_DIM]
        nq = tot[M_V_DIM:M_V_DIM + 1]
        h_out = num * (1.0 / jnp.maximum(jnp.abs(nq), jnp.exp(-d["m_row"])))
        h_out = h_out * lax.rsqrt(jnp.mean(h_out * h_out, axis=0, keepdims=True) + EPS)
        rows = slice(h * M_V_DIM, (h + 1) * M_V_DIM)
        yt_ref[rows, :] = (ot_ref[rows, :] * (h_out * nwt_ref[rows, :])).astype(BF16)

    def update(h):
        d = st[h]
        b_last = d["b"][:, L - 1:L]
        dec = b_last + (gt_ref[h:h + 1, :] - d["b"])
        m_new = jnp.maximum(b_last + d["m_prev"], jnp.max(dec, axis=-1, keepdims=True))
        wk = jnp.exp(dec - m_new)
        keep = jnp.exp(b_last + d["m_prev"] - m_new)
        v_scaled = (d["v"].astype(F32) * wk).astype(BF16)
        c_sc[h] = keep * d["state"] + _dot(v_scaled, d["k"])
        m_sc[h] = jnp.broadcast_to(m_new, (1, LANES))
        d.clear()

    score(0)
    for h in range(H + 1):
        if h + 1 < H:
            score(h + 1)
        if h < H:
            gate(h)
            emit(h)
        if h >= 1:
            update(h - 1)


def _mlstm_core(qt, k, vt, ot, src, gt, nwt, *, batch, seq):
    L = M_CHUNK
    nb = seq // L
    cur = lambda b, i: (b * nb + i, 0)
    cur_t = lambda b, i: (0, b * nb + i)
    return pl.pallas_call(
        _mlstm_core_kernel,
        grid=(batch, nb),
        in_specs=[pl.BlockSpec((M_QK_COLS, L), cur_t), pl.BlockSpec((L, M_QK_COLS), cur),
                  pl.BlockSpec((M_V_COLS, L), cur_t), pl.BlockSpec((M_V_COLS, L), cur_t),
                  pl.BlockSpec((L, LANES), cur), pl.BlockSpec((2 * M_HEADS, L), cur_t), _resident((M_V_COLS, L))],
        out_specs=pl.BlockSpec((M_V_COLS, L), cur_t),
        out_shape=jax.ShapeDtypeStruct((M_V_COLS, batch * seq), BF16),
        scratch_shapes=[pltpu.VMEM((M_HEADS, M_STATE_ROWS, LANES), F32),
                        pltpu.VMEM((M_HEADS, 1, LANES), F32)],
        compiler_params=_params(2),
        name="mlstm_core",
    )(qt, k, vt, ot, src, gt, nwt)


R_QK_COLS = R_HEADS * R_QK_DIM
R_V_COLS = R_HEADS * R_V_DIM
R_PROJ_COLS = R_QK_COLS + 2 * R_V_COLS
R_PROJ_CHUNK = 512


def _ret_proj_kernel(x_ref, nw_ref, w_ref, wkt_ref, cos_ref, sin_ref, cost_ref, sint_ref, qdec_ref, kdec_ref,
                     q_ref, kt_ref, v_ref, g_ref):
    h = _rms(x_ref[...], nw_ref[...]).astype(BF16)
    cos = cos_ref[...]
    sin = sin_ref[...]
    half = R_QK_DIM // 2
    for c in range(R_QK_COLS // R_PROJ_CHUNK):
        lo = c * R_PROJ_CHUNK
        t = _dot(h, w_ref[:, lo:lo + R_PROJ_CHUNK])
        for hh in range(R_PROJ_CHUNK // R_QK_DIM):
            head = lo // R_QK_DIM + hh
            dec = qdec_ref[head]
            x1 = t[:, hh * R_QK_DIM:hh * R_QK_DIM + half]
            x2 = t[:, hh * R_QK_DIM + half:(hh + 1) * R_QK_DIM]
            base = head * R_QK_DIM
            q_ref[:, base:base + half] = ((x1 * cos - x2 * sin) * dec).astype(BF16)
            q_ref[:, base + half:base + R_QK_DIM] = ((x2 * cos + x1 * sin) * dec).astype(BF16)
    for c in range(2 * R_V_COLS // R_PROJ_CHUNK):
        lo = c * R_PROJ_CHUNK
        t = _dot(h, w_ref[:, R_QK_COLS + lo:R_QK_COLS + lo + R_PROJ_CHUNK])
        if lo < R_V_COLS:
            v_ref[:, lo:lo + R_PROJ_CHUNK] = t.astype(BF16)
        else:
            g_ref[:, lo - R_V_COLS:lo - R_V_COLS + R_PROJ_CHUNK] = t * jax.nn.sigmoid(t)
    cos_t = cost_ref[...]
    sin_t = sint_ref[...]
    for head in range(R_HEADS):
        base = head * R_QK_DIM
        kt = _dot_nt(wkt_ref[base:base + R_QK_DIM, :], h)
        dec = kdec_ref[head:head + 1, :]
        x1 = kt[:half]
        x2 = kt[half:]
        kt_ref[base:base + half, :] = ((x1 * cos_t - x2 * sin_t) * dec).astype(BF16)
        kt_ref[base + half:base + R_QK_DIM, :] = ((x2 * cos_t + x1 * sin_t) * dec).astype(BF16)


def _ret_proj(x, nw, w, wkt, cos_r, sin_r, cos_rt, sin_rt, qdec, kdec, *, tm):
    t = x.shape[0]
    return pl.pallas_call(
        _ret_proj_kernel,
        grid=(t // tm,),
        in_specs=[_rows(tm, D_MODEL), _resident((1, D_MODEL)), _resident((D_MODEL, R_PROJ_COLS)),
                  _resident((R_QK_COLS, D_MODEL)),
                  _rows(tm, LANES), _rows(tm, LANES), _cols(LANES, tm), _cols(LANES, tm),
                  _resident((R_HEADS, tm, LANES)), _resident((SUBLANES, tm))],
        out_specs=[_rows(tm, R_QK_COLS), _cols(R_QK_COLS, tm), _rows(tm, R_V_COLS), _rows(tm, R_V_COLS)],
        out_shape=[jax.ShapeDtypeStruct((t, R_QK_COLS), BF16),
                   jax.ShapeDtypeStruct((R_QK_COLS, t), BF16),
                   jax.ShapeDtypeStruct((t, R_V_COLS), BF16),
                   jax.ShapeDtypeStruct((t, R_V_COLS), F32)],
        compiler_params=_params(1),
        name="ret_proj",
    )(x, nw, w, wkt, cos_r, sin_r, cos_rt, sin_rt, qdec, kdec)


def _ret_core_kernel(cdec_ref, q_ref, kt_ref, v_ref, g_ref, nw_ref, y_ref, s_sc):
    L = R_CHUNK

    @pl.when(pl.program_id(1) == 0)
    def _():
        s_sc[...] = jnp.zeros_like(s_sc)

    row = lax.broadcasted_iota(jnp.int32, (L, L), 0)
    col = lax.broadcasted_iota(jnp.int32, (L, L), 1)
    causal = row >= col
    nw = nw_ref[...]
    st = [dict() for _ in range(R_HEADS)]

    def score(h):
        d = st[h]
        d["q"] = q_ref[:, h * R_QK_DIM:(h + 1) * R_QK_DIM]
        d["kt"] = kt_ref[h * R_QK_DIM:(h + 1) * R_QK_DIM, :]
        d["v"] = v_ref[:, h * R_V_DIM:(h + 1) * R_V_DIM]
        d["state"] = s_sc[h]
        d["scores"] = _dot(d["q"], d["kt"])
        d["carry"] = _dot(d["q"], d["state"].astype(BF16))

    def emit(h):
        d = st[h]
        scores = jnp.where(causal, d["scores"], 0.0)
        y = _dot(scores.astype(BF16), d["v"]) + d["carry"]
        s_sc[h] = cdec_ref[h] * (d["state"] + _dot(d["kt"], d["v"]))
        d.clear()
        mu = jnp.mean(y, axis=-1, keepdims=True)
        yc = y - mu
        yn = yc * lax.rsqrt(jnp.mean(yc * yc, axis=-1, keepdims=True) + EPS)
        gate = g_ref[:, h * R_V_DIM:(h + 1) * R_V_DIM]
        out = gate * (yn * nw[:, h * R_V_DIM:(h + 1) * R_V_DIM])
        y_ref[:, h * R_V_DIM:(h + 1) * R_V_DIM] = out.astype(BF16)

    score(0)
    for h in range(R_HEADS):
        if h + 1 < R_HEADS:
            score(h + 1)
        emit(h)


def _ret_core(q, kt, v, g, nw, cdec, *, batch, seq):
    L = R_CHUNK
    nb = seq // L
    cur = lambda b, i: (b * nb + i, 0)
    cur_t = lambda b, i: (0, b * nb + i)
    return pl.pallas_call(
        _ret_core_kernel,
        grid=(batch, nb),
        in_specs=[pl.BlockSpec(memory_space=pltpu.SMEM),
                  pl.BlockSpec((L, R_QK_COLS), cur), pl.BlockSpec((R_QK_COLS, L), cur_t),
                  pl.BlockSpec((L, R_V_COLS), cur), pl.BlockSpec((L, R_V_COLS), cur),
                  _resident((1, R_V_COLS))],
        out_specs=pl.BlockSpec((L, R_V_COLS), cur),
        out_shape=jax.ShapeDtypeStruct((batch * seq, R_V_COLS), BF16),
        scratch_shapes=[pltpu.VMEM((R_HEADS, R_QK_DIM, R_V_DIM), F32)],
        compiler_params=_params(2),
        name="ret_core",
    )(cdec, q, kt, v, g, nw)


def _rope_constants():
    inv = 1.0 / (ROPE_THETA ** (jnp.arange(0, A_ROPE_DIM, 2, dtype=F32) / A_ROPE_DIM))
    lane = np.arange(LANES) % A_HEAD_DIM
    idx = jnp.asarray(lane % (A_ROPE_DIM // 2))
    rot = jnp.asarray(lane < A_ROPE_DIM)
    inv_a = jnp.where(rot, inv[idx], 0.0).reshape(1, LANES)
    sgn = np.where(lane < A_ROPE_DIM // 2, -1.0, np.where(lane < A_ROPE_DIM, 1.0, 0.0))
    sgn_a = jnp.asarray(sgn, F32).reshape(1, LANES)
    inv_r = (1.0 / (R_THETA ** jnp.linspace(0.0, 1.0, R_QK_DIM // 2, dtype=F32))).reshape(1, LANES)
    return inv_a, sgn_a, inv_r


def _retention_constants(tm):
    L = R_CHUNK
    log_gamma = jnp.log(1.0 - 2.0 ** (-5.0 - jnp.arange(R_HEADS, dtype=F32)))
    j = jnp.asarray(np.arange(tm) % L, F32)
    q_decay = jnp.exp((j + 1.0) * log_gamma[:, None])
    k_decay = jnp.exp(-(j + 1.0) * log_gamma[:, None]) * (R_QK_DIM ** -0.5)
    chunk_decay = jnp.exp(L * log_gamma)
    qdec = jnp.broadcast_to(q_decay[:, :, None], (R_HEADS, tm, LANES))
    kdec = jnp.pad(k_decay, ((0, SUBLANES - R_HEADS), (0, 0)))
    return qdec, kdec, chunk_decay


def _attn_weight_layout(w_qkv, b_qkv, tm):
    nq = A_Q_ROWS
    qv = np.concatenate([np.arange(nq), np.arange(nq + A_KV_DIM, nq + 2 * A_KV_DIM)])
    wt = w_qkv[:, qv].T.astype(BF16)
    bt = jnp.broadcast_to(b_qkv[qv][:, None], (nq + A_KV_DIM, tm))
    wk = w_qkv[:, nq:nq + A_KV_DIM].astype(BF16)
    bk = b_qkv[nq:nq + A_KV_DIM].reshape(1, A_KV_DIM)
    return wt, bt, wk, bk


def _ret_weight_layout(w_in):
    head = np.concatenate([np.arange(0, R_QK_DIM, 2), np.arange(1, R_QK_DIM, 2)])
    perm = np.concatenate([h * R_QK_DIM + head for h in range(R_HEADS)])
    w = jnp.concatenate([w_in[:, perm], w_in[:, 2 * R_QK_COLS:]], axis=-1).astype(BF16)
    wkt = w_in[:, R_QK_COLS + perm].T.astype(BF16)
    return w, wkt


def _mlstm_weight_layout(w_in, b_if):
    pad = LANES - 2 * M_HEADS
    gate_lo = 2 * M_QK_COLS + 2 * M_V_COLS
    wt = jnp.concatenate([w_in[:, :M_QK_COLS], w_in[:, 2 * M_QK_COLS:gate_lo]], axis=-1).T.astype(BF16)
    wk = w_in[:, M_QK_COLS:2 * M_QK_COLS].astype(BF16)
    wg = jnp.pad(w_in[:, gate_lo:], ((0, 0), (0, pad))).astype(BF16)
    return wt, wk, wg, jnp.pad(b_if, (0, pad)).reshape(1, LANES)


def kernel(x, positions, norm_w, final_norm_w, ffn_w_in, ffn_w_out, attn_w_qkv, attn_b_qkv, attn_w_o, attn_b_o,
           attn_sinks, mlstm_w_in, mlstm_b_if, mlstm_norm_w, mlstm_w_out, ret_w_in, ret_norm_w, ret_w_out):
    batch, seq, d = x.shape
    t = batch * seq
    tm = min(ROW_TILE, t)
    ftm = min(FFN_ROW_TILE, t)
    assert d == D_MODEL and t % tm == 0 and tm % R_CHUNK == 0
    assert seq % R_CHUNK == 0 and seq % M_CHUNK == 0 and seq % A_CHUNK == 0

    xf = x.reshape(t, d)
    pos_b = jnp.broadcast_to(positions.astype(F32).reshape(t, 1), (t, LANES))
    inv_a, sgn_a, inv_r = _rope_constants()
    cos_a, sin_a, cos_r, sin_r, cos_at, sin_at, cos_rt, sin_rt = _rope_tables(pos_b, inv_a, sgn_a, inv_r, tm)
    qdec, kdec, cdec = _retention_constants(tm)
    zero_bias = jnp.zeros((1, D_MODEL), F32)
    norm_w4 = norm_w.reshape(DEPTH, 3, 1, d)
    w_in_bf = ffn_w_in.astype(BF16)
    w_out_bf = ffn_w_out.astype(BF16)

    for i in range(DEPTH):
        kind, slot = i % N_MIXERS, i // N_MIXERS
        xf = _ffn(xf, norm_w4, w_in_bf, w_out_bf, i, 0, tm=ftm)
        nw = norm_w[i, 1].reshape(1, d)
        if kind == 0:
            wt, bt, wk, bk = _attn_weight_layout(attn_w_qkv[slot], attn_b_qkv[slot], tm)
            qt, k, vt = _attn_proj(xf, nw, wt, bt, wk, bk, cos_a, sin_a, cos_at, sin_at, tm=tm)
            yt = _attn_core(qt, k, vt, attn_sinks[slot], batch=batch, seq=seq)
            mix = (yt, attn_w_o[slot].astype(BF16), attn_b_o[slot].reshape(1, d), True)
        elif kind == 1:
            wt, wk, wg, bg = _mlstm_weight_layout(mlstm_w_in[slot], mlstm_b_if[slot])
            qt, k, vt, ot, src, gt = _mlstm_proj(xf, nw, wt, wk, wg, bg, tm=tm)
            nwt = jnp.broadcast_to(mlstm_norm_w[slot][:, None], (M_V_COLS, M_CHUNK))
            yt = _mlstm_core(qt, k, vt, ot, src, gt, nwt, batch=batch, seq=seq)
            mix = (yt, mlstm_w_out[slot].astype(BF16), zero_bias, True)
        else:
            w, wkt = _ret_weight_layout(ret_w_in[slot])
            q, kt, v, g = _ret_proj(xf, nw, w, wkt, cos_r, sin_r, cos_rt, sin_rt, qdec, kdec, tm=tm)
            y = _ret_core(q, kt, v, g, ret_norm_w[slot].reshape(1, R_V_COLS), cdec, batch=batch, seq=seq)
            mix = (y, ret_w_out[slot].astype(BF16), zero_bias, False)
        last = i == DEPTH - 1
        xf = _ffn(xf, norm_w4, w_in_bf, w_out_bf, i, 1, tm=ftm, mix=mix,
                  final_w=final_norm_w.reshape(1, d) if last else None)
    return xf.reshape(batch, seq, d)
```

```python
import functools

import jax
import jax.numpy as jnp
import numpy as np
from jax import lax
from jax.experimental import pallas as pl
from jax.experimental.pallas import tpu as pltpu

F32 = jnp.float32
BF16 = jnp.bfloat16

D_MODEL = 1024
DEPTH = 4
N_MIXERS = 3
EPS = 1e-6
D_FF = 2816
A_HEADS = 16
A_KV_HEADS = 4
A_GROUP = A_HEADS // A_KV_HEADS
A_HEAD_DIM = 64
A_WINDOW = 128
A_ROPE_DIM = 16
ROPE_THETA = 500000.0
M_HEADS = 8
M_QK_DIM = 64
M_V_DIM = 128
M_GATE_CAP = 15.0
R_HEADS = 4
R_QK_DIM = 256
R_V_DIM = 512
R_THETA = 10000.0

LANES = 128
SUBLANES = 8
BF16_ROWS = 16
VMEM_LIMIT = 56 * 1024 * 1024

ROW_TILE = 1024
ROW_SPLIT = 2
FF_CHUNK = 256
A_CHUNK = A_WINDOW
M_CHUNK = 128
R_CHUNK = 256


def _params(n_axes):
    return pltpu.CompilerParams(
        dimension_semantics=("arbitrary",) * n_axes,
        vmem_limit_bytes=VMEM_LIMIT,
    )


def _resident(shape):
    nd = len(shape)
    return pl.BlockSpec(shape, lambda *_: (0,) * nd, pipeline_mode=pl.Buffered(1))


def _resident_slice(lead, shape):
    nd = len(shape)
    return pl.BlockSpec((None,) * len(lead) + tuple(shape), lambda *_: tuple(lead) + (0,) * nd,
                        pipeline_mode=pl.Buffered(1))


def _rows(tm, width):
    return pl.BlockSpec((tm, width), lambda i: (i, 0))


def _cols(height, tm):
    return pl.BlockSpec((height, tm), lambda i: (0, i))


def _rms(x, w):
    ms = jnp.mean(x * x, axis=-1, keepdims=True)
    return x * lax.rsqrt(ms + EPS) * w


def _dot(a, b):
    return jnp.dot(a, b, preferred_element_type=F32)


def _dot_nt(a, b):
    return lax.dot_general(a, b, (((1,), (1,)), ((), ())), preferred_element_type=F32)


def _dot_tn(a, b):
    return lax.dot_general(a, b, (((0,), (0,)), ((), ())), preferred_element_type=F32)


def _tables_kernel(pos_ref, inv_a_ref, inv_r_ref, cos_a, sin_a, cos_r, sin_r, cos_at, sin_at, cos_rt, sin_rt):
    tm = pos_ref.shape[1]
    pos = pos_ref[...]
    ang_a = inv_a_ref[...] * pos
    ca = jnp.cos(ang_a)
    sa = jnp.sin(ang_a)
    cos_at[...] = ca
    sin_at[...] = sa
    rest = A_HEAD_DIM - A_ROPE_DIM
    cos_head = jnp.concatenate([ca, ca, jnp.ones((rest, tm), F32)], axis=0)
    sin_head = jnp.concatenate([-sa, sa, jnp.zeros((rest, tm), F32)], axis=0)
    cos_a[...] = jnp.concatenate([cos_head] * (LANES // A_HEAD_DIM), axis=0).T
    sin_a[...] = jnp.concatenate([sin_head] * (LANES // A_HEAD_DIM), axis=0).T
    ang_r = inv_r_ref[...] * pos
    cr = jnp.cos(ang_r)
    sr = jnp.sin(ang_r)
    cos_rt[...] = cr
    sin_rt[...] = sr
    cos_r[...] = cr.T
    sin_r[...] = sr.T


def _rope_tables(pos_row, inv_a, inv_r, tm):
    t = pos_row.shape[1]
    half = A_ROPE_DIM // 2
    tok = jax.ShapeDtypeStruct((t, LANES), F32)
    return pl.pallas_call(
        _tables_kernel,
        grid=(t // tm,),
        in_specs=[_cols(1, tm), _resident((half, tm)), _resident((LANES, tm))],
        out_specs=[_rows(tm, LANES)] * 4 + [_cols(half, tm), _cols(half, tm), _cols(LANES, tm), _cols(LANES, tm)],
        out_shape=[tok] * 4 + [jax.ShapeDtypeStruct((half, t), F32)] * 2 + [jax.ShapeDtypeStruct((LANES, t), F32)] * 2,
        compiler_params=_params(1),
        name="rope_tables",
    )(pos_row, inv_a, inv_r)


def _ffn_kernel(*refs, mix_in, mix_feature_major, final_norm):
    refs = list(refs)
    x_ref = refs.pop(0)
    if mix_in:
        y_ref, wo_ref, bo_ref = refs.pop(0), refs.pop(0), refs.pop(0)
    nw_ref, win_ref, wout_ref = refs.pop(0), refs.pop(0), refs.pop(0)
    fw_ref = refs.pop(0) if final_norm else None
    (o_ref,) = refs

    tm = x_ref.shape[0]
    sub = tm // ROW_SPLIT
    for part in range(ROW_SPLIT):
        rows = slice(part * sub, (part + 1) * sub)
        x = x_ref[rows, :]
        if mix_in:
            if mix_feature_major:
                mixed = _dot_tn(y_ref[:, rows], wo_ref[...])
            else:
                mixed = _dot(y_ref[rows, :], wo_ref[...])
            x = x + (mixed + bo_ref[...])
        h = _rms(x, nw_ref[...]).astype(BF16)
        acc = jnp.zeros_like(x)
        for c in range(D_FF // FF_CHUNK):
            lo = c * FF_CHUNK
            g = _dot(h, win_ref[:, lo:lo + FF_CHUNK])
            u = _dot(h, win_ref[:, D_FF + lo:D_FF + lo + FF_CHUNK])
            a = (g * jax.nn.sigmoid(g) * u).astype(BF16)
            acc = acc + _dot(a, wout_ref[lo:lo + FF_CHUNK, :])
        y = x + 0.5 * acc
        if final_norm:
            y = _rms(y, fw_ref[...])
        o_ref[rows, :] = y


def _ffn(x, norm_w, w_in, w_out, layer, which, *, tm, mix=None, final_w=None):
    t = x.shape[0]
    in_specs = [_rows(tm, D_MODEL)]
    args = [x]
    feature_major = False
    if mix is not None:
        y, w_o, b_o, feature_major = mix
        kdim = w_o.shape[0]
        assert y.shape == ((kdim, t) if feature_major else (t, kdim))
        in_specs += [_cols(kdim, tm) if feature_major else _rows(tm, kdim),
                     _resident((kdim, D_MODEL)), _resident((1, D_MODEL))]
        args += [y, w_o, b_o]
    in_specs += [_resident_slice((layer, 2 * which), (1, D_MODEL)),
                 _resident_slice((layer, which), (D_MODEL, 2 * D_FF)),
                 _resident_slice((layer, which), (D_FF, D_MODEL))]
    args += [norm_w, w_in, w_out]
    if final_w is not None:
        in_specs.append(_resident((1, D_MODEL)))
        args.append(final_w)
    return pl.pallas_call(
        functools.partial(_ffn_kernel, mix_in=mix is not None, mix_feature_major=feature_major,
                          final_norm=final_w is not None),
        grid=(t // tm,),
        in_specs=in_specs,
        out_specs=_rows(tm, D_MODEL),
        out_shape=jax.ShapeDtypeStruct((t, D_MODEL), F32),
        compiler_params=_params(1),
        name="ffn_mix" if mix is not None else "ffn",
    )(*args)


A_Q_ROWS = A_HEADS * A_HEAD_DIM
A_KV_DIM = A_KV_HEADS * A_HEAD_DIM
A_PROJ_ROWS = 256


def _attn_proj_kernel(x_ref, nw_ref, wt_ref, bt_ref, wk_ref, bk_ref, cos_ref, sin_ref, cost_ref, sint_ref,
                      qt_ref, k_ref, vt_ref):
    sub = x_ref.shape[0] // ROW_SPLIT
    half = A_ROPE_DIM // 2
    lane = lax.broadcasted_iota(jnp.int32, (sub, LANES), 1) & (A_HEAD_DIM - 1)
    upper = (lane >= half) & (lane < A_ROPE_DIM)
    q_scale = A_HEAD_DIM ** -0.5
    for split in range(ROW_SPLIT):
        tok = slice(split * sub, (split + 1) * sub)
        h = _rms(x_ref[tok, :], nw_ref[...]).astype(BF16)

        cos = cos_ref[tok, :]
        sin = sin_ref[tok, :]
        k = _dot(h, wk_ref[...]) + bk_ref[...]
        for j in range(A_KV_DIM // LANES):
            piece = k[:, j * LANES:(j + 1) * LANES]
            partner = jnp.where(upper, pltpu.roll(piece, half, 1), pltpu.roll(piece, LANES - half, 1))
            k_ref[tok, j * LANES:(j + 1) * LANES] = (piece * cos + partner * sin).astype(BF16)

        cos_t = cost_ref[:, tok]
        sin_t = sint_ref[:, tok]
        for c in range((A_Q_ROWS + A_KV_DIM) // A_PROJ_ROWS):
            lo = c * A_PROJ_ROWS
            part = _dot_nt(wt_ref[lo:lo + A_PROJ_ROWS, :], h) + bt_ref[lo:lo + A_PROJ_ROWS, tok]
            if lo < A_Q_ROWS:
                for j in range(A_PROJ_ROWS // A_HEAD_DIM):
                    base = j * A_HEAD_DIM
                    x1 = part[base:base + half]
                    x2 = part[base + half:base + 2 * half]
                    rot = jnp.concatenate([x1 * cos_t - x2 * sin_t, x2 * cos_t + x1 * sin_t,
                                           part[base + 2 * half:base + A_HEAD_DIM]], axis=0)
                    qt_ref[lo + base:lo + base + A_HEAD_DIM, tok] = (rot * q_scale).astype(BF16)
            else:
                vt_ref[:, tok] = part.astype(BF16)


def _attn_proj(x, nw, wt, bt, wk, bk, cos_a, sin_a, cos_at, sin_at, *, tm):
    t = x.shape[0]
    half = A_ROPE_DIM // 2
    rows = A_Q_ROWS + A_KV_DIM
    return pl.pallas_call(
        _attn_proj_kernel,
        grid=(t // tm,),
        in_specs=[_rows(tm, D_MODEL), _resident((1, D_MODEL)), _resident((rows, D_MODEL)), _resident((rows, tm)),
                  _resident((D_MODEL, A_KV_DIM)), _resident((1, A_KV_DIM)),
                  _rows(tm, LANES), _rows(tm, LANES), _cols(half, tm), _cols(half, tm)],
        out_specs=[_cols(A_Q_ROWS, tm), _rows(tm, A_KV_DIM), _cols(A_KV_DIM, tm)],
        out_shape=[jax.ShapeDtypeStruct((A_Q_ROWS, t), BF16),
                   jax.ShapeDtypeStruct((t, A_KV_DIM), BF16),
                   jax.ShapeDtypeStruct((A_KV_DIM, t), BF16)],
        compiler_params=_params(1),
        name="attn_proj",
    )(x, nw, wt, bt, wk, bk, cos_a, sin_a, cos_at, sin_at)


def _attn_core_kernel(sink_ref, qt_ref, kp_ref, kc_ref, vtp_ref, vtc_ref, ot_ref):
    L = A_CHUNK
    blk = pl.program_id(1)
    key = lax.broadcasted_iota(jnp.int32, (L, L), 0)
    qry = lax.broadcasted_iota(jnp.int32, (L, L), 1)
    from_prev = key > qry
    prev_bias = jnp.where(blk > 0, 0.0, -jnp.inf).astype(F32)
    zero_q = jnp.zeros((A_HEAD_DIM, A_GROUP * L), BF16)
    scores, probs_of = {}, {}

    def score(g):
        col = g // 2
        kk = jnp.concatenate([kp_ref[:, col * LANES:(col + 1) * LANES],
                              kc_ref[:, col * LANES:(col + 1) * LANES]], axis=0)
        q_cat = jnp.concatenate([qt_ref[(A_GROUP * g + j) * A_HEAD_DIM:(A_GROUP * g + j + 1) * A_HEAD_DIM, :]
                                 for j in range(A_GROUP)], axis=1)
        q_ext = jnp.concatenate([q_cat, zero_q] if g % 2 == 0 else [zero_q, q_cat], axis=0)
        scores[g] = _dot(kk, q_ext)

    def softmax(g):
        s = scores.pop(g)
        probs = []
        for j in range(A_GROUP):
            sink = sink_ref[A_GROUP * g + j]
            s_prev = s[:L, j * L:(j + 1) * L]
            s_own = s[L:, j * L:(j + 1) * L]
            sc = jnp.where(from_prev, s_prev + prev_bias, s_own)
            mx = jnp.maximum(jnp.max(sc, axis=0, keepdims=True), sink)
            p = jnp.exp(sc - mx)
            denom = jnp.sum(p, axis=0, keepdims=True) + jnp.exp(sink - mx)
            pn = p * (1.0 / denom)
            zero = jnp.zeros_like(pn)
            probs.append(jnp.concatenate([jnp.where(from_prev, pn, zero), jnp.where(from_prev, zero, pn)],
                                         axis=0).astype(BF16))
        probs_of[g] = jnp.concatenate(probs, axis=1)

    def weighted_values(g):
        vt = jnp.concatenate([vtp_ref[g * A_HEAD_DIM:(g + 1) * A_HEAD_DIM, :],
                              vtc_ref[g * A_HEAD_DIM:(g + 1) * A_HEAD_DIM, :]], axis=1)
        out = _dot(vt, probs_of.pop(g))
        for j in range(A_GROUP):
            head = A_GROUP * g + j
            ot_ref[head * A_HEAD_DIM:(head + 1) * A_HEAD_DIM, :] = out[:, j * L:(j + 1) * L].astype(BF16)

    score(0)
    for g in range(A_KV_HEADS + 1):
        if g + 1 < A_KV_HEADS:
            score(g + 1)
        if g < A_KV_HEADS:
            softmax(g)
        if g >= 1:
            weighted_values(g - 1)


def _attn_core(qt, k, vt, sinks, *, batch, seq):
    L = A_CHUNK
    nb = seq // L
    cur = lambda b, i: (b * nb + i, 0)
    prev = lambda b, i: (b * nb + jnp.maximum(i - 1, 0), 0)
    cur_t = lambda b, i: (0, b * nb + i)
    prev_t = lambda b, i: (0, b * nb + jnp.maximum(i - 1, 0))
    return pl.pallas_call(
        _attn_core_kernel,
        grid=(batch, nb),
        in_specs=[pl.BlockSpec(memory_space=pltpu.SMEM),
                  pl.BlockSpec((A_Q_ROWS, L), cur_t),
                  pl.BlockSpec((L, A_KV_DIM), prev), pl.BlockSpec((L, A_KV_DIM), cur),
                  pl.BlockSpec((A_KV_DIM, L), prev_t), pl.BlockSpec((A_KV_DIM, L), cur_t)],
        out_specs=pl.BlockSpec((A_Q_ROWS, L), cur_t),
        out_shape=jax.ShapeDtypeStruct((A_Q_ROWS, batch * seq), BF16),
        compiler_params=_params(2),
        name="attn_core",
    )(sinks, qt, k, k, vt, vt)


M_QK_COLS = M_HEADS * M_QK_DIM
M_V_COLS = M_HEADS * M_V_DIM
M_T_ROWS = M_QK_COLS + 2 * M_V_COLS
M_PROJ_ROWS = 512
M_STATE_ROWS = M_V_DIM + BF16_ROWS


def _mlstm_proj_kernel(x_ref, nw_ref, wt_ref, wk_ref, wg_ref, bg_ref,
                       qt_ref, k_ref, vt_ref, ot_ref, src_ref, gt_ref):
    sub = x_ref.shape[0] // ROW_SPLIT
    H = M_HEADS
    lane = lax.broadcasted_iota(jnp.int32, (sub, LANES), 1)
    time = lax.broadcasted_iota(jnp.int32, (sub, LANES), 0) & (M_CHUNK - 1)
    for split in range(ROW_SPLIT):
        tok = slice(split * sub, (split + 1) * sub)
        h = _rms(x_ref[tok, :], nw_ref[...]).astype(BF16)
        k_ref[tok, :] = _dot(h, wk_ref[...]).astype(BF16)
        for c in range(M_T_ROWS // M_PROJ_ROWS):
            lo = c * M_PROJ_ROWS
            part = _dot_nt(wt_ref[lo:lo + M_PROJ_ROWS, :], h)
            if lo < M_QK_COLS:
                qt_ref[:, tok] = (part * (M_QK_DIM ** -0.5)).astype(BF16)
            elif lo < M_QK_COLS + M_V_COLS:
                vt_ref[lo - M_QK_COLS:lo - M_QK_COLS + M_PROJ_ROWS, tok] = part.astype(BF16)
            else:
                o_lo = lo - M_QK_COLS - M_V_COLS
                ot_ref[o_lo:o_lo + M_PROJ_ROWS, tok] = jax.nn.sigmoid(part)

        gates = _dot(h, wg_ref[...]) + bg_ref[...]
        gates = M_GATE_CAP * jnp.tanh(gates / M_GATE_CAP)
        b = jnp.where((lane >= H) & (lane < 2 * H), jax.nn.log_sigmoid(gates), 0.0)
        shift = 1
        while shift < M_CHUNK:
            b = b + jnp.where(time >= shift, pltpu.roll(b, shift, 0), 0.0)
            shift *= 2
        src_ref[tok, :] = gates - pltpu.roll(b, LANES - H, 1)
        gt_ref[:, tok] = jnp.where(lane < H, gates, b).T[:2 * H]


def _mlstm_proj(x, nw, wt, wk, wg, bg, *, tm):
    t = x.shape[0]
    assert (tm // ROW_SPLIT) % M_CHUNK == 0 and M_CHUNK & (M_CHUNK - 1) == 0
    return pl.pallas_call(
        _mlstm_proj_kernel,
        grid=(t // tm,),
        in_specs=[_rows(tm, D_MODEL), _resident((1, D_MODEL)), _resident((M_T_ROWS, D_MODEL)),
                  _resident((D_MODEL, M_QK_COLS)), _resident((D_MODEL, LANES)), _resident((1, LANES))],
        out_specs=[_cols(M_QK_COLS, tm), _rows(tm, M_QK_COLS), _cols(M_V_COLS, tm),
                   _cols(M_V_COLS, tm), _rows(tm, LANES), _cols(2 * M_HEADS, tm)],
        out_shape=[jax.ShapeDtypeStruct((M_QK_COLS, t), BF16),
                   jax.ShapeDtypeStruct((t, M_QK_COLS), BF16),
                   jax.ShapeDtypeStruct((M_V_COLS, t), BF16),
                   jax.ShapeDtypeStruct((M_V_COLS, t), F32),
                   jax.ShapeDtypeStruct((t, LANES), F32),
                   jax.ShapeDtypeStruct((2 * M_HEADS, t), F32)],
        compiler_params=_params(1),
        name="mlstm_proj",
    )(x, nw, wt, wk, wg, bg)


def _mlstm_core_kernel(qt_ref, k_ref, vt_ref, ot_ref, src_ref, gt_ref, nwt_ref, yt_ref, c_sc, m_sc):
    L = M_CHUNK
    H = M_HEADS

    @pl.when(pl.program_id(1) == 0)
    def _():
        c_sc[...] = jnp.zeros_like(c_sc)
        m_sc[...] = jnp.zeros_like(m_sc)

    src_i = lax.broadcasted_iota(jnp.int32, (L, L), 0)
    tgt_i = lax.broadcasted_iota(jnp.int32, (L, L), 1)
    causal = src_i <= tgt_i
    ones_rows = (lax.broadcasted_iota(jnp.int32, (BF16_ROWS, L), 0) == 0).astype(BF16)
    zero_q = jnp.zeros((M_QK_DIM, L), BF16)
    low = lax.broadcasted_iota(jnp.int32, (L, LANES), 1) < M_QK_DIM

    st = [dict() for _ in range(H)]

    def score(h):
        d = st[h]
        pair = h // 2
        k_pair = k_ref[:, pair * LANES:(pair + 1) * LANES]
        d["k"] = jnp.where(low if h % 2 == 0 else jnp.logical_not(low), k_pair, jnp.zeros_like(k_pair))
        qt_h = qt_ref[h * M_QK_DIM:(h + 1) * M_QK_DIM, :]
        d["qt"] = jnp.concatenate([qt_h, zero_q] if h % 2 == 0 else [zero_q, qt_h], axis=0)
        d["v"] = jnp.concatenate([vt_ref[h * M_V_DIM:(h + 1) * M_V_DIM, :], ones_rows], axis=0)
        d["state"] = c_sc[h]
        d["s"] = _dot(d["k"], d["qt"])
        d["carry"] = _dot(d["state"].astype(BF16), d["qt"])

    def gate(h):
        d = st[h]
        src_col = src_ref[:, h:h + 1]
        d["b"] = gt_ref[H + h:H + h + 1, :]
        d["m_prev"] = m_sc[h][:, :1]
        dmat = jnp.where(causal, src_col + d["b"], -jnp.inf)
        inter = d["b"] + d["m_prev"]
        d["m_row"] = jnp.maximum(inter, jnp.max(dmat, axis=0, keepdims=True))
        d["w_intra"] = jnp.exp(dmat - d["m_row"])
        d["w_inter"] = jnp.exp(inter - d["m_row"])

    def emit(h):
        d = st[h]
        tot = _dot(d["v"], (d["s"] * d["w_intra"]).astype(BF16)) + d["w_inter"] * d["carry"]
        num = tot[:M_V_DIM]
        nq = tot[M_V_DIM:M_V_DIM + 1]
        h_out = num * (1.0 / jnp.maximum(jnp.abs(nq), jnp.exp(-d["m_row"])))
        h_out = h_out * lax.rsqrt(jnp.mean(h_out * h_out, axis=0, keepdims=True) + EPS)
        rows = slice(h * M_V_DIM, (h + 1) * M_V_DIM)
        yt_ref[rows, :] = (ot_ref[rows, :] * (h_out * nwt_ref[rows, :])).astype(BF16)

    def update(h):
        d = st[h]
        b_last = d["b"][:, L - 1:L]
        dec = b_last + (gt_ref[h:h + 1, :] - d["b"])
        m_new = jnp.maximum(b_last + d["m_prev"], jnp.max(dec, axis=-1, keepdims=True))
        wk = jnp.exp(dec - m_new)
        keep = jnp.exp(b_last + d["m_prev"] - m_new)
        v_scaled = (d["v"].astype(F32) * wk).astype(BF16)
        c_sc[h] = keep * d["state"] + _dot(v_scaled, d["k"])
        m_sc[h] = jnp.broadcast_to(m_new, (1, LANES))
        d.clear()

    score(0)
    for h in range(H + 1):
        if h + 1 < H:
            score(h + 1)
        if h < H:
            gate(h)
            emit(h)
        if h >= 1:
            update(h - 1)


def _mlstm_core(qt, k, vt, ot, src, gt, nwt, *, batch, seq):
    L = M_CHUNK
    nb = seq // L
    cur = lambda b, i: (b * nb + i, 0)
    cur_t = lambda b, i: (0, b * nb + i)
    return pl.pallas_call(
        _mlstm_core_kernel,
        grid=(batch, nb),
        in_specs=[pl.BlockSpec((M_QK_COLS, L), cur_t), pl.BlockSpec((L, M_QK_COLS), cur),
                  pl.BlockSpec((M_V_COLS, L), cur_t), pl.BlockSpec((M_V_COLS, L), cur_t),
                  pl.BlockSpec((L, LANES), cur), pl.BlockSpec((2 * M_HEADS, L), cur_t), _resident((M_V_COLS, L))],
        out_specs=pl.BlockSpec((M_V_COLS, L), cur_t),
        out_shape=jax.ShapeDtypeStruct((M_V_COLS, batch * seq), BF16),
        scratch_shapes=[pltpu.VMEM((M_HEADS, M_STATE_ROWS, LANES), F32),
                        pltpu.VMEM((M_HEADS, 1, LANES), F32)],
        compiler_params=_params(2),
        name="mlstm_core",
    )(qt, k, vt, ot, src, gt, nwt)


R_QK_COLS = R_HEADS * R_QK_DIM
R_V_COLS = R_HEADS * R_V_DIM
R_PROJ_COLS = R_QK_COLS + 2 * R_V_COLS
R_PROJ_CHUNK = 512


def _ret_proj_kernel(x_ref, nw_ref, w_ref, wkt_ref, cos_ref, sin_ref, cost_ref, sint_ref, qdec_ref, kdec_ref,
                     q_ref, kt_ref, v_ref, g_ref):
    sub = x_ref.shape[0] // ROW_SPLIT
    half = R_QK_DIM // 2
    for split in range(ROW_SPLIT):
        tok = slice(split * sub, (split + 1) * sub)
        h = _rms(x_ref[tok, :], nw_ref[...]).astype(BF16)
        cos = cos_ref[tok, :]
        sin = sin_ref[tok, :]
        for c in range(R_QK_COLS // R_PROJ_CHUNK):
            lo = c * R_PROJ_CHUNK
            t = _dot(h, w_ref[:, lo:lo + R_PROJ_CHUNK])
            for hh in range(R_PROJ_CHUNK // R_QK_DIM):
                head = lo // R_QK_DIM + hh
                dec = qdec_ref[head, tok, :]
                x1 = t[:, hh * R_QK_DIM:hh * R_QK_DIM + half]
                x2 = t[:, hh * R_QK_DIM + half:(hh + 1) * R_QK_DIM]
                base = head * R_QK_DIM
                q_ref[tok, base:base + half] = ((x1 * cos - x2 * sin) * dec).astype(BF16)
                q_ref[tok, base + half:base + R_QK_DIM] = ((x2 * cos + x1 * sin) * dec).astype(BF16)
        for c in range(2 * R_V_COLS // R_PROJ_CHUNK):
            lo = c * R_PROJ_CHUNK
            t = _dot(h, w_ref[:, R_QK_COLS + lo:R_QK_COLS + lo + R_PROJ_CHUNK])
            if lo < R_V_COLS:
                v_ref[tok, lo:lo + R_PROJ_CHUNK] = t.astype(BF16)
            else:
                g_lo = lo - R_V_COLS
                g_ref[tok, g_lo:g_lo + R_PROJ_CHUNK] = (t * jax.nn.sigmoid(t)).astype(BF16)
        cos_t = cost_ref[:, tok]
        sin_t = sint_ref[:, tok]
        for head in range(R_HEADS):
            base = head * R_QK_DIM
            kt = _dot_nt(wkt_ref[base:base + R_QK_DIM, :], h)
            dec = kdec_ref[head:head + 1, tok]
            x1 = kt[:half]
            x2 = kt[half:]
            kt_ref[base:base + half, tok] = ((x1 * cos_t - x2 * sin_t) * dec).astype(BF16)
            kt_ref[base + half:base + R_QK_DIM, tok] = ((x2 * cos_t + x1 * sin_t) * dec).astype(BF16)


def _ret_proj(x, nw, w, wkt, cos_r, sin_r, cos_rt, sin_rt, qdec, kdec, *, tm):
    t = x.shape[0]
    return pl.pallas_call(
        _ret_proj_kernel,
        grid=(t // tm,),
        in_specs=[_rows(tm, D_MODEL), _resident((1, D_MODEL)), _resident((D_MODEL, R_PROJ_COLS)),
                  _resident((R_QK_COLS, D_MODEL)),
                  _rows(tm, LANES), _rows(tm, LANES), _cols(LANES, tm), _cols(LANES, tm),
                  _resident((R_HEADS, tm, LANES)), _resident((SUBLANES, tm))],
        out_specs=[_rows(tm, R_QK_COLS), _cols(R_QK_COLS, tm), _rows(tm, R_V_COLS), _rows(tm, R_V_COLS)],
        out_shape=[jax.ShapeDtypeStruct((t, R_QK_COLS), BF16),
                   jax.ShapeDtypeStruct((R_QK_COLS, t), BF16),
                   jax.ShapeDtypeStruct((t, R_V_COLS), BF16),
                   jax.ShapeDtypeStruct((t, R_V_COLS), BF16)],
        compiler_params=_params(1),
        name="ret_proj",
    )(x, nw, w, wkt, cos_r, sin_r, cos_rt, sin_rt, qdec, kdec)


def _ret_core_kernel(cdec_ref, q_ref, kt_ref, v_ref, g_ref, nw_ref, y_ref, s_sc):
    L = R_CHUNK

    @pl.when(pl.program_id(1) == 0)
    def _():
        s_sc[...] = jnp.zeros_like(s_sc)

    row = lax.broadcasted_iota(jnp.int32, (L, L), 0)
    col = lax.broadcasted_iota(jnp.int32, (L, L), 1)
    causal = row >= col
    nw = nw_ref[...]
    st = [dict() for _ in range(R_HEADS)]

    def score(h):
        d = st[h]
        d["q"] = q_ref[:, h * R_QK_DIM:(h + 1) * R_QK_DIM]
        d["kt"] = kt_ref[h * R_QK_DIM:(h + 1) * R_QK_DIM, :]
        d["v"] = v_ref[:, h * R_V_DIM:(h + 1) * R_V_DIM]
        d["state"] = s_sc[h]
        d["scores"] = _dot(d["q"], d["kt"])
        d["carry"] = _dot(d["q"], d["state"].astype(BF16))

    def emit(h):
        d = st[h]
        scores = jnp.where(causal, d["scores"], 0.0)
        y = _dot(scores.astype(BF16), d["v"]) + d["carry"]
        s_sc[h] = cdec_ref[h] * (d["state"] + _dot(d["kt"], d["v"]))
        d.clear()
        mu = jnp.mean(y, axis=-1, keepdims=True)
        yc = y - mu
        yn = yc * lax.rsqrt(jnp.mean(yc * yc, axis=-1, keepdims=True) + EPS)
        gate = g_ref[:, h * R_V_DIM:(h + 1) * R_V_DIM]
        out = gate * (yn * nw[:, h * R_V_DIM:(h + 1) * R_V_DIM])
        y_ref[:, h * R_V_DIM:(h + 1) * R_V_DIM] = out.astype(BF16)

    score(0)
    for h in range(R_HEADS):
        if h + 1 < R_HEADS:
            score(h + 1)
        emit(h)


def _ret_core(q, kt, v, g, nw, cdec, *, batch, seq):
    L = R_CHUNK
    nb = seq // L
    cur = lambda b, i: (b * nb + i, 0)
    cur_t = lambda b, i: (0, b * nb + i)
    return pl.pallas_call(
        _ret_core_kernel,
        grid=(batch, nb),
        in_specs=[pl.BlockSpec(memory_space=pltpu.SMEM),
                  pl.BlockSpec((L, R_QK_COLS), cur), pl.BlockSpec((R_QK_COLS, L), cur_t),
                  pl.BlockSpec((L, R_V_COLS), cur), pl.BlockSpec((L, R_V_COLS), cur),
                  _resident((1, R_V_COLS))],
        out_specs=pl.BlockSpec((L, R_V_COLS), cur),
        out_shape=jax.ShapeDtypeStruct((batch * seq, R_V_COLS), BF16),
        scratch_shapes=[pltpu.VMEM((R_HEADS, R_QK_DIM, R_V_DIM), F32)],
        compiler_params=_params(2),
        name="ret_core",
    )(cdec, q, kt, v, g, nw)


def _rope_constants(tm):
    inv_a = 1.0 / (ROPE_THETA ** (jnp.arange(0, A_ROPE_DIM, 2, dtype=F32) / A_ROPE_DIM))
    inv_r = 1.0 / (R_THETA ** jnp.linspace(0.0, 1.0, R_QK_DIM // 2, dtype=F32))
    return (jnp.broadcast_to(inv_a[:, None], (A_ROPE_DIM // 2, tm)),
            jnp.broadcast_to(inv_r[:, None], (R_QK_DIM // 2, tm)))


def _retention_constants(tm):
    L = R_CHUNK
    log_gamma = jnp.log(1.0 - 2.0 ** (-5.0 - jnp.arange(R_HEADS, dtype=F32)))
    j = jnp.asarray(np.arange(tm) % L, F32)
    q_decay = jnp.exp((j + 1.0) * log_gamma[:, None])
    k_decay = jnp.exp(-(j + 1.0) * log_gamma[:, None]) * (R_QK_DIM ** -0.5)
    chunk_decay = jnp.exp(L * log_gamma)
    qdec = jnp.broadcast_to(q_decay[:, :, None], (R_HEADS, tm, LANES))
    kdec = jnp.pad(k_decay, ((0, SUBLANES - R_HEADS), (0, 0)))
    return qdec, kdec, chunk_decay


def _attn_weight_layout(w_qkv, b_qkv, tm):
    nq = A_Q_ROWS
    qv = np.concatenate([np.arange(nq), np.arange(nq + A_KV_DIM, nq + 2 * A_KV_DIM)])
    wt = w_qkv[:, qv].T.astype(BF16)
    bt = jnp.broadcast_to(b_qkv[qv][:, None], (nq + A_KV_DIM, tm))
    wk = w_qkv[:, nq:nq + A_KV_DIM].astype(BF16)
    bk = b_qkv[nq:nq + A_KV_DIM].reshape(1, A_KV_DIM)
    return wt, bt, wk, bk


def _ret_weight_layout(w_in):
    head = np.concatenate([np.arange(0, R_QK_DIM, 2), np.arange(1, R_QK_DIM, 2)])
    perm = np.concatenate([h * R_QK_DIM + head for h in range(R_HEADS)])
    w = jnp.concatenate([w_in[:, perm], w_in[:, 2 * R_QK_COLS:]], axis=-1).astype(BF16)
    wkt = w_in[:, R_QK_COLS + perm].T.astype(BF16)
    return w, wkt


def _mlstm_weight_layout(w_in, b_if):
    pad = LANES - 2 * M_HEADS
    gate_lo = 2 * M_QK_COLS + 2 * M_V_COLS
    wt = jnp.concatenate([w_in[:, :M_QK_COLS], w_in[:, 2 * M_QK_COLS:gate_lo]], axis=-1).T.astype(BF16)
    wk = w_in[:, M_QK_COLS:2 * M_QK_COLS].astype(BF16)
    wg = jnp.pad(w_in[:, gate_lo:], ((0, 0), (0, pad))).astype(BF16)
    return wt, wk, wg, jnp.pad(b_if, (0, pad)).reshape(1, LANES)


def kernel(x, positions, norm_w, final_norm_w, ffn_w_in, ffn_w_out, attn_w_qkv, attn_b_qkv, attn_w_o, attn_b_o,
           attn_sinks, mlstm_w_in, mlstm_b_if, mlstm_norm_w, mlstm_w_out, ret_w_in, ret_norm_w, ret_w_out):
    batch, seq, d = x.shape
    t = batch * seq
    tm = min(ROW_TILE, t)
    assert d == D_MODEL and t % tm == 0 and (tm // ROW_SPLIT) % R_CHUNK == 0
    assert seq % R_CHUNK == 0 and seq % M_CHUNK == 0 and seq % A_CHUNK == 0

    xf = x.reshape(t, d)
    inv_a, inv_r = _rope_constants(tm)
    cos_a, sin_a, cos_r, sin_r, cos_at, sin_at, cos_rt, sin_rt = _rope_tables(
        positions.astype(F32).reshape(1, t), inv_a, inv_r, tm)
    qdec, kdec, cdec = _retention_constants(tm)
    zero_bias = jnp.zeros((1, D_MODEL), F32)
    norm_w4 = norm_w.reshape(DEPTH, 3, 1, d)
    w_in_bf = ffn_w_in.astype(BF16)
    w_out_bf = ffn_w_out.astype(BF16)

    for i in range(DEPTH):
        kind, slot = i % N_MIXERS, i // N_MIXERS
        xf = _ffn(xf, norm_w4, w_in_bf, w_out_bf, i, 0, tm=tm)
        nw = norm_w[i, 1].reshape(1, d)
        if kind == 0:
            wt, bt, wk, bk = _attn_weight_layout(attn_w_qkv[slot], attn_b_qkv[slot], tm)
            qt, k, vt = _attn_proj(xf, nw, wt, bt, wk, bk, cos_a, sin_a, cos_at, sin_at, tm=tm)
            yt = _attn_core(qt, k, vt, attn_sinks[slot], batch=batch, seq=seq)
            mix = (yt, attn_w_o[slot].astype(BF16), attn_b_o[slot].reshape(1, d), True)
        elif kind == 1:
            wt, wk, wg, bg = _mlstm_weight_layout(mlstm_w_in[slot], mlstm_b_if[slot])
            qt, k, vt, ot, src, gt = _mlstm_proj(xf, nw, wt, wk, wg, bg, tm=tm)
            nwt = jnp.broadcast_to(mlstm_norm_w[slot][:, None], (M_V_COLS, M_CHUNK))
            yt = _mlstm_core(qt, k, vt, ot, src, gt, nwt, batch=batch, seq=seq)
            mix = (yt, mlstm_w_out[slot].astype(BF16), zero_bias, True)
        else:
            w, wkt = _ret_weight_layout(ret_w_in[slot])
            q, kt, v, g = _ret_proj(xf, nw, w, wkt, cos_r, sin_r, cos_rt, sin_rt, qdec, kdec, tm=tm)
            y = _ret_core(q, kt, v, g, ret_norm_w[slot].reshape(1, R_V_COLS), cdec, batch=batch, seq=seq)
            mix = (y, ret_w_out[slot].astype(BF16), zero_bias, False)
        last = i == DEPTH - 1
        xf = _ffn(xf, norm_w4, w_in_bf, w_out_bf, i, 1, tm=tm, mix=mix,
                  final_w=final_norm_w.reshape(1, d) if last else None)
    return xf.reshape(batch, seq, d)
```

```python
import functools

import jax
import jax.numpy as jnp
import numpy as np
from jax import lax
from jax.experimental import pallas as pl
from jax.experimental.pallas import tpu as pltpu

F32 = jnp.float32
BF16 = jnp.bfloat16

D_MODEL = 1024
DEPTH = 4
N_MIXERS = 3
EPS = 1e-6
D_FF = 2816
A_HEADS = 16
A_KV_HEADS = 4
A_GROUP = A_HEADS // A_KV_HEADS
A_HEAD_DIM = 64
A_WINDOW = 128
A_ROPE_DIM = 16
ROPE_THETA = 500000.0
M_HEADS = 8
M_QK_DIM = 64
M_V_DIM = 128
M_GATE_CAP = 15.0
R_HEADS = 4
R_QK_DIM = 256
R_V_DIM = 512
R_THETA = 10000.0

LANES = 128
SUBLANES = 8
BF16_ROWS = 16
VMEM_LIMIT = 56 * 1024 * 1024

ROW_TILE = 1024
ROW_SPLIT = 2
FF_CHUNK = 256
A_CHUNK = A_WINDOW
A_BLOCKS_PER_STEP = 4
M_CHUNK = 128
M_CHUNKS_PER_STEP = 4
R_CHUNK = 256
R_CHUNKS_PER_STEP = 2


def _params(n_axes):
    return pltpu.CompilerParams(
        dimension_semantics=("arbitrary",) * n_axes,
        vmem_limit_bytes=VMEM_LIMIT,
    )


def _resident(shape):
    nd = len(shape)
    return pl.BlockSpec(shape, lambda *_: (0,) * nd, pipeline_mode=pl.Buffered(1))


def _resident_slice(lead, shape):
    nd = len(shape)
    return pl.BlockSpec((None,) * len(lead) + tuple(shape), lambda *_: tuple(lead) + (0,) * nd,
                        pipeline_mode=pl.Buffered(1))


def _rows(tm, width):
    return pl.BlockSpec((tm, width), lambda i: (i, 0))


def _cols(height, tm):
    return pl.BlockSpec((height, tm), lambda i: (0, i))


def _rms(x, w):
    ms = jnp.mean(x * x, axis=-1, keepdims=True)
    return x * lax.rsqrt(ms + EPS) * w


def _dot(a, b):
    return jnp.dot(a, b, preferred_element_type=F32)


def _dot_nt(a, b):
    return lax.dot_general(a, b, (((1,), (1,)), ((), ())), preferred_element_type=F32)


def _dot_tn(a, b):
    return lax.dot_general(a, b, (((0,), (0,)), ((), ())), preferred_element_type=F32)


def _tables_kernel(pos_ref, inv_a_ref, inv_r_ref, cos_a, sin_a, cos_r, sin_r, cos_at, sin_at, cos_rt, sin_rt):
    tm = pos_ref.shape[1]
    pos = pos_ref[...]
    ang_a = inv_a_ref[...] * pos
    ca = jnp.cos(ang_a)
    sa = jnp.sin(ang_a)
    cos_at[...] = ca
    sin_at[...] = sa
    rest = A_HEAD_DIM - A_ROPE_DIM
    cos_head = jnp.concatenate([ca, ca, jnp.ones((rest, tm), F32)], axis=0)
    sin_head = jnp.concatenate([-sa, sa, jnp.zeros((rest, tm), F32)], axis=0)
    cos_a[...] = jnp.concatenate([cos_head] * (LANES // A_HEAD_DIM), axis=0).T
    sin_a[...] = jnp.concatenate([sin_head] * (LANES // A_HEAD_DIM), axis=0).T
    ang_r = inv_r_ref[...] * pos
    cr = jnp.cos(ang_r)
    sr = jnp.sin(ang_r)
    cos_rt[...] = cr
    sin_rt[...] = sr
    cos_r[...] = cr.T
    sin_r[...] = sr.T


def _rope_tables(pos_row, inv_a, inv_r, tm):
    t = pos_row.shape[1]
    half = A_ROPE_DIM // 2
    tok = jax.ShapeDtypeStruct((t, LANES), F32)
    return pl.pallas_call(
        _tables_kernel,
        grid=(t // tm,),
        in_specs=[_cols(1, tm), _resident((half, tm)), _resident((LANES, tm))],
        out_specs=[_rows(tm, LANES)] * 4 + [_cols(half, tm), _cols(half, tm), _cols(LANES, tm), _cols(LANES, tm)],
        out_shape=[tok] * 4 + [jax.ShapeDtypeStruct((half, t), F32)] * 2 + [jax.ShapeDtypeStruct((LANES, t), F32)] * 2,
        compiler_params=_params(1),
        name="rope_tables",
    )(pos_row, inv_a, inv_r)


def _ffn_kernel(*refs, mix_in, mix_feature_major, final_norm):
    refs = list(refs)
    x_ref = refs.pop(0)
    if mix_in:
        y_ref, wo_ref, bo_ref = refs.pop(0), refs.pop(0), refs.pop(0)
    nw_ref, win_ref, wout_ref = refs.pop(0), refs.pop(0), refs.pop(0)
    fw_ref = refs.pop(0) if final_norm else None
    (o_ref,) = refs

    tm = x_ref.shape[0]
    sub = tm // ROW_SPLIT
    for part in range(ROW_SPLIT):
        rows = slice(part * sub, (part + 1) * sub)
        x = x_ref[rows, :]
        if mix_in:
            if mix_feature_major:
                mixed = _dot_tn(y_ref[:, rows], wo_ref[...])
            else:
                mixed = _dot(y_ref[rows, :], wo_ref[...])
            x = x + (mixed + bo_ref[...])
        h = _rms(x, nw_ref[...]).astype(BF16)
        acc = jnp.zeros_like(x)
        for c in range(D_FF // FF_CHUNK):
            lo = c * FF_CHUNK
            g = _dot(h, win_ref[:, lo:lo + FF_CHUNK])
            u = _dot(h, win_ref[:, D_FF + lo:D_FF + lo + FF_CHUNK])
            a = (g * jax.nn.sigmoid(g) * u).astype(BF16)
            acc = acc + _dot(a, wout_ref[lo:lo + FF_CHUNK, :])
        y = x + 0.5 * acc
        if final_norm:
            y = _rms(y, fw_ref[...])
        o_ref[rows, :] = y


def _ffn(x, norm_w, w_in, w_out, layer, which, *, tm, mix=None, final_w=None):
    t = x.shape[0]
    in_specs = [_rows(tm, D_MODEL)]
    args = [x]
    feature_major = False
    if mix is not None:
        y, w_o, b_o, feature_major = mix
        kdim = w_o.shape[0]
        assert y.shape == ((kdim, t) if feature_major else (t, kdim))
        in_specs += [_cols(kdim, tm) if feature_major else _rows(tm, kdim),
                     _resident((kdim, D_MODEL)), _resident((1, D_MODEL))]
        args += [y, w_o, b_o]
    in_specs += [_resident_slice((layer, 2 * which), (1, D_MODEL)),
                 _resident_slice((layer, which), (D_MODEL, 2 * D_FF)),
                 _resident_slice((layer, which), (D_FF, D_MODEL))]
    args += [norm_w, w_in, w_out]
    if final_w is not None:
        in_specs.append(_resident((1, D_MODEL)))
        args.append(final_w)
    return pl.pallas_call(
        functools.partial(_ffn_kernel, mix_in=mix is not None, mix_feature_major=feature_major,
                          final_norm=final_w is not None),
        grid=(t // tm,),
        in_specs=in_specs,
        out_specs=_rows(tm, D_MODEL),
        out_shape=jax.ShapeDtypeStruct((t, D_MODEL), F32),
        compiler_params=_params(1),
        name="ffn_mix" if mix is not None else "ffn",
    )(*args)


A_Q_ROWS = A_HEADS * A_HEAD_DIM
A_KV_DIM = A_KV_HEADS * A_HEAD_DIM
A_PROJ_ROWS = 256


def _attn_proj_kernel(x_ref, nw_ref, wt_ref, bt_ref, wk_ref, bk_ref, cos_ref, sin_ref, cost_ref, sint_ref,
                      qt_ref, k_ref, vt_ref):
    sub = x_ref.shape[0] // ROW_SPLIT
    half = A_ROPE_DIM // 2
    lane = lax.broadcasted_iota(jnp.int32, (sub, LANES), 1) & (A_HEAD_DIM - 1)
    upper = (lane >= half) & (lane < A_ROPE_DIM)
    q_scale = A_HEAD_DIM ** -0.5
    for split in range(ROW_SPLIT):
        tok = slice(split * sub, (split + 1) * sub)
        h = _rms(x_ref[tok, :], nw_ref[...]).astype(BF16)

        cos = cos_ref[tok, :]
        sin = sin_ref[tok, :]
        k = _dot(h, wk_ref[...]) + bk_ref[...]
        for j in range(A_KV_DIM // LANES):
            piece = k[:, j * LANES:(j + 1) * LANES]
            partner = jnp.where(upper, pltpu.roll(piece, half, 1), pltpu.roll(piece, LANES - half, 1))
            k_ref[tok, j * LANES:(j + 1) * LANES] = (piece * cos + partner * sin).astype(BF16)

        cos_t = cost_ref[:, tok]
        sin_t = sint_ref[:, tok]
        for c in range((A_Q_ROWS + A_KV_DIM) // A_PROJ_ROWS):
            lo = c * A_PROJ_ROWS
            part = _dot_nt(wt_ref[lo:lo + A_PROJ_ROWS, :], h) + bt_ref[lo:lo + A_PROJ_ROWS, tok]
            if lo < A_Q_ROWS:
                for j in range(A_PROJ_ROWS // A_HEAD_DIM):
                    base = j * A_HEAD_DIM
                    x1 = part[base:base + half]
                    x2 = part[base + half:base + 2 * half]
                    rot = jnp.concatenate([x1 * cos_t - x2 * sin_t, x2 * cos_t + x1 * sin_t,
                                           part[base + 2 * half:base + A_HEAD_DIM]], axis=0)
                    qt_ref[lo + base:lo + base + A_HEAD_DIM, tok] = (rot * q_scale).astype(BF16)
            else:
                vt_ref[:, tok] = part.astype(BF16)


def _attn_proj(x, nw, wt, bt, wk, bk, cos_a, sin_a, cos_at, sin_at, *, tm):
    t = x.shape[0]
    half = A_ROPE_DIM // 2
    rows = A_Q_ROWS + A_KV_DIM
    return pl.pallas_call(
        _attn_proj_kernel,
        grid=(t // tm,),
        in_specs=[_rows(tm, D_MODEL), _resident((1, D_MODEL)), _resident((rows, D_MODEL)), _resident((rows, tm)),
                  _resident((D_MODEL, A_KV_DIM)), _resident((1, A_KV_DIM)),
                  _rows(tm, LANES), _rows(tm, LANES), _cols(half, tm), _cols(half, tm)],
        out_specs=[_cols(A_Q_ROWS, tm), _rows(tm, A_KV_DIM), _cols(A_KV_DIM, tm)],
        out_shape=[jax.ShapeDtypeStruct((A_Q_ROWS, t), BF16),
                   jax.ShapeDtypeStruct((t, A_KV_DIM), BF16),
                   jax.ShapeDtypeStruct((A_KV_DIM, t), BF16)],
        compiler_params=_params(1),
        name="attn_proj",
    )(x, nw, wt, bt, wk, bk, cos_a, sin_a, cos_at, sin_at)


def _attn_core_kernel(sink_ref, qt_ref, kp_ref, kc_ref, vtp_ref, vtc_ref, ot_ref):
    L = A_CHUNK
    step = pl.program_id(1)
    key = lax.broadcasted_iota(jnp.int32, (L, L), 0)
    qry = lax.broadcasted_iota(jnp.int32, (L, L), 1)
    from_prev = key > qry
    first_bias = jnp.where(step > 0, 0.0, -jnp.inf).astype(F32)
    zero_q = jnp.zeros((A_HEAD_DIM, A_GROUP * L), BF16)
    units = [(u, g) for u in range(A_BLOCKS_PER_STEP) for g in range(A_KV_HEADS)]
    scores, probs_of = {}, {}

    def score(unit):
        u, g = unit
        col = slice((g // 2) * LANES, (g // 2 + 1) * LANES)
        if u == 0:
            kk = jnp.concatenate([kp_ref[:, col], kc_ref[:L, col]], axis=0)
        else:
            kk = kc_ref[(u - 1) * L:(u + 1) * L, col]
        q_cat = jnp.concatenate([qt_ref[(A_GROUP * g + j) * A_HEAD_DIM:(A_GROUP * g + j + 1) * A_HEAD_DIM,
                                        u * L:(u + 1) * L] for j in range(A_GROUP)], axis=1)
        q_ext = jnp.concatenate([q_cat, zero_q] if g % 2 == 0 else [zero_q, q_cat], axis=0)
        scores[unit] = _dot(kk, q_ext)

    def softmax(unit):
        u, g = unit
        s = scores.pop(unit)
        probs = []
        for j in range(A_GROUP):
            sink = sink_ref[A_GROUP * g + j]
            s_prev = s[:L, j * L:(j + 1) * L]
            if u == 0:
                s_prev = s_prev + first_bias
            s_own = s[L:, j * L:(j + 1) * L]
            sc = jnp.where(from_prev, s_prev, s_own)
            mx = jnp.maximum(jnp.max(sc, axis=0, keepdims=True), sink)
            p = jnp.exp(sc - mx)
            denom = jnp.sum(p, axis=0, keepdims=True) + jnp.exp(sink - mx)
            pn = p * (1.0 / denom)
            zero = jnp.zeros_like(pn)
            probs.append(jnp.concatenate([jnp.where(from_prev, pn, zero), jnp.where(from_prev, zero, pn)],
                                         axis=0).astype(BF16))
        probs_of[unit] = jnp.concatenate(probs, axis=1)

    def weighted_values(unit):
        u, g = unit
        rows = slice(g * A_HEAD_DIM, (g + 1) * A_HEAD_DIM)
        if u == 0:
            vt = jnp.concatenate([vtp_ref[rows, :], vtc_ref[rows, :L]], axis=1)
        else:
            vt = vtc_ref[rows, (u - 1) * L:(u + 1) * L]
        out = _dot(vt, probs_of.pop(unit))
        for j in range(A_GROUP):
            head = A_GROUP * g + j
            ot_ref[head * A_HEAD_DIM:(head + 1) * A_HEAD_DIM, u * L:(u + 1) * L] = (
                out[:, j * L:(j + 1) * L].astype(BF16))

    n = len(units)
    score(units[0])
    for i in range(n + 1):
        if i + 1 < n:
            score(units[i + 1])
        if i < n:
            softmax(units[i])
        if i >= 1:
            weighted_values(units[i - 1])


def _attn_core(qt, k, vt, sinks, *, batch, seq):
    L = A_CHUNK
    W = A_BLOCKS_PER_STEP * L
    assert seq % W == 0
    ns = seq // W
    cur = lambda b, i: (b * ns + i, 0)
    cur_t = lambda b, i: (0, b * ns + i)
    prev = lambda b, i: ((b * ns + i) * A_BLOCKS_PER_STEP - jnp.minimum(i, 1), 0)
    prev_t = lambda b, i: (0, (b * ns + i) * A_BLOCKS_PER_STEP - jnp.minimum(i, 1))
    return pl.pallas_call(
        _attn_core_kernel,
        grid=(batch, ns),
        in_specs=[pl.BlockSpec(memory_space=pltpu.SMEM),
                  pl.BlockSpec((A_Q_ROWS, W), cur_t),
                  pl.BlockSpec((L, A_KV_DIM), prev), pl.BlockSpec((W, A_KV_DIM), cur),
                  pl.BlockSpec((A_KV_DIM, L), prev_t), pl.BlockSpec((A_KV_DIM, W), cur_t)],
        out_specs=pl.BlockSpec((A_Q_ROWS, W), cur_t),
        out_shape=jax.ShapeDtypeStruct((A_Q_ROWS, batch * seq), BF16),
        compiler_params=_params(2),
        name="attn_core",
    )(sinks, qt, k, k, vt, vt)


M_QK_COLS = M_HEADS * M_QK_DIM
M_V_COLS = M_HEADS * M_V_DIM
M_T_ROWS = M_QK_COLS + 2 * M_V_COLS
M_PROJ_ROWS = 512
M_STATE_ROWS = M_V_DIM + BF16_ROWS


def _mlstm_proj_kernel(x_ref, nw_ref, wt_ref, wk_ref, wg_ref, bg_ref,
                       qt_ref, k_ref, vt_ref, ot_ref, src_ref, gt_ref):
    sub = x_ref.shape[0] // ROW_SPLIT
    H = M_HEADS
    lane = lax.broadcasted_iota(jnp.int32, (sub, LANES), 1)
    time = lax.broadcasted_iota(jnp.int32, (sub, LANES), 0) & (M_CHUNK - 1)
    for split in range(ROW_SPLIT):
        tok = slice(split * sub, (split + 1) * sub)
        h = _rms(x_ref[tok, :], nw_ref[...]).astype(BF16)
        k_ref[tok, :] = _dot(h, wk_ref[...]).astype(BF16)
        for c in range(M_T_ROWS // M_PROJ_ROWS):
            lo = c * M_PROJ_ROWS
            part = _dot_nt(wt_ref[lo:lo + M_PROJ_ROWS, :], h)
            if lo < M_QK_COLS:
                qt_ref[:, tok] = (part * (M_QK_DIM ** -0.5)).astype(BF16)
            elif lo < M_QK_COLS + M_V_COLS:
                vt_ref[lo - M_QK_COLS:lo - M_QK_COLS + M_PROJ_ROWS, tok] = part.astype(BF16)
            else:
                o_lo = lo - M_QK_COLS - M_V_COLS
                ot_ref[o_lo:o_lo + M_PROJ_ROWS, tok] = jax.nn.sigmoid(part)

        gates = _dot(h, wg_ref[...]) + bg_ref[...]
        gates = M_GATE_CAP * jnp.tanh(gates / M_GATE_CAP)
        b = jnp.where((lane >= H) & (lane < 2 * H), jax.nn.log_sigmoid(gates), 0.0)
        shift = 1
        while shift < M_CHUNK:
            b = b + jnp.where(time >= shift, pltpu.roll(b, shift, 0), 0.0)
            shift *= 2
        src_ref[tok, :] = gates - pltpu.roll(b, LANES - H, 1)
        gt_ref[:, tok] = jnp.where(lane < H, gates, b).T[:2 * H]


def _mlstm_proj(x, nw, wt, wk, wg, bg, *, tm):
    t = x.shape[0]
    assert (tm // ROW_SPLIT) % M_CHUNK == 0 and M_CHUNK & (M_CHUNK - 1) == 0
    return pl.pallas_call(
        _mlstm_proj_kernel,
        grid=(t // tm,),
        in_specs=[_rows(tm, D_MODEL), _resident((1, D_MODEL)), _resident((M_T_ROWS, D_MODEL)),
                  _resident((D_MODEL, M_QK_COLS)), _resident((D_MODEL, LANES)), _resident((1, LANES))],
        out_specs=[_cols(M_QK_COLS, tm), _rows(tm, M_QK_COLS), _cols(M_V_COLS, tm),
                   _cols(M_V_COLS, tm), _rows(tm, LANES), _cols(2 * M_HEADS, tm)],
        out_shape=[jax.ShapeDtypeStruct((M_QK_COLS, t), BF16),
                   jax.ShapeDtypeStruct((t, M_QK_COLS), BF16),
                   jax.ShapeDtypeStruct((M_V_COLS, t), BF16),
                   jax.ShapeDtypeStruct((M_V_COLS, t), F32),
                   jax.ShapeDtypeStruct((t, LANES), F32),
                   jax.ShapeDtypeStruct((2 * M_HEADS, t), F32)],
        compiler_params=_params(1),
        name="mlstm_proj",
    )(x, nw, wt, wk, wg, bg)


def _mlstm_core_kernel(qt_ref, k_ref, vt_ref, ot_ref, src_ref, gt_ref, nwt_ref, yt_ref, c_sc, m_sc):
    L = M_CHUNK
    H = M_HEADS

    @pl.when(pl.program_id(1) == 0)
    def _():
        c_sc[...] = jnp.zeros_like(c_sc)
        m_sc[...] = jnp.zeros_like(m_sc)

    src_i = lax.broadcasted_iota(jnp.int32, (L, L), 0)
    tgt_i = lax.broadcasted_iota(jnp.int32, (L, L), 1)
    causal = src_i <= tgt_i
    ones_rows = (lax.broadcasted_iota(jnp.int32, (BF16_ROWS, L), 0) == 0).astype(BF16)
    zero_q = jnp.zeros((M_QK_DIM, L), BF16)
    low = lax.broadcasted_iota(jnp.int32, (L, LANES), 1) < M_QK_DIM

    units = [(c, h) for c in range(M_CHUNKS_PER_STEP) for h in range(H)]
    st = {unit: dict() for unit in units}

    def score(unit):
        c, h = unit
        d = st[unit]
        tok = slice(c * L, (c + 1) * L)
        pair = h // 2
        k_pair = k_ref[tok, pair * LANES:(pair + 1) * LANES]
        d["k"] = jnp.where(low if h % 2 == 0 else jnp.logical_not(low), k_pair, jnp.zeros_like(k_pair))
        qt_h = qt_ref[h * M_QK_DIM:(h + 1) * M_QK_DIM, tok]
        d["qt"] = jnp.concatenate([qt_h, zero_q] if h % 2 == 0 else [zero_q, qt_h], axis=0)
        d["v"] = jnp.concatenate([vt_ref[h * M_V_DIM:(h + 1) * M_V_DIM, tok], ones_rows], axis=0)
        d["state"] = c_sc[h]
        d["s"] = _dot(d["k"], d["qt"])
        d["carry"] = _dot(d["state"].astype(BF16), d["qt"])

    def gate(unit):
        c, h = unit
        d = st[unit]
        tok = slice(c * L, (c + 1) * L)
        src_col = src_ref[tok, h:h + 1]
        d["b"] = gt_ref[H + h:H + h + 1, tok]
        d["m_prev"] = m_sc[h][:, :1]
        dmat = jnp.where(causal, src_col + d["b"], -jnp.inf)
        inter = d["b"] + d["m_prev"]
        d["m_row"] = jnp.maximum(inter, jnp.max(dmat, axis=0, keepdims=True))
        d["w_intra"] = jnp.exp(dmat - d["m_row"])
        d["w_inter"] = jnp.exp(inter - d["m_row"])

    def emit(unit):
        c, h = unit
        d = st[unit]
        tok = slice(c * L, (c + 1) * L)
        tot = _dot(d["v"], (d["s"] * d["w_intra"]).astype(BF16)) + d["w_inter"] * d["carry"]
        num = tot[:M_V_DIM]
        nq = tot[M_V_DIM:M_V_DIM + 1]
        h_out = num * (1.0 / jnp.maximum(jnp.abs(nq), jnp.exp(-d["m_row"])))
        h_out = h_out * lax.rsqrt(jnp.mean(h_out * h_out, axis=0, keepdims=True) + EPS)
        rows = slice(h * M_V_DIM, (h + 1) * M_V_DIM)
        yt_ref[rows, tok] = (ot_ref[rows, tok] * (h_out * nwt_ref[rows, :])).astype(BF16)

    def update(unit):
        c, h = unit
        d = st[unit]
        b_last = d["b"][:, L - 1:L]
        dec = b_last + (gt_ref[h:h + 1, c * L:(c + 1) * L] - d["b"])
        m_new = jnp.maximum(b_last + d["m_prev"], jnp.max(dec, axis=-1, keepdims=True))
        wk = jnp.exp(dec - m_new)
        keep = jnp.exp(b_last + d["m_prev"] - m_new)
        v_scaled = (d["v"].astype(F32) * wk).astype(BF16)
        c_sc[h] = keep * d["state"] + _dot(v_scaled, d["k"])
        m_sc[h] = jnp.broadcast_to(m_new, (1, LANES))
        d.clear()

    n = len(units)
    score(units[0])
    for i in range(n + 1):
        if i + 1 < n:
            score(units[i + 1])
        if i < n:
            gate(units[i])
            emit(units[i])
        if i >= 1:
            update(units[i - 1])


def _mlstm_core(qt, k, vt, ot, src, gt, nwt, *, batch, seq):
    L = M_CHUNK
    W = M_CHUNKS_PER_STEP * L
    assert seq % W == 0
    nb = seq // W
    cur = lambda b, i: (b * nb + i, 0)
    cur_t = lambda b, i: (0, b * nb + i)
    return pl.pallas_call(
        _mlstm_core_kernel,
        grid=(batch, nb),
        in_specs=[pl.BlockSpec((M_QK_COLS, W), cur_t), pl.BlockSpec((W, M_QK_COLS), cur),
                  pl.BlockSpec((M_V_COLS, W), cur_t), pl.BlockSpec((M_V_COLS, W), cur_t),
                  pl.BlockSpec((W, LANES), cur), pl.BlockSpec((2 * M_HEADS, W), cur_t), _resident((M_V_COLS, L))],
        out_specs=pl.BlockSpec((M_V_COLS, W), cur_t),
        out_shape=jax.ShapeDtypeStruct((M_V_COLS, batch * seq), BF16),
        scratch_shapes=[pltpu.VMEM((M_HEADS, M_STATE_ROWS, LANES), F32),
                        pltpu.VMEM((M_HEADS, 1, LANES), F32)],
        compiler_params=_params(2),
        name="mlstm_core",
    )(qt, k, vt, ot, src, gt, nwt)


R_QK_COLS = R_HEADS * R_QK_DIM
R_V_COLS = R_HEADS * R_V_DIM
R_PROJ_COLS = R_QK_COLS + 2 * R_V_COLS
R_PROJ_CHUNK = 512


def _ret_proj_kernel(x_ref, nw_ref, w_ref, wkt_ref, cos_ref, sin_ref, cost_ref, sint_ref, qdec_ref, kdec_ref,
                     q_ref, kt_ref, v_ref, g_ref):
    sub = x_ref.shape[0] // ROW_SPLIT
    half = R_QK_DIM // 2
    for split in range(ROW_SPLIT):
        tok = slice(split * sub, (split + 1) * sub)
        h = _rms(x_ref[tok, :], nw_ref[...]).astype(BF16)
        cos = cos_ref[tok, :]
        sin = sin_ref[tok, :]
        for c in range(R_QK_COLS // R_PROJ_CHUNK):
            lo = c * R_PROJ_CHUNK
            t = _dot(h, w_ref[:, lo:lo + R_PROJ_CHUNK])
            for hh in range(R_PROJ_CHUNK // R_QK_DIM):
                head = lo // R_QK_DIM + hh
                dec = qdec_ref[head, tok, :]
                x1 = t[:, hh * R_QK_DIM:hh * R_QK_DIM + half]
                x2 = t[:, hh * R_QK_DIM + half:(hh + 1) * R_QK_DIM]
                base = head * R_QK_DIM
                q_ref[tok, base:base + half] = ((x1 * cos - x2 * sin) * dec).astype(BF16)
                q_ref[tok, base + half:base + R_QK_DIM] = ((x2 * cos + x1 * sin) * dec).astype(BF16)
        for c in range(2 * R_V_COLS // R_PROJ_CHUNK):
            lo = c * R_PROJ_CHUNK
            t = _dot(h, w_ref[:, R_QK_COLS + lo:R_QK_COLS + lo + R_PROJ_CHUNK])
            if lo < R_V_COLS:
                v_ref[tok, lo:lo + R_PROJ_CHUNK] = t.astype(BF16)
            else:
                g_lo = lo - R_V_COLS
                g_ref[tok, g_lo:g_lo + R_PROJ_CHUNK] = (t * jax.nn.sigmoid(t)).astype(BF16)
        cos_t = cost_ref[:, tok]
        sin_t = sint_ref[:, tok]
        for head in range(R_HEADS):
            base = head * R_QK_DIM
            kt = _dot_nt(wkt_ref[base:base + R_QK_DIM, :], h)
            dec = kdec_ref[head:head + 1, tok]
            x1 = kt[:half]
            x2 = kt[half:]
            kt_ref[base:base + half, tok] = ((x1 * cos_t - x2 * sin_t) * dec).astype(BF16)
            kt_ref[base + half:base + R_QK_DIM, tok] = ((x2 * cos_t + x1 * sin_t) * dec).astype(BF16)


def _ret_proj(x, nw, w, wkt, cos_r, sin_r, cos_rt, sin_rt, qdec, kdec, *, tm):
    t = x.shape[0]
    return pl.pallas_call(
        _ret_proj_kernel,
        grid=(t // tm,),
        in_specs=[_rows(tm, D_MODEL), _resident((1, D_MODEL)), _resident((D_MODEL, R_PROJ_COLS)),
                  _resident((R_QK_COLS, D_MODEL)),
                  _rows(tm, LANES), _rows(tm, LANES), _cols(LANES, tm), _cols(LANES, tm),
                  _resident((R_HEADS, tm, LANES)), _resident((SUBLANES, tm))],
        out_specs=[_rows(tm, R_QK_COLS), _cols(R_QK_COLS, tm), _rows(tm, R_V_COLS), _rows(tm, R_V_COLS)],
        out_shape=[jax.ShapeDtypeStruct((t, R_QK_COLS), BF16),
                   jax.ShapeDtypeStruct((R_QK_COLS, t), BF16),
                   jax.ShapeDtypeStruct((t, R_V_COLS), BF16),
                   jax.ShapeDtypeStruct((t, R_V_COLS), BF16)],
        compiler_params=_params(1),
        name="ret_proj",
    )(x, nw, w, wkt, cos_r, sin_r, cos_rt, sin_rt, qdec, kdec)


def _ret_core_kernel(cdec_ref, q_ref, kt_ref, v_ref, g_ref, nw_ref, y_ref, s_sc):
    L = R_CHUNK

    @pl.when(pl.program_id(1) == 0)
    def _():
        s_sc[...] = jnp.zeros_like(s_sc)

    row = lax.broadcasted_iota(jnp.int32, (L, L), 0)
    col = lax.broadcasted_iota(jnp.int32, (L, L), 1)
    causal = row >= col
    nw = nw_ref[...]
    units = [(c, h) for c in range(R_CHUNKS_PER_STEP) for h in range(R_HEADS)]
    st = {unit: dict() for unit in units}

    def score(unit):
        c, h = unit
        d = st[unit]
        tok = slice(c * L, (c + 1) * L)
        d["q"] = q_ref[tok, h * R_QK_DIM:(h + 1) * R_QK_DIM]
        d["kt"] = kt_ref[h * R_QK_DIM:(h + 1) * R_QK_DIM, tok]
        d["v"] = v_ref[tok, h * R_V_DIM:(h + 1) * R_V_DIM]
        d["state"] = s_sc[h]
        d["scores"] = _dot(d["q"], d["kt"])
        d["carry"] = _dot(d["q"], d["state"].astype(BF16))

    def emit(unit):
        c, h = unit
        d = st[unit]
        tok = slice(c * L, (c + 1) * L)
        scores = jnp.where(causal, d["scores"], 0.0)
        y = _dot(scores.astype(BF16), d["v"]) + d["carry"]
        s_sc[h] = cdec_ref[h] * (d["state"] + _dot(d["kt"], d["v"]))
        d.clear()
        mu = jnp.mean(y, axis=-1, keepdims=True)
        yc = y - mu
        yn = yc * lax.rsqrt(jnp.mean(yc * yc, axis=-1, keepdims=True) + EPS)
        gate = g_ref[tok, h * R_V_DIM:(h + 1) * R_V_DIM]
        out = gate * (yn * nw[:, h * R_V_DIM:(h + 1) * R_V_DIM])
        y_ref[tok, h * R_V_DIM:(h + 1) * R_V_DIM] = out.astype(BF16)

    score(units[0])
    for i, unit in enumerate(units):
        if i + 1 < len(units):
            score(units[i + 1])
        emit(unit)


def _ret_core(q, kt, v, g, nw, cdec, *, batch, seq):
    L = R_CHUNK
    W = R_CHUNKS_PER_STEP * L
    assert seq % W == 0
    nb = seq // W
    cur = lambda b, i: (b * nb + i, 0)
    cur_t = lambda b, i: (0, b * nb + i)
    return pl.pallas_call(
        _ret_core_kernel,
        grid=(batch, nb),
        in_specs=[pl.BlockSpec(memory_space=pltpu.SMEM),
                  pl.BlockSpec((W, R_QK_COLS), cur), pl.BlockSpec((R_QK_COLS, W), cur_t),
                  pl.BlockSpec((W, R_V_COLS), cur), pl.BlockSpec((W, R_V_COLS), cur),
                  _resident((1, R_V_COLS))],
        out_specs=pl.BlockSpec((W, R_V_COLS), cur),
        out_shape=jax.ShapeDtypeStruct((batch * seq, R_V_COLS), BF16),
        scratch_shapes=[pltpu.VMEM((R_HEADS, R_QK_DIM, R_V_DIM), F32)],
        compiler_params=_params(2),
        name="ret_core",
    )(cdec, q, kt, v, g, nw)


def _rope_constants(tm):
    inv_a = 1.0 / (ROPE_THETA ** (jnp.arange(0, A_ROPE_DIM, 2, dtype=F32) / A_ROPE_DIM))
    inv_r = 1.0 / (R_THETA ** jnp.linspace(0.0, 1.0, R_QK_DIM // 2, dtype=F32))
    return (jnp.broadcast_to(inv_a[:, None], (A_ROPE_DIM // 2, tm)),
            jnp.broadcast_to(inv_r[:, None], (R_QK_DIM // 2, tm)))


def _retention_constants(tm):
    L = R_CHUNK
    log_gamma = jnp.log(1.0 - 2.0 ** (-5.0 - jnp.arange(R_HEADS, dtype=F32)))
    j = jnp.asarray(np.arange(tm) % L, F32)
    q_decay = jnp.exp((j + 1.0) * log_gamma[:, None])
    k_decay = jnp.exp(-(j + 1.0) * log_gamma[:, None]) * (R_QK_DIM ** -0.5)
    chunk_decay = jnp.exp(L * log_gamma)
    qdec = jnp.broadcast_to(q_decay[:, :, None], (R_HEADS, tm, LANES))
    kdec = jnp.pad(k_decay, ((0, SUBLANES - R_HEADS), (0, 0)))
    return qdec, kdec, chunk_decay


def _attn_weight_layout(w_qkv, b_qkv, tm):
    nq = A_Q_ROWS
    qv = np.concatenate([np.arange(nq), np.arange(nq + A_KV_DIM, nq + 2 * A_KV_DIM)])
    wt = w_qkv[:, qv].T.astype(BF16)
    bt = jnp.broadcast_to(b_qkv[qv][:, None], (nq + A_KV_DIM, tm))
    wk = w_qkv[:, nq:nq + A_KV_DIM].astype(BF16)
    bk = b_qkv[nq:nq + A_KV_DIM].reshape(1, A_KV_DIM)
    return wt, bt, wk, bk


def _ret_weight_layout(w_in):
    head = np.concatenate([np.arange(0, R_QK_DIM, 2), np.arange(1, R_QK_DIM, 2)])
    perm = np.concatenate([h * R_QK_DIM + head for h in range(R_HEADS)])
    w = jnp.concatenate([w_in[:, perm], w_in[:, 2 * R_QK_COLS:]], axis=-1).astype(BF16)
    wkt = w_in[:, R_QK_COLS + perm].T.astype(BF16)
    return w, wkt


def _mlstm_weight_layout(w_in, b_if):
    pad = LANES - 2 * M_HEADS
    gate_lo = 2 * M_QK_COLS + 2 * M_V_COLS
    wt = jnp.concatenate([w_in[:, :M_QK_COLS], w_in[:, 2 * M_QK_COLS:gate_lo]], axis=-1).T.astype(BF16)
    wk = w_in[:, M_QK_COLS:2 * M_QK_COLS].astype(BF16)
    wg = jnp.pad(w_in[:, gate_lo:], ((0, 0), (0, pad))).astype(BF16)
    return wt, wk, wg, jnp.pad(b_if, (0, pad)).reshape(1, LANES)


def kernel(x, positions, norm_w, final_norm_w, ffn_w_in, ffn_w_out, attn_w_qkv, attn_b_qkv, attn_w_o, attn_b_o,
           attn_sinks, mlstm_w_in, mlstm_b_if, mlstm_norm_w, mlstm_w_out, ret_w_in, ret_norm_w, ret_w_out):
    batch, seq, d = x.shape
    t = batch * seq
    tm = min(ROW_TILE, t)
    assert d == D_MODEL and t % tm == 0 and (tm // ROW_SPLIT) % R_CHUNK == 0
    assert seq % R_CHUNK == 0 and seq % M_CHUNK == 0 and seq % (A_CHUNK * A_BLOCKS_PER_STEP) == 0

    xf = x.reshape(t, d)
    inv_a, inv_r = _rope_constants(tm)
    cos_a, sin_a, cos_r, sin_r, cos_at, sin_at, cos_rt, sin_rt = _rope_tables(
        positions.astype(F32).reshape(1, t), inv_a, inv_r, tm)
    qdec, kdec, cdec = _retention_constants(tm)
    zero_bias = jnp.zeros((1, D_MODEL), F32)
    norm_w4 = norm_w.reshape(DEPTH, 3, 1, d)
    w_in_bf = ffn_w_in.astype(BF16)
    w_out_bf = ffn_w_out.astype(BF16)

    for i in range(DEPTH):
        kind, slot = i % N_MIXERS, i // N_MIXERS
        xf = _ffn(xf, norm_w4, w_in_bf, w_out_bf, i, 0, tm=tm)
        nw = norm_w[i, 1].reshape(1, d)
        if kind == 0:
            wt, bt, wk, bk = _attn_weight_layout(attn_w_qkv[slot], attn_b_qkv[slot], tm)
            qt, k, vt = _attn_proj(xf, nw, wt, bt, wk, bk, cos_a, sin_a, cos_at, sin_at, tm=tm)
            yt = _attn_core(qt, k, vt, attn_sinks[slot], batch=batch, seq=seq)
            mix = (yt, attn_w_o[slot].astype(BF16), attn_b_o[slot].reshape(1, d), True)
        elif kind == 1:
            wt, wk, wg, bg = _mlstm_weight_layout(mlstm_w_in[slot], mlstm_b_if[slot])
            qt, k, vt, ot, src, gt = _mlstm_proj(xf, nw, wt, wk, wg, bg, tm=tm)
            nwt = jnp.broadcast_to(mlstm_norm_w[slot][:, None], (M_V_COLS, M_CHUNK))
            yt = _mlstm_core(qt, k, vt, ot, src, gt, nwt, batch=batch, seq=seq)
            mix = (yt, mlstm_w_out[slot].astype(BF16), zero_bias, True)
        else:
            w, wkt = _ret_weight_layout(ret_w_in[slot])
            q, kt, v, g = _ret_proj(xf, nw, w, wkt, cos_r, sin_r, cos_rt, sin_rt, qdec, kdec, tm=tm)
            y = _ret_core(q, kt, v, g, ret_norm_w[slot].reshape(1, R_V_COLS), cdec, batch=batch, seq=seq)
            mix = (y, ret_w_out[slot].astype(BF16), zero_bias, False)
        last = i == DEPTH - 1
        xf = _ffn(xf, norm_w4, w_in_bf, w_out_bf, i, 1, tm=tm, mix=mix,
                  final_w=final_norm_w.reshape(1, d) if last else None)
    return xf.reshape(batch, seq, d)
```

```python
import functools

import jax
import jax.numpy as jnp
import numpy as np
from jax import lax
from jax.experimental import pallas as pl
from jax.experimental.pallas import tpu as pltpu

F32 = jnp.float32
BF16 = jnp.bfloat16

D_MODEL = 1024
DEPTH = 4
N_MIXERS = 3
EPS = 1e-6
D_FF = 2816
A_HEADS = 16
A_KV_HEADS = 4
A_GROUP = A_HEADS // A_KV_HEADS
A_HEAD_DIM = 64
A_WINDOW = 128
A_ROPE_DIM = 16
ROPE_THETA = 500000.0
M_HEADS = 8
M_QK_DIM = 64
M_V_DIM = 128
M_GATE_CAP = 15.0
R_HEADS = 4
R_QK_DIM = 256
R_V_DIM = 512
R_THETA = 10000.0

LANES = 128
SUBLANES = 8
BF16_ROWS = 16
VMEM_LIMIT = 56 * 1024 * 1024

ROW_TILE = 1024
ROW_SPLIT = 2
FF_CHUNK = 256
A_CHUNK = A_WINDOW
A_BLOCKS_PER_STEP = 8
M_CHUNK = 128
M_CHUNKS_PER_STEP = 8
R_CHUNK = 256
R_CHUNKS_PER_STEP = 4


def _params(n_axes):
    return pltpu.CompilerParams(
        dimension_semantics=("arbitrary",) * n_axes,
        vmem_limit_bytes=VMEM_LIMIT,
    )


def _resident(shape):
    nd = len(shape)
    return pl.BlockSpec(shape, lambda *_: (0,) * nd, pipeline_mode=pl.Buffered(1))


def _resident_slice(lead, shape):
    nd = len(shape)
    return pl.BlockSpec((None,) * len(lead) + tuple(shape), lambda *_: tuple(lead) + (0,) * nd,
                        pipeline_mode=pl.Buffered(1))


def _rows(tm, width):
    return pl.BlockSpec((tm, width), lambda i: (i, 0))


def _cols(height, tm):
    return pl.BlockSpec((height, tm), lambda i: (0, i))


def _rms(x, w):
    ms = jnp.mean(x * x, axis=-1, keepdims=True)
    return x * lax.rsqrt(ms + EPS) * w


def _dot(a, b):
    return jnp.dot(a, b, preferred_element_type=F32)


def _dot_nt(a, b):
    return lax.dot_general(a, b, (((1,), (1,)), ((), ())), preferred_element_type=F32)


def _dot_tn(a, b):
    return lax.dot_general(a, b, (((0,), (0,)), ((), ())), preferred_element_type=F32)


def _tables_kernel(pos_ref, inv_a_ref, inv_r_ref, cos_a, sin_a, cos_r, sin_r, cos_at, sin_at, cos_rt, sin_rt):
    tm = pos_ref.shape[1]
    pos = pos_ref[...]
    ang_a = inv_a_ref[...] * pos
    ca = jnp.cos(ang_a)
    sa = jnp.sin(ang_a)
    cos_at[...] = ca
    sin_at[...] = sa
    rest = A_HEAD_DIM - A_ROPE_DIM
    cos_head = jnp.concatenate([ca, ca, jnp.ones((rest, tm), F32)], axis=0)
    sin_head = jnp.concatenate([-sa, sa, jnp.zeros((rest, tm), F32)], axis=0)
    cos_a[...] = jnp.concatenate([cos_head] * (LANES // A_HEAD_DIM), axis=0).T
    sin_a[...] = jnp.concatenate([sin_head] * (LANES // A_HEAD_DIM), axis=0).T
    ang_r = inv_r_ref[...] * pos
    cr = jnp.cos(ang_r)
    sr = jnp.sin(ang_r)
    cos_rt[...] = cr
    sin_rt[...] = sr
    cos_r[...] = cr.T
    sin_r[...] = sr.T


def _rope_tables(pos_row, inv_a, inv_r, tm):
    t = pos_row.shape[1]
    half = A_ROPE_DIM // 2
    tok = jax.ShapeDtypeStruct((t, LANES), F32)
    return pl.pallas_call(
        _tables_kernel,
        grid=(t // tm,),
        in_specs=[_cols(1, tm), _resident((half, tm)), _resident((LANES, tm))],
        out_specs=[_rows(tm, LANES)] * 4 + [_cols(half, tm), _cols(half, tm), _cols(LANES, tm), _cols(LANES, tm)],
        out_shape=[tok] * 4 + [jax.ShapeDtypeStruct((half, t), F32)] * 2 + [jax.ShapeDtypeStruct((LANES, t), F32)] * 2,
        compiler_params=_params(1),
        name="rope_tables",
    )(pos_row, inv_a, inv_r)


def _ffn_kernel(*refs, mix_in, mix_feature_major, final_norm):
    refs = list(refs)
    x_ref = refs.pop(0)
    if mix_in:
        y_ref, wo_ref, bo_ref = refs.pop(0), refs.pop(0), refs.pop(0)
    nw_ref, win_ref, wout_ref = refs.pop(0), refs.pop(0), refs.pop(0)
    fw_ref = refs.pop(0) if final_norm else None
    (o_ref,) = refs

    tm = x_ref.shape[0]
    sub = tm // ROW_SPLIT
    for part in range(ROW_SPLIT):
        rows = slice(part * sub, (part + 1) * sub)
        x = x_ref[rows, :]
        if mix_in:
            if mix_feature_major:
                mixed = _dot_tn(y_ref[:, rows], wo_ref[...])
            else:
                mixed = _dot(y_ref[rows, :], wo_ref[...])
            x = x + (mixed + bo_ref[...])
        h = _rms(x, nw_ref[...]).astype(BF16)
        acc = jnp.zeros_like(x)
        for c in range(D_FF // FF_CHUNK):
            lo = c * FF_CHUNK
            g = _dot(h, win_ref[:, lo:lo + FF_CHUNK])
            u = _dot(h, win_ref[:, D_FF + lo:D_FF + lo + FF_CHUNK])
            a = (g * jax.nn.sigmoid(g) * u).astype(BF16)
            acc = acc + _dot(a, wout_ref[lo:lo + FF_CHUNK, :])
        y = x + 0.5 * acc
        if final_norm:
            y = _rms(y, fw_ref[...])
        o_ref[rows, :] = y


def _ffn(x, norm_w, w_in, w_out, layer, which, *, tm, mix=None, final_w=None):
    t = x.shape[0]
    in_specs = [_rows(tm, D_MODEL)]
    args = [x]
    feature_major = False
    if mix is not None:
        y, w_o, b_o, feature_major = mix
        kdim = w_o.shape[0]
        assert y.shape == ((kdim, t) if feature_major else (t, kdim))
        in_specs += [_cols(kdim, tm) if feature_major else _rows(tm, kdim),
                     _resident((kdim, D_MODEL)), _resident((1, D_MODEL))]
        args += [y, w_o, b_o]
    in_specs += [_resident_slice((layer, 2 * which), (1, D_MODEL)),
                 _resident_slice((layer, which), (D_MODEL, 2 * D_FF)),
                 _resident_slice((layer, which), (D_FF, D_MODEL))]
    args += [norm_w, w_in, w_out]
    if final_w is not None:
        in_specs.append(_resident((1, D_MODEL)))
        args.append(final_w)
    return pl.pallas_call(
        functools.partial(_ffn_kernel, mix_in=mix is not None, mix_feature_major=feature_major,
                          final_norm=final_w is not None),
        grid=(t // tm,),
        in_specs=in_specs,
        out_specs=_rows(tm, D_MODEL),
        out_shape=jax.ShapeDtypeStruct((t, D_MODEL), F32),
        compiler_params=_params(1),
        name="ffn_mix" if mix is not None else "ffn",
    )(*args)


A_Q_ROWS = A_HEADS * A_HEAD_DIM
A_KV_DIM = A_KV_HEADS * A_HEAD_DIM
A_PROJ_ROWS = 256


def _attn_proj_kernel(x_ref, nw_ref, wt_ref, bt_ref, wk_ref, bk_ref, cos_ref, sin_ref, cost_ref, sint_ref,
                      qt_ref, k_ref, vt_ref):
    sub = x_ref.shape[0] // ROW_SPLIT
    half = A_ROPE_DIM // 2
    lane = lax.broadcasted_iota(jnp.int32, (sub, LANES), 1) & (A_HEAD_DIM - 1)
    upper = (lane >= half) & (lane < A_ROPE_DIM)
    q_scale = A_HEAD_DIM ** -0.5
    for split in range(ROW_SPLIT):
        tok = slice(split * sub, (split + 1) * sub)
        h = _rms(x_ref[tok, :], nw_ref[...]).astype(BF16)

        cos = cos_ref[tok, :]
        sin = sin_ref[tok, :]
        k = _dot(h, wk_ref[...]) + bk_ref[...]
        for j in range(A_KV_DIM // LANES):
            piece = k[:, j * LANES:(j + 1) * LANES]
            partner = jnp.where(upper, pltpu.roll(piece, half, 1), pltpu.roll(piece, LANES - half, 1))
            k_ref[tok, j * LANES:(j + 1) * LANES] = (piece * cos + partner * sin).astype(BF16)

        cos_t = cost_ref[:, tok]
        sin_t = sint_ref[:, tok]
        for c in range((A_Q_ROWS + A_KV_DIM) // A_PROJ_ROWS):
            lo = c * A_PROJ_ROWS
            part = _dot_nt(wt_ref[lo:lo + A_PROJ_ROWS, :], h) + bt_ref[lo:lo + A_PROJ_ROWS, tok]
            if lo < A_Q_ROWS:
                for j in range(A_PROJ_ROWS // A_HEAD_DIM):
                    base = j * A_HEAD_DIM
                    x1 = part[base:base + half]
                    x2 = part[base + half:base + 2 * half]
                    rot = jnp.concatenate([x1 * cos_t - x2 * sin_t, x2 * cos_t + x1 * sin_t,
                                           part[base + 2 * half:base + A_HEAD_DIM]], axis=0)
                    qt_ref[lo + base:lo + base + A_HEAD_DIM, tok] = (rot * q_scale).astype(BF16)
            else:
                vt_ref[:, tok] = part.astype(BF16)


def _attn_proj(x, nw, wt, bt, wk, bk, cos_a, sin_a, cos_at, sin_at, *, tm):
    t = x.shape[0]
    half = A_ROPE_DIM // 2
    rows = A_Q_ROWS + A_KV_DIM
    return pl.pallas_call(
        _attn_proj_kernel,
        grid=(t // tm,),
        in_specs=[_rows(tm, D_MODEL), _resident((1, D_MODEL)), _resident((rows, D_MODEL)), _resident((rows, tm)),
                  _resident((D_MODEL, A_KV_DIM)), _resident((1, A_KV_DIM)),
                  _rows(tm, LANES), _rows(tm, LANES), _cols(half, tm), _cols(half, tm)],
        out_specs=[_cols(A_Q_ROWS, tm), _rows(tm, A_KV_DIM), _cols(A_KV_DIM, tm)],
        out_shape=[jax.ShapeDtypeStruct((A_Q_ROWS, t), BF16),
                   jax.ShapeDtypeStruct((t, A_KV_DIM), BF16),
                   jax.ShapeDtypeStruct((A_KV_DIM, t), BF16)],
        compiler_params=_params(1),
        name="attn_proj",
    )(x, nw, wt, bt, wk, bk, cos_a, sin_a, cos_at, sin_at)


def _attn_core_kernel(sink_ref, qt_ref, kp_ref, kc_ref, vtp_ref, vtc_ref, ot_ref):
    L = A_CHUNK
    step = pl.program_id(1)
    key = lax.broadcasted_iota(jnp.int32, (L, L), 0)
    qry = lax.broadcasted_iota(jnp.int32, (L, L), 1)
    from_prev = key > qry
    first_bias = jnp.where(step > 0, 0.0, -jnp.inf).astype(F32)
    zero_q = jnp.zeros((A_HEAD_DIM, A_GROUP * L), BF16)
    units = [(u, g) for u in range(A_BLOCKS_PER_STEP) for g in range(A_KV_HEADS)]
    scores, probs_of = {}, {}

    def score(unit):
        u, g = unit
        col = slice((g // 2) * LANES, (g // 2 + 1) * LANES)
        if u == 0:
            kk = jnp.concatenate([kp_ref[:, col], kc_ref[:L, col]], axis=0)
        else:
            kk = kc_ref[(u - 1) * L:(u + 1) * L, col]
        q_cat = jnp.concatenate([qt_ref[(A_GROUP * g + j) * A_HEAD_DIM:(A_GROUP * g + j + 1) * A_HEAD_DIM,
                                        u * L:(u + 1) * L] for j in range(A_GROUP)], axis=1)
        q_ext = jnp.concatenate([q_cat, zero_q] if g % 2 == 0 else [zero_q, q_cat], axis=0)
        scores[unit] = _dot(kk, q_ext)

    def softmax(unit):
        u, g = unit
        s = scores.pop(unit)
        probs = []
        for j in range(A_GROUP):
            sink = sink_ref[A_GROUP * g + j]
            s_prev = s[:L, j * L:(j + 1) * L]
            if u == 0:
                s_prev = s_prev + first_bias
            s_own = s[L:, j * L:(j + 1) * L]
            sc = jnp.where(from_prev, s_prev, s_own)
            mx = jnp.maximum(jnp.max(sc, axis=0, keepdims=True), sink)
            p = jnp.exp(sc - mx)
            denom = jnp.sum(p, axis=0, keepdims=True) + jnp.exp(sink - mx)
            pn = p * (1.0 / denom)
            zero = jnp.zeros_like(pn)
            probs.append(jnp.concatenate([jnp.where(from_prev, pn, zero), jnp.where(from_prev, zero, pn)],
                                         axis=0).astype(BF16))
        probs_of[unit] = jnp.concatenate(probs, axis=1)

    def weighted_values(unit):
        u, g = unit
        rows = slice(g * A_HEAD_DIM, (g + 1) * A_HEAD_DIM)
        if u == 0:
            vt = jnp.concatenate([vtp_ref[rows, :], vtc_ref[rows, :L]], axis=1)
        else:
            vt = vtc_ref[rows, (u - 1) * L:(u + 1) * L]
        out = _dot(vt, probs_of.pop(unit))
        for j in range(A_GROUP):
            head = A_GROUP * g + j
            ot_ref[head * A_HEAD_DIM:(head + 1) * A_HEAD_DIM, u * L:(u + 1) * L] = (
                out[:, j * L:(j + 1) * L].astype(BF16))

    n = len(units)
    score(units[0])
    for i in range(n + 1):
        if i + 1 < n:
            score(units[i + 1])
        if i < n:
            softmax(units[i])
        if i >= 1:
            weighted_values(units[i - 1])


def _attn_core(qt, k, vt, sinks, *, batch, seq):
    L = A_CHUNK
    W = A_BLOCKS_PER_STEP * L
    assert seq % W == 0
    ns = seq // W
    cur = lambda b, i: (b * ns + i, 0)
    cur_t = lambda b, i: (0, b * ns + i)
    prev = lambda b, i: ((b * ns + i) * A_BLOCKS_PER_STEP - jnp.minimum(i, 1), 0)
    prev_t = lambda b, i: (0, (b * ns + i) * A_BLOCKS_PER_STEP - jnp.minimum(i, 1))
    return pl.pallas_call(
        _attn_core_kernel,
        grid=(batch, ns),
        in_specs=[pl.BlockSpec(memory_space=pltpu.SMEM),
                  pl.BlockSpec((A_Q_ROWS, W), cur_t),
                  pl.BlockSpec((L, A_KV_DIM), prev), pl.BlockSpec((W, A_KV_DIM), cur),
                  pl.BlockSpec((A_KV_DIM, L), prev_t), pl.BlockSpec((A_KV_DIM, W), cur_t)],
        out_specs=pl.BlockSpec((A_Q_ROWS, W), cur_t),
        out_shape=jax.ShapeDtypeStruct((A_Q_ROWS, batch * seq), BF16),
        compiler_params=_params(2),
        name="attn_core",
    )(sinks, qt, k, k, vt, vt)


M_QK_COLS = M_HEADS * M_QK_DIM
M_V_COLS = M_HEADS * M_V_DIM
M_T_ROWS = M_QK_COLS + 2 * M_V_COLS
M_PROJ_ROWS = 512
M_STATE_ROWS = M_V_DIM + BF16_ROWS


def _mlstm_proj_kernel(x_ref, nw_ref, wt_ref, wk_ref, wg_ref, bg_ref,
                       qt_ref, k_ref, vt_ref, ot_ref, src_ref, gt_ref):
    sub = x_ref.shape[0] // ROW_SPLIT
    H = M_HEADS
    lane = lax.broadcasted_iota(jnp.int32, (sub, LANES), 1)
    time = lax.broadcasted_iota(jnp.int32, (sub, LANES), 0) & (M_CHUNK - 1)
    for split in range(ROW_SPLIT):
        tok = slice(split * sub, (split + 1) * sub)
        h = _rms(x_ref[tok, :], nw_ref[...]).astype(BF16)
        k_ref[tok, :] = _dot(h, wk_ref[...]).astype(BF16)
        for c in range(M_T_ROWS // M_PROJ_ROWS):
            lo = c * M_PROJ_ROWS
            part = _dot_nt(wt_ref[lo:lo + M_PROJ_ROWS, :], h)
            if lo < M_QK_COLS:
                qt_ref[:, tok] = (part * (M_QK_DIM ** -0.5)).astype(BF16)
            elif lo < M_QK_COLS + M_V_COLS:
                vt_ref[lo - M_QK_COLS:lo - M_QK_COLS + M_PROJ_ROWS, tok] = part.astype(BF16)
            else:
                o_lo = lo - M_QK_COLS - M_V_COLS
                ot_ref[o_lo:o_lo + M_PROJ_ROWS, tok] = jax.nn.sigmoid(part)

        gates = _dot(h, wg_ref[...]) + bg_ref[...]
        gates = M_GATE_CAP * jnp.tanh(gates / M_GATE_CAP)
        b = jnp.where((lane >= H) & (lane < 2 * H), jax.nn.log_sigmoid(gates), 0.0)
        shift = 1
        while shift < M_CHUNK:
            b = b + jnp.where(time >= shift, pltpu.roll(b, shift, 0), 0.0)
            shift *= 2
        src_ref[tok, :] = gates - pltpu.roll(b, LANES - H, 1)
        gt_ref[:, tok] = jnp.where(lane < H, gates, b).T[:2 * H]


def _mlstm_proj(x, nw, wt, wk, wg, bg, *, tm):
    t = x.shape[0]
    assert (tm // ROW_SPLIT) % M_CHUNK == 0 and M_CHUNK & (M_CHUNK - 1) == 0
    return pl.pallas_call(
        _mlstm_proj_kernel,
        grid=(t // tm,),
        in_specs=[_rows(tm, D_MODEL), _resident((1, D_MODEL)), _resident((M_T_ROWS, D_MODEL)),
                  _resident((D_MODEL, M_QK_COLS)), _resident((D_MODEL, LANES)), _resident((1, LANES))],
        out_specs=[_cols(M_QK_COLS, tm), _rows(tm, M_QK_COLS), _cols(M_V_COLS, tm),
                   _cols(M_V_COLS, tm), _rows(tm, LANES), _cols(2 * M_HEADS, tm)],
        out_shape=[jax.ShapeDtypeStruct((M_QK_COLS, t), BF16),
                   jax.ShapeDtypeStruct((t, M_QK_COLS), BF16),
                   jax.ShapeDtypeStruct((M_V_COLS, t), BF16),
                   jax.ShapeDtypeStruct((M_V_COLS, t), F32),
                   jax.ShapeDtypeStruct((t, LANES), F32),
                   jax.ShapeDtypeStruct((2 * M_HEADS, t), F32)],
        compiler_params=_params(1),
        name="mlstm_proj",
    )(x, nw, wt, wk, wg, bg)


def _mlstm_core_kernel(qt_ref, k_ref, vt_ref, ot_ref, src_ref, gt_ref, nwt_ref, yt_ref, c_sc, m_sc):
    L = M_CHUNK
    H = M_HEADS

    @pl.when(pl.program_id(1) == 0)
    def _():
        c_sc[...] = jnp.zeros_like(c_sc)
        m_sc[...] = jnp.zeros_like(m_sc)

    src_i = lax.broadcasted_iota(jnp.int32, (L, L), 0)
    tgt_i = lax.broadcasted_iota(jnp.int32, (L, L), 1)
    causal = src_i <= tgt_i
    ones_rows = (lax.broadcasted_iota(jnp.int32, (BF16_ROWS, L), 0) == 0).astype(BF16)
    zero_q = jnp.zeros((M_QK_DIM, L), BF16)
    low = lax.broadcasted_iota(jnp.int32, (L, LANES), 1) < M_QK_DIM

    units = [(c, h) for c in range(M_CHUNKS_PER_STEP) for h in range(H)]
    st = {unit: dict() for unit in units}

    def score(unit):
        c, h = unit
        d = st[unit]
        tok = slice(c * L, (c + 1) * L)
        pair = h // 2
        k_pair = k_ref[tok, pair * LANES:(pair + 1) * LANES]
        d["k"] = jnp.where(low if h % 2 == 0 else jnp.logical_not(low), k_pair, jnp.zeros_like(k_pair))
        qt_h = qt_ref[h * M_QK_DIM:(h + 1) * M_QK_DIM, tok]
        d["qt"] = jnp.concatenate([qt_h, zero_q] if h % 2 == 0 else [zero_q, qt_h], axis=0)
        d["v"] = jnp.concatenate([vt_ref[h * M_V_DIM:(h + 1) * M_V_DIM, tok], ones_rows], axis=0)
        d["state"] = c_sc[h]
        d["s"] = _dot(d["k"], d["qt"])
        d["carry"] = _dot(d["state"].astype(BF16), d["qt"])

    def gate(unit):
        c, h = unit
        d = st[unit]
        tok = slice(c * L, (c + 1) * L)
        src_col = src_ref[tok, h:h + 1]
        d["b"] = gt_ref[H + h:H + h + 1, tok]
        d["m_prev"] = m_sc[h][:, :1]
        dmat = jnp.where(causal, src_col + d["b"], -jnp.inf)
        inter = d["b"] + d["m_prev"]
        d["m_row"] = jnp.maximum(inter, jnp.max(dmat, axis=0, keepdims=True))
        d["w_intra"] = jnp.exp(dmat - d["m_row"])
        d["w_inter"] = jnp.exp(inter - d["m_row"])

    def emit(unit):
        c, h = unit
        d = st[unit]
        tok = slice(c * L, (c + 1) * L)
        tot = _dot(d["v"], (d["s"] * d["w_intra"]).astype(BF16)) + d["w_inter"] * d["carry"]
        num = tot[:M_V_DIM]
        nq = tot[M_V_DIM:M_V_DIM + 1]
        h_out = num * (1.0 / jnp.maximum(jnp.abs(nq), jnp.exp(-d["m_row"])))
        h_out = h_out * lax.rsqrt(jnp.mean(h_out * h_out, axis=0, keepdims=True) + EPS)
        rows = slice(h * M_V_DIM, (h + 1) * M_V_DIM)
        yt_ref[rows, tok] = (ot_ref[rows, tok] * (h_out * nwt_ref[rows, :])).astype(BF16)

    def update(unit):
        c, h = unit
        d = st[unit]
        b_last = d["b"][:, L - 1:L]
        dec = b_last + (gt_ref[h:h + 1, c * L:(c + 1) * L] - d["b"])
        m_new = jnp.maximum(b_last + d["m_prev"], jnp.max(dec, axis=-1, keepdims=True))
        wk = jnp.exp(dec - m_new)
        keep = jnp.exp(b_last + d["m_prev"] - m_new)
        v_scaled = (d["v"].astype(F32) * wk).astype(BF16)
        c_sc[h] = keep * d["state"] + _dot(v_scaled, d["k"])
        m_sc[h] = jnp.broadcast_to(m_new, (1, LANES))
        d.clear()

    n = len(units)
    score(units[0])
    for i in range(n + 1):
        if i + 1 < n:
            score(units[i + 1])
        if i < n:
            gate(units[i])
            emit(units[i])
        if i >= 1:
            update(units[i - 1])


def _mlstm_core(qt, k, vt, ot, src, gt, nwt, *, batch, seq):
    L = M_CHUNK
    W = M_CHUNKS_PER_STEP * L
    assert seq % W == 0
    nb = seq // W
    cur = lambda b, i: (b * nb + i, 0)
    cur_t = lambda b, i: (0, b * nb + i)
    return pl.pallas_call(
        _mlstm_core_kernel,
        grid=(batch, nb),
        in_specs=[pl.BlockSpec((M_QK_COLS, W), cur_t), pl.BlockSpec((W, M_QK_COLS), cur),
                  pl.BlockSpec((M_V_COLS, W), cur_t), pl.BlockSpec((M_V_COLS, W), cur_t),
                  pl.BlockSpec((W, LANES), cur), pl.BlockSpec((2 * M_HEADS, W), cur_t), _resident((M_V_COLS, L))],
        out_specs=pl.BlockSpec((M_V_COLS, W), cur_t),
        out_shape=jax.ShapeDtypeStruct((M_V_COLS, batch * seq), BF16),
        scratch_shapes=[pltpu.VMEM((M_HEADS, M_STATE_ROWS, LANES), F32),
                        pltpu.VMEM((M_HEADS, 1, LANES), F32)],
        compiler_params=_params(2),
        name="mlstm_core",
    )(qt, k, vt, ot, src, gt, nwt)


R_QK_COLS = R_HEADS * R_QK_DIM
R_V_COLS = R_HEADS * R_V_DIM
R_PROJ_COLS = R_QK_COLS + 2 * R_V_COLS
R_PROJ_CHUNK = 512


def _ret_proj_kernel(x_ref, nw_ref, w_ref, wkt_ref, cos_ref, sin_ref, cost_ref, sint_ref, qdec_ref, kdec_ref,
                     q_ref, kt_ref, v_ref, g_ref):
    sub = x_ref.shape[0] // ROW_SPLIT
    half = R_QK_DIM // 2
    for split in range(ROW_SPLIT):
        tok = slice(split * sub, (split + 1) * sub)
        h = _rms(x_ref[tok, :], nw_ref[...]).astype(BF16)
        cos = cos_ref[tok, :]
        sin = sin_ref[tok, :]
        for c in range(R_QK_COLS // R_PROJ_CHUNK):
            lo = c * R_PROJ_CHUNK
            t = _dot(h, w_ref[:, lo:lo + R_PROJ_CHUNK])
            for hh in range(R_PROJ_CHUNK // R_QK_DIM):
                head = lo // R_QK_DIM + hh
                dec = qdec_ref[head, tok, :]
                x1 = t[:, hh * R_QK_DIM:hh * R_QK_DIM + half]
                x2 = t[:, hh * R_QK_DIM + half:(hh + 1) * R_QK_DIM]
                base = head * R_QK_DIM
                q_ref[tok, base:base + half] = ((x1 * cos - x2 * sin) * dec).astype(BF16)
                q_ref[tok, base + half:base + R_QK_DIM] = ((x2 * cos + x1 * sin) * dec).astype(BF16)
        for c in range(2 * R_V_COLS // R_PROJ_CHUNK):
            lo = c * R_PROJ_CHUNK
            t = _dot(h, w_ref[:, R_QK_COLS + lo:R_QK_COLS + lo + R_PROJ_CHUNK])
            if lo < R_V_COLS:
                v_ref[tok, lo:lo + R_PROJ_CHUNK] = t.astype(BF16)
            else:
                g_lo = lo - R_V_COLS
                g_ref[tok, g_lo:g_lo + R_PROJ_CHUNK] = (t * jax.nn.sigmoid(t)).astype(BF16)
        cos_t = cost_ref[:, tok]
        sin_t = sint_ref[:, tok]
        for head in range(R_HEADS):
            base = head * R_QK_DIM
            kt = _dot_nt(wkt_ref[base:base + R_QK_DIM, :], h)
            dec = kdec_ref[head:head + 1, tok]
            x1 = kt[:half]
            x2 = kt[half:]
            kt_ref[base:base + half, tok] = ((x1 * cos_t - x2 * sin_t) * dec).astype(BF16)
            kt_ref[base + half:base + R_QK_DIM, tok] = ((x2 * cos_t + x1 * sin_t) * dec).astype(BF16)


def _ret_proj(x, nw, w, wkt, cos_r, sin_r, cos_rt, sin_rt, qdec, kdec, *, tm):
    t = x.shape[0]
    return pl.pallas_call(
        _ret_proj_kernel,
        grid=(t // tm,),
        in_specs=[_rows(tm, D_MODEL), _resident((1, D_MODEL)), _resident((D_MODEL, R_PROJ_COLS)),
                  _resident((R_QK_COLS, D_MODEL)),
                  _rows(tm, LANES), _rows(tm, LANES), _cols(LANES, tm), _cols(LANES, tm),
                  _resident((R_HEADS, tm, LANES)), _resident((SUBLANES, tm))],
        out_specs=[_rows(tm, R_QK_COLS), _cols(R_QK_COLS, tm), _rows(tm, R_V_COLS), _rows(tm, R_V_COLS)],
        out_shape=[jax.ShapeDtypeStruct((t, R_QK_COLS), BF16),
                   jax.ShapeDtypeStruct((R_QK_COLS, t), BF16),
                   jax.ShapeDtypeStruct((t, R_V_COLS), BF16),
                   jax.ShapeDtypeStruct((t, R_V_COLS), BF16)],
        compiler_params=_params(1),
        name="ret_proj",
    )(x, nw, w, wkt, cos_r, sin_r, cos_rt, sin_rt, qdec, kdec)


def _ret_core_kernel(cdec_ref, q_ref, kt_ref, v_ref, g_ref, nw_ref, y_ref, s_sc):
    L = R_CHUNK

    @pl.when(pl.program_id(1) == 0)
    def _():
        s_sc[...] = jnp.zeros_like(s_sc)

    row = lax.broadcasted_iota(jnp.int32, (L, L), 0)
    col = lax.broadcasted_iota(jnp.int32, (L, L), 1)
    causal = row >= col
    nw = nw_ref[...]
    units = [(c, h) for c in range(R_CHUNKS_PER_STEP) for h in range(R_HEADS)]
    st = {unit: dict() for unit in units}

    def score(unit):
        c, h = unit
        d = st[unit]
        tok = slice(c * L, (c + 1) * L)
        d["q"] = q_ref[tok, h * R_QK_DIM:(h + 1) * R_QK_DIM]
        d["kt"] = kt_ref[h * R_QK_DIM:(h + 1) * R_QK_DIM, tok]
        d["v"] = v_ref[tok, h * R_V_DIM:(h + 1) * R_V_DIM]
        d["state"] = s_sc[h]
        d["scores"] = _dot(d["q"], d["kt"])
        d["carry"] = _dot(d["q"], d["state"].astype(BF16))

    def emit(unit):
        c, h = unit
        d = st[unit]
        tok = slice(c * L, (c + 1) * L)
        scores = jnp.where(causal, d["scores"], 0.0)
        y = _dot(scores.astype(BF16), d["v"]) + d["carry"]
        s_sc[h] = cdec_ref[h] * (d["state"] + _dot(d["kt"], d["v"]))
        d.clear()
        mu = jnp.mean(y, axis=-1, keepdims=True)
        yc = y - mu
        yn = yc * lax.rsqrt(jnp.mean(yc * yc, axis=-1, keepdims=True) + EPS)
        gate = g_ref[tok, h * R_V_DIM:(h + 1) * R_V_DIM]
        out = gate * (yn * nw[:, h * R_V_DIM:(h + 1) * R_V_DIM])
        y_ref[tok, h * R_V_DIM:(h + 1) * R_V_DIM] = out.astype(BF16)

    score(units[0])
    for i, unit in enumerate(units):
        if i + 1 < len(units):
            score(units[i + 1])
        emit(unit)


def _ret_core(q, kt, v, g, nw, cdec, *, batch, seq):
    L = R_CHUNK
    W = R_CHUNKS_PER_STEP * L
    assert seq % W == 0
    nb = seq // W
    cur = lambda b, i: (b * nb + i, 0)
    cur_t = lambda b, i: (0, b * nb + i)
    return pl.pallas_call(
        _ret_core_kernel,
        grid=(batch, nb),
        in_specs=[pl.BlockSpec(memory_space=pltpu.SMEM),
                  pl.BlockSpec((W, R_QK_COLS), cur), pl.BlockSpec((R_QK_COLS, W), cur_t),
                  pl.BlockSpec((W, R_V_COLS), cur), pl.BlockSpec((W, R_V_COLS), cur),
                  _resident((1, R_V_COLS))],
        out_specs=pl.BlockSpec((W, R_V_COLS), cur),
        out_shape=jax.ShapeDtypeStruct((batch * seq, R_V_COLS), BF16),
        scratch_shapes=[pltpu.VMEM((R_HEADS, R_QK_DIM, R_V_DIM), F32)],
        compiler_params=_params(2),
        name="ret_core",
    )(cdec, q, kt, v, g, nw)


def _rope_constants(tm):
    inv_a = 1.0 / (ROPE_THETA ** (jnp.arange(0, A_ROPE_DIM, 2, dtype=F32) / A_ROPE_DIM))
    inv_r = 1.0 / (R_THETA ** jnp.linspace(0.0, 1.0, R_QK_DIM // 2, dtype=F32))
    return (jnp.broadcast_to(inv_a[:, None], (A_ROPE_DIM // 2, tm)),
            jnp.broadcast_to(inv_r[:, None], (R_QK_DIM // 2, tm)))


def _retention_constants(tm):
    L = R_CHUNK
    log_gamma = jnp.log(1.0 - 2.0 ** (-5.0 - jnp.arange(R_HEADS, dtype=F32)))
    j = jnp.asarray(np.arange(tm) % L, F32)
    q_decay = jnp.exp((j + 1.0) * log_gamma[:, None])
    k_decay = jnp.exp(-(j + 1.0) * log_gamma[:, None]) * (R_QK_DIM ** -0.5)
    chunk_decay = jnp.exp(L * log_gamma)
    qdec = jnp.broadcast_to(q_decay[:, :, None], (R_HEADS, tm, LANES))
    kdec = jnp.pad(k_decay, ((0, SUBLANES - R_HEADS), (0, 0)))
    return qdec, kdec, chunk_decay


def _attn_weight_layout(w_qkv, b_qkv, tm):
    nq = A_Q_ROWS
    qv = np.concatenate([np.arange(nq), np.arange(nq + A_KV_DIM, nq + 2 * A_KV_DIM)])
    wt = w_qkv[:, qv].T.astype(BF16)
    bt = jnp.broadcast_to(b_qkv[qv][:, None], (nq + A_KV_DIM, tm))
    wk = w_qkv[:, nq:nq + A_KV_DIM].astype(BF16)
    bk = b_qkv[nq:nq + A_KV_DIM].reshape(1, A_KV_DIM)
    return wt, bt, wk, bk


def _ret_weight_layout(w_in):
    head = np.concatenate([np.arange(0, R_QK_DIM, 2), np.arange(1, R_QK_DIM, 2)])
    perm = np.concatenate([h * R_QK_DIM + head for h in range(R_HEADS)])
    w = jnp.concatenate([w_in[:, perm], w_in[:, 2 * R_QK_COLS:]], axis=-1).astype(BF16)
    wkt = w_in[:, R_QK_COLS + perm].T.astype(BF16)
    return w, wkt


def _mlstm_weight_layout(w_in, b_if):
    pad = LANES - 2 * M_HEADS
    gate_lo = 2 * M_QK_COLS + 2 * M_V_COLS
    wt = jnp.concatenate([w_in[:, :M_QK_COLS], w_in[:, 2 * M_QK_COLS:gate_lo]], axis=-1).T.astype(BF16)
    wk = w_in[:, M_QK_COLS:2 * M_QK_COLS].astype(BF16)
    wg = jnp.pad(w_in[:, gate_lo:], ((0, 0), (0, pad))).astype(BF16)
    return wt, wk, wg, jnp.pad(b_if, (0, pad)).reshape(1, LANES)


def kernel(x, positions, norm_w, final_norm_w, ffn_w_in, ffn_w_out, attn_w_qkv, attn_b_qkv, attn_w_o, attn_b_o,
           attn_sinks, mlstm_w_in, mlstm_b_if, mlstm_norm_w, mlstm_w_out, ret_w_in, ret_norm_w, ret_w_out):
    batch, seq, d = x.shape
    t = batch * seq
    tm = min(ROW_TILE, t)
    assert d == D_MODEL and t % tm == 0 and (tm // ROW_SPLIT) % R_CHUNK == 0
    assert seq % R_CHUNK == 0 and seq % M_CHUNK == 0 and seq % (A_CHUNK * A_BLOCKS_PER_STEP) == 0

    xf = x.reshape(t, d)
    inv_a, inv_r = _rope_constants(tm)
    cos_a, sin_a, cos_r, sin_r, cos_at, sin_at, cos_rt, sin_rt = _rope_tables(
        positions.astype(F32).reshape(1, t), inv_a, inv_r, tm)
    qdec, kdec, cdec = _retention_constants(tm)
    zero_bias = jnp.zeros((1, D_MODEL), F32)
    norm_w4 = norm_w.reshape(DEPTH, 3, 1, d)
    w_in_bf = ffn_w_in.astype(BF16)
    w_out_bf = ffn_w_out.astype(BF16)

    for i in range(DEPTH):
        kind, slot = i % N_MIXERS, i // N_MIXERS
        xf = _ffn(xf, norm_w4, w_in_bf, w_out_bf, i, 0, tm=tm)
        nw = norm_w[i, 1].reshape(1, d)
        if kind == 0:
            wt, bt, wk, bk = _attn_weight_layout(attn_w_qkv[slot], attn_b_qkv[slot], tm)
            qt, k, vt = _attn_proj(xf, nw, wt, bt, wk, bk, cos_a, sin_a, cos_at, sin_at, tm=tm)
            yt = _attn_core(qt, k, vt, attn_sinks[slot], batch=batch, seq=seq)
            mix = (yt, attn_w_o[slot].astype(BF16), attn_b_o[slot].reshape(1, d), True)
        elif kind == 1:
            wt, wk, wg, bg = _mlstm_weight_layout(mlstm_w_in[slot], mlstm_b_if[slot])
            qt, k, vt, ot, src, gt = _mlstm_proj(xf, nw, wt, wk, wg, bg, tm=tm)
            nwt = jnp.broadcast_to(mlstm_norm_w[slot][:, None], (M_V_COLS, M_CHUNK))
            yt = _mlstm_core(qt, k, vt, ot, src, gt, nwt, batch=batch, seq=seq)
            mix = (yt, mlstm_w_out[slot].astype(BF16), zero_bias, True)
        else:
            w, wkt = _ret_weight_layout(ret_w_in[slot])
            q, kt, v, g = _ret_proj(xf, nw, w, wkt, cos_r, sin_r, cos_rt, sin_rt, qdec, kdec, tm=tm)
            y = _ret_core(q, kt, v, g, ret_norm_w[slot].reshape(1, R_V_COLS), cdec, batch=batch, seq=seq)
            mix = (y, ret_w_out[slot].astype(BF16), zero_bias, False)
        last = i == DEPTH - 1
        xf = _ffn(xf, norm_w4, w_in_bf, w_out_bf, i, 1, tm=tm, mix=mix,
                  final_w=final_norm_w.reshape(1, d) if last else None)
    return xf.reshape(batch, seq, d)
```

```python
import functools

import jax
import jax.numpy as jnp
import numpy as np
from jax import lax
from jax.experimental import pallas as pl
from jax.experimental.pallas import tpu as pltpu

F32 = jnp.float32
BF16 = jnp.bfloat16

D_MODEL = 1024
DEPTH = 4
N_MIXERS = 3
EPS = 1e-6
D_FF = 2816
A_HEADS = 16
A_KV_HEADS = 4
A_GROUP = A_HEADS // A_KV_HEADS
A_HEAD_DIM = 64
A_WINDOW = 128
A_ROPE_DIM = 16
ROPE_THETA = 500000.0
M_HEADS = 8
M_QK_DIM = 64
M_V_DIM = 128
M_GATE_CAP = 15.0
R_HEADS = 4
R_QK_DIM = 256
R_V_DIM = 512
R_THETA = 10000.0

LANES = 128
SUBLANES = 8
BF16_ROWS = 16
VMEM_LIMIT = 56 * 1024 * 1024

ROW_TILE = 1024
ROW_SPLIT = 2
FF_CHUNK = 256
A_CHUNK = A_WINDOW
A_BLOCKS_PER_STEP = 8
M_CHUNK = 128
M_CHUNKS_PER_STEP = 8
R_CHUNK = 256
R_CHUNKS_PER_STEP = 4


def _params(n_axes):
    return pltpu.CompilerParams(
        dimension_semantics=("arbitrary",) * n_axes,
        vmem_limit_bytes=VMEM_LIMIT,
    )


def _resident(shape):
    nd = len(shape)
    return pl.BlockSpec(shape, lambda *_: (0,) * nd, pipeline_mode=pl.Buffered(1))


def _resident_slice(lead, shape):
    nd = len(shape)
    return pl.BlockSpec((None,) * len(lead) + tuple(shape), lambda *_: tuple(lead) + (0,) * nd,
                        pipeline_mode=pl.Buffered(1))


def _rows(tm, width):
    return pl.BlockSpec((tm, width), lambda i: (i, 0))


def _cols(height, tm):
    return pl.BlockSpec((height, tm), lambda i: (0, i))


def _rms(x, w):
    ms = jnp.mean(x * x, axis=-1, keepdims=True)
    return x * lax.rsqrt(ms + EPS) * w


def _dot(a, b):
    return jnp.dot(a, b, preferred_element_type=F32)


def _dot_nt(a, b):
    return lax.dot_general(a, b, (((1,), (1,)), ((), ())), preferred_element_type=F32)


def _dot_tn(a, b):
    return lax.dot_general(a, b, (((0,), (0,)), ((), ())), preferred_element_type=F32)


def _tables_kernel(pos_ref, inv_a_ref, inv_r_ref, cos_a, sin_a, cos_r, sin_r, cos_at, sin_at, cos_rt, sin_rt):
    tm = pos_ref.shape[1]
    pos = pos_ref[...]
    ang_a = inv_a_ref[...] * pos
    ca = jnp.cos(ang_a)
    sa = jnp.sin(ang_a)
    cos_at[...] = ca
    sin_at[...] = sa
    rest = A_HEAD_DIM - A_ROPE_DIM
    cos_head = jnp.concatenate([ca, ca, jnp.ones((rest, tm), F32)], axis=0)
    sin_head = jnp.concatenate([-sa, sa, jnp.zeros((rest, tm), F32)], axis=0)
    cos_a[...] = jnp.concatenate([cos_head] * (LANES // A_HEAD_DIM), axis=0).T
    sin_a[...] = jnp.concatenate([sin_head] * (LANES // A_HEAD_DIM), axis=0).T
    ang_r = inv_r_ref[...] * pos
    cr = jnp.cos(ang_r)
    sr = jnp.sin(ang_r)
    cos_rt[...] = cr
    sin_rt[...] = sr
    cos_r[...] = cr.T
    sin_r[...] = sr.T


def _rope_tables(pos_row, inv_a, inv_r, tm):
    t = pos_row.shape[1]
    half = A_ROPE_DIM // 2
    tok = jax.ShapeDtypeStruct((t, LANES), F32)
    return pl.pallas_call(
        _tables_kernel,
        grid=(t // tm,),
        in_specs=[_cols(1, tm), _resident((half, tm)), _resident((LANES, tm))],
        out_specs=[_rows(tm, LANES)] * 4 + [_cols(half, tm), _cols(half, tm), _cols(LANES, tm), _cols(LANES, tm)],
        out_shape=[tok] * 4 + [jax.ShapeDtypeStruct((half, t), F32)] * 2 + [jax.ShapeDtypeStruct((LANES, t), F32)] * 2,
        compiler_params=_params(1),
        name="rope_tables",
    )(pos_row, inv_a, inv_r)


def _ffn_kernel(*refs, mix_in, mix_feature_major, final_norm):
    refs = list(refs)
    x_ref = refs.pop(0)
    if mix_in:
        y_ref, wo_ref, bo_ref = refs.pop(0), refs.pop(0), refs.pop(0)
    nw_ref, win_ref, wout_ref = refs.pop(0), refs.pop(0), refs.pop(0)
    fw_ref = refs.pop(0) if final_norm else None
    (o_ref,) = refs

    tm = x_ref.shape[0]
    sub = tm // ROW_SPLIT
    for part in range(ROW_SPLIT):
        rows = slice(part * sub, (part + 1) * sub)
        x = x_ref[rows, :]
        if mix_in:
            if mix_feature_major:
                mixed = _dot_tn(y_ref[:, rows], wo_ref[...])
            else:
                mixed = _dot(y_ref[rows, :], wo_ref[...])
            x = x + (mixed + bo_ref[...])
        h = _rms(x, nw_ref[...]).astype(BF16)
        acc = jnp.zeros_like(x)
        for c in range(D_FF // FF_CHUNK):
            lo = c * FF_CHUNK
            g = _dot(h, win_ref[:, lo:lo + FF_CHUNK])
            u = _dot(h, win_ref[:, D_FF + lo:D_FF + lo + FF_CHUNK])
            a = (g * jax.nn.sigmoid(g) * u).astype(BF16)
            acc = acc + _dot(a, wout_ref[lo:lo + FF_CHUNK, :])
        y = x + 0.5 * acc
        if final_norm:
            y = _rms(y, fw_ref[...])
        o_ref[rows, :] = y


def _ffn(x, norm_w, w_in, w_out, layer, which, *, tm, mix=None, final_w=None):
    t = x.shape[0]
    in_specs = [_rows(tm, D_MODEL)]
    args = [x]
    feature_major = False
    if mix is not None:
        y, w_o, b_o, feature_major = mix
        kdim = w_o.shape[0]
        assert y.shape == ((kdim, t) if feature_major else (t, kdim))
        in_specs += [_cols(kdim, tm) if feature_major else _rows(tm, kdim),
                     _resident((kdim, D_MODEL)), _resident((1, D_MODEL))]
        args += [y, w_o, b_o]
    in_specs += [_resident_slice((layer, 2 * which), (1, D_MODEL)),
                 _resident_slice((layer, which), (D_MODEL, 2 * D_FF)),
                 _resident_slice((layer, which), (D_FF, D_MODEL))]
    args += [norm_w, w_in, w_out]
    if final_w is not None:
        in_specs.append(_resident((1, D_MODEL)))
        args.append(final_w)
    return pl.pallas_call(
        functools.partial(_ffn_kernel, mix_in=mix is not None, mix_feature_major=feature_major,
                          final_norm=final_w is not None),
        grid=(t // tm,),
        in_specs=in_specs,
        out_specs=_rows(tm, D_MODEL),
        out_shape=jax.ShapeDtypeStruct((t, D_MODEL), F32),
        compiler_params=_params(1),
        name="ffn_mix" if mix is not None else "ffn",
    )(*args)


A_Q_ROWS = A_HEADS * A_HEAD_DIM
A_KV_DIM = A_KV_HEADS * A_HEAD_DIM
A_PROJ_ROWS = 256


def _attn_proj_kernel(x_ref, nw_ref, wt_ref, bt_ref, wk_ref, bk_ref, cos_ref, sin_ref, cost_ref, sint_ref,
                      qt_ref, k_ref, vt_ref):
    sub = x_ref.shape[0] // ROW_SPLIT
    half = A_ROPE_DIM // 2
    lane = lax.broadcasted_iota(jnp.int32, (sub, LANES), 1) & (A_HEAD_DIM - 1)
    upper = (lane >= half) & (lane < A_ROPE_DIM)
    q_scale = A_HEAD_DIM ** -0.5
    for split in range(ROW_SPLIT):
        tok = slice(split * sub, (split + 1) * sub)
        h = _rms(x_ref[tok, :], nw_ref[...]).astype(BF16)

        cos = cos_ref[tok, :]
        sin = sin_ref[tok, :]
        k = _dot(h, wk_ref[...]) + bk_ref[...]
        for j in range(A_KV_DIM // LANES):
            piece = k[:, j * LANES:(j + 1) * LANES]
            partner = jnp.where(upper, pltpu.roll(piece, half, 1), pltpu.roll(piece, LANES - half, 1))
            k_ref[tok, j * LANES:(j + 1) * LANES] = (piece * cos + partner * sin).astype(BF16)

        cos_t = cost_ref[:, tok]
        sin_t = sint_ref[:, tok]
        for c in range((A_Q_ROWS + A_KV_DIM) // A_PROJ_ROWS):
            lo = c * A_PROJ_ROWS
            part = _dot_nt(wt_ref[lo:lo + A_PROJ_ROWS, :], h) + bt_ref[lo:lo + A_PROJ_ROWS, tok]
            if lo < A_Q_ROWS:
                for j in range(A_PROJ_ROWS // A_HEAD_DIM):
                    base = j * A_HEAD_DIM
                    x1 = part[base:base + half]
                    x2 = part[base + half:base + 2 * half]
                    rot = jnp.concatenate([x1 * cos_t - x2 * sin_t, x2 * cos_t + x1 * sin_t,
                                           part[base + 2 * half:base + A_HEAD_DIM]], axis=0)
                    qt_ref[lo + base:lo + base + A_HEAD_DIM, tok] = (rot * q_scale).astype(BF16)
            else:
                vt_ref[:, tok] = part.astype(BF16)


def _attn_proj(x, nw, wt, bt, wk, bk, cos_a, sin_a, cos_at, sin_at, *, tm):
    t = x.shape[0]
    half = A_ROPE_DIM // 2
    rows = A_Q_ROWS + A_KV_DIM
    return pl.pallas_call(
        _attn_proj_kernel,
        grid=(t // tm,),
        in_specs=[_rows(tm, D_MODEL), _resident((1, D_MODEL)), _resident((rows, D_MODEL)), _resident((rows, tm)),
                  _resident((D_MODEL, A_KV_DIM)), _resident((1, A_KV_DIM)),
                  _rows(tm, LANES), _rows(tm, LANES), _cols(half, tm), _cols(half, tm)],
        out_specs=[_cols(A_Q_ROWS, tm), _rows(tm, A_KV_DIM), _cols(A_KV_DIM, tm)],
        out_shape=[jax.ShapeDtypeStruct((A_Q_ROWS, t), BF16),
                   jax.ShapeDtypeStruct((t, A_KV_DIM), BF16),
                   jax.ShapeDtypeStruct((A_KV_DIM, t), BF16)],
        compiler_params=_params(1),
        name="attn_proj",
    )(x, nw, wt, bt, wk, bk, cos_a, sin_a, cos_at, sin_at)


def _attn_core_kernel(sink_ref, qt_ref, kp_ref, kc_ref, vtp_ref, vtc_ref, ot_ref):
    L = A_CHUNK
    step = pl.program_id(1)
    key = lax.broadcasted_iota(jnp.int32, (L, L), 0)
    qry = lax.broadcasted_iota(jnp.int32, (L, L), 1)
    from_prev = key > qry
    first_bias = jnp.where(step > 0, 0.0, -jnp.inf).astype(F32)
    zero_q = jnp.zeros((A_HEAD_DIM, A_GROUP * L), BF16)
    units = [(u, g) for u in range(A_BLOCKS_PER_STEP) for g in range(A_KV_HEADS)]
    scores, probs_of = {}, {}

    def score(unit):
        u, g = unit
        col = slice((g // 2) * LANES, (g // 2 + 1) * LANES)
        if u == 0:
            kk = jnp.concatenate([kp_ref[:, col], kc_ref[:L, col]], axis=0)
        else:
            kk = kc_ref[(u - 1) * L:(u + 1) * L, col]
        q_cat = jnp.concatenate([qt_ref[(A_GROUP * g + j) * A_HEAD_DIM:(A_GROUP * g + j + 1) * A_HEAD_DIM,
                                        u * L:(u + 1) * L] for j in range(A_GROUP)], axis=1)
        q_ext = jnp.concatenate([q_cat, zero_q] if g % 2 == 0 else [zero_q, q_cat], axis=0)
        scores[unit] = _dot(kk, q_ext)

    def softmax(unit):
        u, g = unit
        s = scores.pop(unit)
        probs = []
        for j in range(A_GROUP):
            sink = sink_ref[A_GROUP * g + j]
            s_prev = s[:L, j * L:(j + 1) * L]
            if u == 0:
                s_prev = s_prev + first_bias
            s_own = s[L:, j * L:(j + 1) * L]
            sc = jnp.where(from_prev, s_prev, s_own)
            mx = jnp.maximum(jnp.max(sc, axis=0, keepdims=True), sink)
            p = jnp.exp(sc - mx)
            denom = jnp.sum(p, axis=0, keepdims=True) + jnp.exp(sink - mx)
            pn = p * (1.0 / denom)
            zero = jnp.zeros_like(pn)
            probs.append(jnp.concatenate([jnp.where(from_prev, pn, zero), jnp.where(from_prev, zero, pn)],
                                         axis=0).astype(BF16))
        probs_of[unit] = jnp.concatenate(probs, axis=1)

    def weighted_values(unit):
        u, g = unit
        rows = slice(g * A_HEAD_DIM, (g + 1) * A_HEAD_DIM)
        if u == 0:
            vt = jnp.concatenate([vtp_ref[rows, :], vtc_ref[rows, :L]], axis=1)
        else:
            vt = vtc_ref[rows, (u - 1) * L:(u + 1) * L]
        out = _dot(vt, probs_of.pop(unit))
        for j in range(A_GROUP):
            head = A_GROUP * g + j
            ot_ref[head * A_HEAD_DIM:(head + 1) * A_HEAD_DIM, u * L:(u + 1) * L] = (
                out[:, j * L:(j + 1) * L].astype(BF16))

    n = len(units)
    score(units[0])
    for i in range(n + 1):
        if i + 1 < n:
            score(units[i + 1])
        if i < n:
            softmax(units[i])
        if i >= 1:
            weighted_values(units[i - 1])


def _attn_core(qt, k, vt, sinks, *, batch, seq):
    L = A_CHUNK
    W = A_BLOCKS_PER_STEP * L
    assert seq % W == 0
    ns = seq // W
    cur = lambda b, i: (b * ns + i, 0)
    cur_t = lambda b, i: (0, b * ns + i)
    prev = lambda b, i: ((b * ns + i) * A_BLOCKS_PER_STEP - jnp.minimum(i, 1), 0)
    prev_t = lambda b, i: (0, (b * ns + i) * A_BLOCKS_PER_STEP - jnp.minimum(i, 1))
    return pl.pallas_call(
        _attn_core_kernel,
        grid=(batch, ns),
        in_specs=[pl.BlockSpec(memory_space=pltpu.SMEM),
                  pl.BlockSpec((A_Q_ROWS, W), cur_t),
                  pl.BlockSpec((L, A_KV_DIM), prev), pl.BlockSpec((W, A_KV_DIM), cur),
                  pl.BlockSpec((A_KV_DIM, L), prev_t), pl.BlockSpec((A_KV_DIM, W), cur_t)],
        out_specs=pl.BlockSpec((A_Q_ROWS, W), cur_t),
        out_shape=jax.ShapeDtypeStruct((A_Q_ROWS, batch * seq), BF16),
        compiler_params=_params(2),
        name="attn_core",
    )(sinks, qt, k, k, vt, vt)


M_QK_COLS = M_HEADS * M_QK_DIM
M_V_COLS = M_HEADS * M_V_DIM
M_T_ROWS = M_QK_COLS + 2 * M_V_COLS
M_PROJ_ROWS = 512
M_STATE_ROWS = M_V_DIM + BF16_ROWS


def _mlstm_proj_kernel(x_ref, nw_ref, wt_ref, wk_ref, wg_ref, bg_ref,
                       qt_ref, k_ref, vt_ref, ot_ref, src_ref, gt_ref):
    sub = x_ref.shape[0] // ROW_SPLIT
    H = M_HEADS
    lane = lax.broadcasted_iota(jnp.int32, (sub, LANES), 1)
    time = lax.broadcasted_iota(jnp.int32, (sub, LANES), 0) & (M_CHUNK - 1)
    for split in range(ROW_SPLIT):
        tok = slice(split * sub, (split + 1) * sub)
        h = _rms(x_ref[tok, :], nw_ref[...]).astype(BF16)
        k_ref[tok, :] = _dot(h, wk_ref[...]).astype(BF16)
        for c in range(M_T_ROWS // M_PROJ_ROWS):
            lo = c * M_PROJ_ROWS
            part = _dot_nt(wt_ref[lo:lo + M_PROJ_ROWS, :], h)
            if lo < M_QK_COLS:
                qt_ref[:, tok] = (part * (M_QK_DIM ** -0.5)).astype(BF16)
            elif lo < M_QK_COLS + M_V_COLS:
                vt_ref[lo - M_QK_COLS:lo - M_QK_COLS + M_PROJ_ROWS, tok] = part.astype(BF16)
            else:
                o_lo = lo - M_QK_COLS - M_V_COLS
                ot_ref[o_lo:o_lo + M_PROJ_ROWS, tok] = jax.nn.sigmoid(part)

        gates = _dot(h, wg_ref[...]) + bg_ref[...]
        gates = M_GATE_CAP * jnp.tanh(gates / M_GATE_CAP)
        b = jnp.where((lane >= H) & (lane < 2 * H), jax.nn.log_sigmoid(gates), 0.0)
        shift = 1
        while shift < M_CHUNK:
            b = b + jnp.where(time >= shift, pltpu.roll(b, shift, 0), 0.0)
            shift *= 2
        src_ref[tok, :] = gates - pltpu.roll(b, LANES - H, 1)
        gt_ref[:, tok] = jnp.where(lane < H, gates, b).T[:2 * H]


def _mlstm_proj(x, nw, wt, wk, wg, bg, *, tm):
    t = x.shape[0]
    assert (tm // ROW_SPLIT) % M_CHUNK == 0 and M_CHUNK & (M_CHUNK - 1) == 0
    return pl.pallas_call(
        _mlstm_proj_kernel,
        grid=(t // tm,),
        in_specs=[_rows(tm, D_MODEL), _resident((1, D_MODEL)), _resident((M_T_ROWS, D_MODEL)),
                  _resident((D_MODEL, M_QK_COLS)), _resident((D_MODEL, LANES)), _resident((1, LANES))],
        out_specs=[_cols(M_QK_COLS, tm), _rows(tm, M_QK_COLS), _cols(M_V_COLS, tm),
                   _cols(M_V_COLS, tm), _rows(tm, LANES), _cols(2 * M_HEADS, tm)],
        out_shape=[jax.ShapeDtypeStruct((M_QK_COLS, t), BF16),
                   jax.ShapeDtypeStruct((t, M_QK_COLS), BF16),
                   jax.ShapeDtypeStruct((M_V_COLS, t), BF16),
                   jax.ShapeDtypeStruct((M_V_COLS, t), F32),
                   jax.ShapeDtypeStruct((t, LANES), F32),
                   jax.ShapeDtypeStruct((2 * M_HEADS, t), F32)],
        compiler_params=_params(1),
        name="mlstm_proj",
    )(x, nw, wt, wk, wg, bg)


def _mlstm_core_kernel(qt_ref, k_ref, vt_ref, ot_ref, src_ref, gt_ref, nwt_ref, yt_ref, c_sc, m_sc):
    L = M_CHUNK
    H = M_HEADS

    @pl.when(pl.program_id(1) == 0)
    def _():
        c_sc[...] = jnp.zeros_like(c_sc)
        m_sc[...] = jnp.zeros_like(m_sc)

    src_i = lax.broadcasted_iota(jnp.int32, (L, L), 0)
    tgt_i = lax.broadcasted_iota(jnp.int32, (L, L), 1)
    causal = src_i <= tgt_i
    ones_rows = (lax.broadcasted_iota(jnp.int32, (BF16_ROWS, L), 0) == 0).astype(BF16)
    zero_q = jnp.zeros((M_QK_DIM, L), BF16)
    low = lax.broadcasted_iota(jnp.int32, (L, LANES), 1) < M_QK_DIM

    units = [(c, h) for c in range(M_CHUNKS_PER_STEP) for h in range(H)]
    st = {unit: dict() for unit in units}

    def score(unit):
        c, h = unit
        d = st[unit]
        tok = slice(c * L, (c + 1) * L)
        pair = h // 2
        k_pair = k_ref[tok, pair * LANES:(pair + 1) * LANES]
        d["k"] = jnp.where(low if h % 2 == 0 else jnp.logical_not(low), k_pair, jnp.zeros_like(k_pair))
        qt_h = qt_ref[h * M_QK_DIM:(h + 1) * M_QK_DIM, tok]
        d["qt"] = jnp.concatenate([qt_h, zero_q] if h % 2 == 0 else [zero_q, qt_h], axis=0)
        d["v"] = jnp.concatenate([vt_ref[h * M_V_DIM:(h + 1) * M_V_DIM, tok], ones_rows], axis=0)
        d["state"] = c_sc[h]
        d["s"] = _dot(d["k"], d["qt"])
        d["carry"] = _dot(d["state"].astype(BF16), d["qt"])

    def gate(unit):
        c, h = unit
        d = st[unit]
        tok = slice(c * L, (c + 1) * L)
        src_col = src_ref[tok, h:h + 1]
        d["b"] = gt_ref[H + h:H + h + 1, tok]
        d["m_prev"] = m_sc[h][:, :1]
        dmat = jnp.where(causal, src_col + d["b"], -jnp.inf)
        inter = d["b"] + d["m_prev"]
        d["m_row"] = jnp.maximum(inter, jnp.max(dmat, axis=0, keepdims=True))
        d["w_intra"] = jnp.exp(dmat - d["m_row"])
        d["w_inter"] = jnp.exp(inter - d["m_row"])

    def emit(unit):
        c, h = unit
        d = st[unit]
        tok = slice(c * L, (c + 1) * L)
        tot = _dot(d["v"], (d["s"] * d["w_intra"]).astype(BF16)) + d["w_inter"] * d["carry"]
        num = tot[:M_V_DIM]
        nq = tot[M_V_DIM:M_V_DIM + 1]
        h_out = num * (1.0 / jnp.maximum(jnp.abs(nq), jnp.exp(-d["m_row"])))
        h_out = h_out * lax.rsqrt(jnp.mean(h_out * h_out, axis=0, keepdims=True) + EPS)
        rows = slice(h * M_V_DIM, (h + 1) * M_V_DIM)
        yt_ref[rows, tok] = (ot_ref[rows, tok] * (h_out * nwt_ref[rows, :])).astype(BF16)

    def update(unit):
        c, h = unit
        d = st[unit]
        b_last = d["b"][:, L - 1:L]
        dec = b_last + (gt_ref[h:h + 1, c * L:(c + 1) * L] - d["b"])
        m_new = jnp.maximum(b_last + d["m_prev"], jnp.max(dec, axis=-1, keepdims=True))
        wk = jnp.exp(dec - m_new)
        keep = jnp.exp(b_last + d["m_prev"] - m_new)
        v_scaled = (d["v"].astype(F32) * wk).astype(BF16)
        c_sc[h] = keep * d["state"] + _dot(v_scaled, d["k"])
        m_sc[h] = jnp.broadcast_to(m_new, (1, LANES))
        d.clear()

    n = len(units)
    score(units[0])
    for i in range(n + 1):
        if i + 1 < n:
            score(units[i + 1])
        if i < n:
            gate(units[i])
            emit(units[i])
        if i >= 1:
            update(units[i - 1])


def _mlstm_core(qt, k, vt, ot, src, gt, nwt, *, batch, seq):
    L = M_CHUNK
    W = M_CHUNKS_PER_STEP * L
    assert seq % W == 0
    nb = seq // W
    cur = lambda b, i: (b * nb + i, 0)
    cur_t = lambda b, i: (0, b * nb + i)
    return pl.pallas_call(
        _mlstm_core_kernel,
        grid=(batch, nb),
        in_specs=[pl.BlockSpec((M_QK_COLS, W), cur_t), pl.BlockSpec((W, M_QK_COLS), cur),
                  pl.BlockSpec((M_V_COLS, W), cur_t), pl.BlockSpec((M_V_COLS, W), cur_t),
                  pl.BlockSpec((W, LANES), cur), pl.BlockSpec((2 * M_HEADS, W), cur_t), _resident((M_V_COLS, L))],
        out_specs=pl.BlockSpec((M_V_COLS, W), cur_t),
        out_shape=jax.ShapeDtypeStruct((M_V_COLS, batch * seq), BF16),
        scratch_shapes=[pltpu.VMEM((M_HEADS, M_STATE_ROWS, LANES), F32),
                        pltpu.VMEM((M_HEADS, 1, LANES), F32)],
        compiler_params=_params(2),
        name="mlstm_core",
    )(qt, k, vt, ot, src, gt, nwt)


R_QK_COLS = R_HEADS * R_QK_DIM
R_V_COLS = R_HEADS * R_V_DIM
R_PROJ_CHUNK = 512


def _ret_proj_kernel(x_ref, nw_ref, wq_ref, wv_ref, wg_ref, wkt_ref, cos_ref, sin_ref, cost_ref, sint_ref, qdec_ref, kdec_ref,
                     q_ref, kt_ref, v_ref, g_ref):
    sub = x_ref.shape[0] // ROW_SPLIT
    half = R_QK_DIM // 2
    for split in range(ROW_SPLIT):
        tok = slice(split * sub, (split + 1) * sub)
        h = _rms(x_ref[tok, :], nw_ref[...]).astype(BF16)
        cos = cos_ref[tok, :]
        sin = sin_ref[tok, :]
        for c in range(R_QK_COLS // R_PROJ_CHUNK):
            lo = c * R_PROJ_CHUNK
            t = _dot(h, wq_ref[:, lo:lo + R_PROJ_CHUNK])
            for hh in range(R_PROJ_CHUNK // R_QK_DIM):
                head = lo // R_QK_DIM + hh
                dec = qdec_ref[head, tok, :]
                x1 = t[:, hh * R_QK_DIM:hh * R_QK_DIM + half]
                x2 = t[:, hh * R_QK_DIM + half:(hh + 1) * R_QK_DIM]
                base = head * R_QK_DIM
                q_ref[tok, base:base + half] = ((x1 * cos - x2 * sin) * dec).astype(BF16)
                q_ref[tok, base + half:base + R_QK_DIM] = ((x2 * cos + x1 * sin) * dec).astype(BF16)
        for c in range(2 * R_V_COLS // R_PROJ_CHUNK):
            lo = c * R_PROJ_CHUNK
            if lo < R_V_COLS:
                v_ref[tok, lo:lo + R_PROJ_CHUNK] = _dot(h, wv_ref[:, lo:lo + R_PROJ_CHUNK]).astype(BF16)
            else:
                g_lo = lo - R_V_COLS
                t = _dot(h, wg_ref[:, g_lo:g_lo + R_PROJ_CHUNK])
                g_ref[tok, g_lo:g_lo + R_PROJ_CHUNK] = (t * jax.nn.sigmoid(t)).astype(BF16)
        cos_t = cost_ref[:, tok]
        sin_t = sint_ref[:, tok]
        for head in range(R_HEADS):
            base = head * R_QK_DIM
            kt = _dot_nt(wkt_ref[base:base + R_QK_DIM, :], h)
            dec = kdec_ref[head:head + 1, tok]
            x1 = kt[:half]
            x2 = kt[half:]
            kt_ref[base:base + half, tok] = ((x1 * cos_t - x2 * sin_t) * dec).astype(BF16)
            kt_ref[base + half:base + R_QK_DIM, tok] = ((x2 * cos_t + x1 * sin_t) * dec).astype(BF16)


def _ret_proj(x, nw, wq, w_all, wkt, cos_r, sin_r, cos_rt, sin_rt, qdec, kdec, *, tm):
    t = x.shape[0]
    assert 2 * R_QK_COLS == R_V_COLS
    col_block = lambda j: pl.BlockSpec((D_MODEL, R_V_COLS), lambda *_: (0, j), pipeline_mode=pl.Buffered(1))
    return pl.pallas_call(
        _ret_proj_kernel,
        grid=(t // tm,),
        in_specs=[_rows(tm, D_MODEL), _resident((1, D_MODEL)), _resident((D_MODEL, R_QK_COLS)),
                  col_block(1), col_block(2), _resident((R_QK_COLS, D_MODEL)),
                  _rows(tm, LANES), _rows(tm, LANES), _cols(LANES, tm), _cols(LANES, tm),
                  _resident((R_HEADS, tm, LANES)), _resident((SUBLANES, tm))],
        out_specs=[_rows(tm, R_QK_COLS), _cols(R_QK_COLS, tm), _rows(tm, R_V_COLS), _rows(tm, R_V_COLS)],
        out_shape=[jax.ShapeDtypeStruct((t, R_QK_COLS), BF16),
                   jax.ShapeDtypeStruct((R_QK_COLS, t), BF16),
                   jax.ShapeDtypeStruct((t, R_V_COLS), BF16),
                   jax.ShapeDtypeStruct((t, R_V_COLS), BF16)],
        compiler_params=_params(1),
        name="ret_proj",
    )(x, nw, wq, w_all, w_all, wkt, cos_r, sin_r, cos_rt, sin_rt, qdec, kdec)


def _ret_core_kernel(cdec_ref, q_ref, kt_ref, v_ref, g_ref, nw_ref, y_ref, s_sc):
    L = R_CHUNK

    @pl.when(pl.program_id(1) == 0)
    def _():
        s_sc[...] = jnp.zeros_like(s_sc)

    row = lax.broadcasted_iota(jnp.int32, (L, L), 0)
    col = lax.broadcasted_iota(jnp.int32, (L, L), 1)
    causal = row >= col
    nw = nw_ref[...]
    units = [(c, h) for c in range(R_CHUNKS_PER_STEP) for h in range(R_HEADS)]
    st = {unit: dict() for unit in units}

    def score(unit):
        c, h = unit
        d = st[unit]
        tok = slice(c * L, (c + 1) * L)
        d["q"] = q_ref[tok, h * R_QK_DIM:(h + 1) * R_QK_DIM]
        d["kt"] = kt_ref[h * R_QK_DIM:(h + 1) * R_QK_DIM, tok]
        d["v"] = v_ref[tok, h * R_V_DIM:(h + 1) * R_V_DIM]
        d["state"] = s_sc[h]
        d["scores"] = _dot(d["q"], d["kt"])
        d["carry"] = _dot(d["q"], d["state"].astype(BF16))

    def emit(unit):
        c, h = unit
        d = st[unit]
        tok = slice(c * L, (c + 1) * L)
        scores = jnp.where(causal, d["scores"], 0.0)
        y = _dot(scores.astype(BF16), d["v"]) + d["carry"]
        s_sc[h] = cdec_ref[h] * (d["state"] + _dot(d["kt"], d["v"]))
        d.clear()
        mu = jnp.mean(y, axis=-1, keepdims=True)
        yc = y - mu
        yn = yc * lax.rsqrt(jnp.mean(yc * yc, axis=-1, keepdims=True) + EPS)
        gate = g_ref[tok, h * R_V_DIM:(h + 1) * R_V_DIM]
        out = gate * (yn * nw[:, h * R_V_DIM:(h + 1) * R_V_DIM])
        y_ref[tok, h * R_V_DIM:(h + 1) * R_V_DIM] = out.astype(BF16)

    score(units[0])
    for i, unit in enumerate(units):
        if i + 1 < len(units):
            score(units[i + 1])
        emit(unit)


def _ret_core(q, kt, v, g, nw, cdec, *, batch, seq):
    L = R_CHUNK
    W = R_CHUNKS_PER_STEP * L
    assert seq % W == 0
    nb = seq // W
    cur = lambda b, i: (b * nb + i, 0)
    cur_t = lambda b, i: (0, b * nb + i)
    return pl.pallas_call(
        _ret_core_kernel,
        grid=(batch, nb),
        in_specs=[pl.BlockSpec(memory_space=pltpu.SMEM),
                  pl.BlockSpec((W, R_QK_COLS), cur), pl.BlockSpec((R_QK_COLS, W), cur_t),
                  pl.BlockSpec((W, R_V_COLS), cur), pl.BlockSpec((W, R_V_COLS), cur),
                  _resident((1, R_V_COLS))],
        out_specs=pl.BlockSpec((W, R_V_COLS), cur),
        out_shape=jax.ShapeDtypeStruct((batch * seq, R_V_COLS), BF16),
        scratch_shapes=[pltpu.VMEM((R_HEADS, R_QK_DIM, R_V_DIM), F32)],
        compiler_params=_params(2),
        name="ret_core",
    )(cdec, q, kt, v, g, nw)


def _rope_constants(tm):
    inv_a = 1.0 / (ROPE_THETA ** (jnp.arange(0, A_ROPE_DIM, 2, dtype=F32) / A_ROPE_DIM))
    inv_r = 1.0 / (R_THETA ** jnp.linspace(0.0, 1.0, R_QK_DIM // 2, dtype=F32))
    return (jnp.broadcast_to(inv_a[:, None], (A_ROPE_DIM // 2, tm)),
            jnp.broadcast_to(inv_r[:, None], (R_QK_DIM // 2, tm)))


def _retention_constants(tm):
    L = R_CHUNK
    log_gamma = jnp.log(1.0 - 2.0 ** (-5.0 - jnp.arange(R_HEADS, dtype=F32)))
    j = jnp.asarray(np.arange(tm) % L, F32)
    q_decay = jnp.exp((j + 1.0) * log_gamma[:, None])
    k_decay = jnp.exp(-(j + 1.0) * log_gamma[:, None]) * (R_QK_DIM ** -0.5)
    chunk_decay = jnp.exp(L * log_gamma)
    qdec = jnp.broadcast_to(q_decay[:, :, None], (R_HEADS, tm, LANES))
    kdec = jnp.pad(k_decay, ((0, SUBLANES - R_HEADS), (0, 0)))
    return qdec, kdec, chunk_decay


def _attn_weight_layout(w_qkv, b_qkv, tm):
    nq = A_Q_ROWS
    wt = jnp.concatenate([w_qkv[:, :nq], w_qkv[:, nq + A_KV_DIM:]], axis=1).T.astype(BF16)
    b_qv = jnp.concatenate([b_qkv[:nq], b_qkv[nq + A_KV_DIM:]])
    bt = jnp.broadcast_to(b_qv[:, None], (nq + A_KV_DIM, tm))
    wk = w_qkv[:, nq:nq + A_KV_DIM].astype(BF16)
    bk = b_qkv[nq:nq + A_KV_DIM].reshape(1, A_KV_DIM)
    return wt, bt, wk, bk


def _ret_weight_layout(w_in):
    w_all = w_in.astype(BF16)

    def split_pairs(w):
        w = w.reshape(D_MODEL, R_HEADS, R_QK_DIM // 2, 2)
        return jnp.swapaxes(w, 2, 3).reshape(D_MODEL, R_QK_COLS)

    return split_pairs(w_all[:, :R_QK_COLS]), w_all, split_pairs(w_all[:, R_QK_COLS:2 * R_QK_COLS]).T


def _mlstm_weight_layout(w_in, b_if):
    pad = LANES - 2 * M_HEADS
    gate_lo = 2 * M_QK_COLS + 2 * M_V_COLS
    wt = jnp.concatenate([w_in[:, :M_QK_COLS], w_in[:, 2 * M_QK_COLS:gate_lo]], axis=-1).T.astype(BF16)
    wk = w_in[:, M_QK_COLS:2 * M_QK_COLS].astype(BF16)
    wg = jnp.pad(w_in[:, gate_lo:], ((0, 0), (0, pad))).astype(BF16)
    return wt, wk, wg, jnp.pad(b_if, (0, pad)).reshape(1, LANES)


def kernel(x, positions, norm_w, final_norm_w, ffn_w_in, ffn_w_out, attn_w_qkv, attn_b_qkv, attn_w_o, attn_b_o,
           attn_sinks, mlstm_w_in, mlstm_b_if, mlstm_norm_w, mlstm_w_out, ret_w_in, ret_norm_w, ret_w_out):
    batch, seq, d = x.shape
    t = batch * seq
    tm = min(ROW_TILE, t)
    assert d == D_MODEL and t % tm == 0 and (tm // ROW_SPLIT) % R_CHUNK == 0
    assert seq % R_CHUNK == 0 and seq % M_CHUNK == 0 and seq % (A_CHUNK * A_BLOCKS_PER_STEP) == 0

    xf = x.reshape(t, d)
    inv_a, inv_r = _rope_constants(tm)
    cos_a, sin_a, cos_r, sin_r, cos_at, sin_at, cos_rt, sin_rt = _rope_tables(
        positions.astype(F32).reshape(1, t), inv_a, inv_r, tm)
    qdec, kdec, cdec = _retention_constants(tm)
    zero_bias = jnp.zeros((1, D_MODEL), F32)
    norm_w4 = norm_w.reshape(DEPTH, 3, 1, d)
    w_in_bf = ffn_w_in.astype(BF16)
    w_out_bf = ffn_w_out.astype(BF16)

    for i in range(DEPTH):
        kind, slot = i % N_MIXERS, i // N_MIXERS
        xf = _ffn(xf, norm_w4, w_in_bf, w_out_bf, i, 0, tm=tm)
        nw = norm_w[i, 1].reshape(1, d)
        if kind == 0:
            wt, bt, wk, bk = _attn_weight_layout(attn_w_qkv[slot], attn_b_qkv[slot], tm)
            qt, k, vt = _attn_proj(xf, nw, wt, bt, wk, bk, cos_a, sin_a, cos_at, sin_at, tm=tm)
            yt = _attn_core(qt, k, vt, attn_sinks[slot], batch=batch, seq=seq)
            mix = (yt, attn_w_o[slot].astype(BF16), attn_b_o[slot].reshape(1, d), True)
        elif kind == 1:
            wt, wk, wg, bg = _mlstm_weight_layout(mlstm_w_in[slot], mlstm_b_if[slot])
            qt, k, vt, ot, src, gt = _mlstm_proj(xf, nw, wt, wk, wg, bg, tm=tm)
            nwt = jnp.broadcast_to(mlstm_norm_w[slot][:, None], (M_V_COLS, M_CHUNK))
            yt = _mlstm_core(qt, k, vt, ot, src, gt, nwt, batch=batch, seq=seq)
            mix = (yt, mlstm_w_out[slot].astype(BF16), zero_bias, True)
        else:
            wq, w_all, wkt = _ret_weight_layout(ret_w_in[slot])
            q, kt, v, g = _ret_proj(xf, nw, wq, w_all, wkt, cos_r, sin_r, cos_rt, sin_rt, qdec, kdec, tm=tm)
            y = _ret_core(q, kt, v, g, ret_norm_w[slot].reshape(1, R_V_COLS), cdec, batch=batch, seq=seq)
            mix = (y, ret_w_out[slot].astype(BF16), zero_bias, False)
        last = i == DEPTH - 1
        xf = _ffn(xf, norm_w4, w_in_bf, w_out_bf, i, 1, tm=tm, mix=mix,
                  final_w=final_norm_w.reshape(1, d) if last else None)
    return xf.reshape(batch, seq, d)
```

```python
import functools

import jax
import jax.numpy as jnp
import numpy as np
from jax import lax
from jax.experimental import pallas as pl
from jax.experimental.pallas import tpu as pltpu

F32 = jnp.float32
BF16 = jnp.bfloat16

D_MODEL = 1024
DEPTH = 4
N_MIXERS = 3
EPS = 1e-6
D_FF = 2816
A_HEADS = 16
A_KV_HEADS = 4
A_GROUP = A_HEADS // A_KV_HEADS
A_HEAD_DIM = 64
A_WINDOW = 128
A_ROPE_DIM = 16
ROPE_THETA = 500000.0
M_HEADS = 8
M_QK_DIM = 64
M_V_DIM = 128
M_GATE_CAP = 15.0
R_HEADS = 4
R_QK_DIM = 256
R_V_DIM = 512
R_THETA = 10000.0

LANES = 128
SUBLANES = 8
BF16_ROWS = 16
VMEM_LIMIT = 56 * 1024 * 1024

ROW_TILE = 1024
ROW_SPLIT = 2
FF_CHUNK = 256
A_CHUNK = A_WINDOW
A_BLOCKS_PER_STEP = 16
M_CHUNK = 128
M_CHUNKS_PER_STEP = 8
R_CHUNK = 256
R_CHUNKS_PER_STEP = 4


def _params(n_axes):
    return pltpu.CompilerParams(
        dimension_semantics=("arbitrary",) * n_axes,
        vmem_limit_bytes=VMEM_LIMIT,
    )


def _resident(shape):
    nd = len(shape)
    return pl.BlockSpec(shape, lambda *_: (0,) * nd, pipeline_mode=pl.Buffered(1))


def _resident_slice(lead, shape):
    nd = len(shape)
    return pl.BlockSpec((None,) * len(lead) + tuple(shape), lambda *_: tuple(lead) + (0,) * nd,
                        pipeline_mode=pl.Buffered(1))


def _rows(tm, width):
    return pl.BlockSpec((tm, width), lambda i: (i, 0))


def _cols(height, tm):
    return pl.BlockSpec((height, tm), lambda i: (0, i))


def _rms(x, w):
    ms = jnp.mean(x * x, axis=-1, keepdims=True)
    return x * lax.rsqrt(ms + EPS) * w


def _dot(a, b):
    return jnp.dot(a, b, preferred_element_type=F32)


def _dot_nt(a, b):
    return lax.dot_general(a, b, (((1,), (1,)), ((), ())), preferred_element_type=F32)


def _dot_tn(a, b):
    return lax.dot_general(a, b, (((0,), (0,)), ((), ())), preferred_element_type=F32)


def _tables_kernel(pos_ref, inv_a_ref, inv_r_ref, cos_a, sin_a, cos_r, sin_r, cos_at, sin_at, cos_rt, sin_rt):
    tm = pos_ref.shape[1]
    pos = pos_ref[...]
    ang_a = inv_a_ref[...] * pos
    ca = jnp.cos(ang_a)
    sa = jnp.sin(ang_a)
    cos_at[...] = ca
    sin_at[...] = sa
    rest = A_HEAD_DIM - A_ROPE_DIM
    cos_head = jnp.concatenate([ca, ca, jnp.ones((rest, tm), F32)], axis=0)
    sin_head = jnp.concatenate([-sa, sa, jnp.zeros((rest, tm), F32)], axis=0)
    cos_a[...] = jnp.concatenate([cos_head] * (LANES // A_HEAD_DIM), axis=0).T
    sin_a[...] = jnp.concatenate([sin_head] * (LANES // A_HEAD_DIM), axis=0).T
    ang_r = inv_r_ref[...] * pos
    cr = jnp.cos(ang_r)
    sr = jnp.sin(ang_r)
    cos_rt[...] = cr
    sin_rt[...] = sr
    cos_r[...] = cr.T
    sin_r[...] = sr.T


def _rope_tables(pos_row, inv_a, inv_r, tm):
    t = pos_row.shape[1]
    half = A_ROPE_DIM // 2
    tok = jax.ShapeDtypeStruct((t, LANES), F32)
    return pl.pallas_call(
        _tables_kernel,
        grid=(t // tm,),
        in_specs=[_cols(1, tm), _resident((half, tm)), _resident((LANES, tm))],
        out_specs=[_rows(tm, LANES)] * 4 + [_cols(half, tm), _cols(half, tm), _cols(LANES, tm), _cols(LANES, tm)],
        out_shape=[tok] * 4 + [jax.ShapeDtypeStruct((half, t), F32)] * 2 + [jax.ShapeDtypeStruct((LANES, t), F32)] * 2,
        compiler_params=_params(1),
        name="rope_tables",
    )(pos_row, inv_a, inv_r)


def _ffn_kernel(*refs, mix_in, mix_feature_major, final_norm):
    refs = list(refs)
    x_ref = refs.pop(0)
    if mix_in:
        y_ref, wo_ref, bo_ref = refs.pop(0), refs.pop(0), refs.pop(0)
    nw_ref, win_ref, wout_ref = refs.pop(0), refs.pop(0), refs.pop(0)
    fw_ref = refs.pop(0) if final_norm else None
    (o_ref,) = refs

    tm = x_ref.shape[0]
    sub = tm // ROW_SPLIT
    for part in range(ROW_SPLIT):
        rows = slice(part * sub, (part + 1) * sub)
        x = x_ref[rows, :]
        if mix_in:
            if mix_feature_major:
                mixed = _dot_tn(y_ref[:, rows], wo_ref[...])
            else:
                mixed = _dot(y_ref[rows, :], wo_ref[...])
            x = x + (mixed + bo_ref[...])
        h = _rms(x, nw_ref[...]).astype(BF16)
        acc = jnp.zeros_like(x)
        for c in range(D_FF // FF_CHUNK):
            lo = c * FF_CHUNK
            g = _dot(h, win_ref[:, lo:lo + FF_CHUNK])
            u = _dot(h, win_ref[:, D_FF + lo:D_FF + lo + FF_CHUNK])
            a = (g * jax.nn.sigmoid(g) * u).astype(BF16)
            acc = acc + _dot(a, wout_ref[lo:lo + FF_CHUNK, :])
        y = x + 0.5 * acc
        if final_norm:
            y = _rms(y, fw_ref[...])
        o_ref[rows, :] = y


def _ffn(x, norm_w, w_in, w_out, layer, which, *, tm, mix=None, final_w=None):
    t = x.shape[0]
    in_specs = [_rows(tm, D_MODEL)]
    args = [x]
    feature_major = False
    if mix is not None:
        y, w_o, b_o, feature_major = mix
        kdim = w_o.shape[0]
        assert y.shape == ((kdim, t) if feature_major else (t, kdim))
        in_specs += [_cols(kdim, tm) if feature_major else _rows(tm, kdim),
                     _resident((kdim, D_MODEL)), _resident((1, D_MODEL))]
        args += [y, w_o, b_o]
    in_specs += [_resident_slice((layer, 2 * which), (1, D_MODEL)),
                 _resident_slice((layer, which), (D_MODEL, 2 * D_FF)),
                 _resident_slice((layer, which), (D_FF, D_MODEL))]
    args += [norm_w, w_in, w_out]
    if final_w is not None:
        in_specs.append(_resident((1, D_MODEL)))
        args.append(final_w)
    return pl.pallas_call(
        functools.partial(_ffn_kernel, mix_in=mix is not None, mix_feature_major=feature_major,
                          final_norm=final_w is not None),
        grid=(t // tm,),
        in_specs=in_specs,
        out_specs=_rows(tm, D_MODEL),
        out_shape=jax.ShapeDtypeStruct((t, D_MODEL), F32),
        compiler_params=_params(1),
        name="ffn_mix" if mix is not None else "ffn",
    )(*args)


A_Q_ROWS = A_HEADS * A_HEAD_DIM
A_KV_DIM = A_KV_HEADS * A_HEAD_DIM
A_PROJ_ROWS = 256


def _attn_proj_kernel(x_ref, nw_ref, wt_ref, bt_ref, wk_ref, bk_ref, cos_ref, sin_ref, cost_ref, sint_ref,
                      qt_ref, k_ref, vt_ref):
    sub = x_ref.shape[0] // ROW_SPLIT
    half = A_ROPE_DIM // 2
    lane = lax.broadcasted_iota(jnp.int32, (sub, LANES), 1) & (A_HEAD_DIM - 1)
    upper = (lane >= half) & (lane < A_ROPE_DIM)
    q_scale = A_HEAD_DIM ** -0.5
    for split in range(ROW_SPLIT):
        tok = slice(split * sub, (split + 1) * sub)
        h = _rms(x_ref[tok, :], nw_ref[...]).astype(BF16)

        cos = cos_ref[tok, :]
        sin = sin_ref[tok, :]
        k = _dot(h, wk_ref[...]) + bk_ref[...]
        for j in range(A_KV_DIM // LANES):
            piece = k[:, j * LANES:(j + 1) * LANES]
            partner = jnp.where(upper, pltpu.roll(piece, half, 1), pltpu.roll(piece, LANES - half, 1))
            k_ref[tok, j * LANES:(j + 1) * LANES] = (piece * cos + partner * sin).astype(BF16)

        cos_t = cost_ref[:, tok]
        sin_t = sint_ref[:, tok]
        for c in range((A_Q_ROWS + A_KV_DIM) // A_PROJ_ROWS):
            lo = c * A_PROJ_ROWS
            part = _dot_nt(wt_ref[lo:lo + A_PROJ_ROWS, :], h) + bt_ref[lo:lo + A_PROJ_ROWS, tok]
            if lo < A_Q_ROWS:
                for j in range(A_PROJ_ROWS // A_HEAD_DIM):
                    base = j * A_HEAD_DIM
                    x1 = part[base:base + half]
                    x2 = part[base + half:base + 2 * half]
                    rot = jnp.concatenate([x1 * cos_t - x2 * sin_t, x2 * cos_t + x1 * sin_t,
                                           part[base + 2 * half:base + A_HEAD_DIM]], axis=0)
                    qt_ref[lo + base:lo + base + A_HEAD_DIM, tok] = (rot * q_scale).astype(BF16)
            else:
                vt_ref[:, tok] = part.astype(BF16)


def _attn_proj(x, nw, wt, bt, wk, bk, cos_a, sin_a, cos_at, sin_at, *, tm):
    t = x.shape[0]
    half = A_ROPE_DIM // 2
    rows = A_Q_ROWS + A_KV_DIM
    return pl.pallas_call(
        _attn_proj_kernel,
        grid=(t // tm,),
        in_specs=[_rows(tm, D_MODEL), _resident((1, D_MODEL)), _resident((rows, D_MODEL)), _resident((rows, tm)),
                  _resident((D_MODEL, A_KV_DIM)), _resident((1, A_KV_DIM)),
                  _rows(tm, LANES), _rows(tm, LANES), _cols(half, tm), _cols(half, tm)],
        out_specs=[_cols(A_Q_ROWS, tm), _rows(tm, A_KV_DIM), _cols(A_KV_DIM, tm)],
        out_shape=[jax.ShapeDtypeStruct((A_Q_ROWS, t), BF16),
                   jax.ShapeDtypeStruct((t, A_KV_DIM), BF16),
                   jax.ShapeDtypeStruct((A_KV_DIM, t), BF16)],
        compiler_params=_params(1),
        name="attn_proj",
    )(x, nw, wt, bt, wk, bk, cos_a, sin_a, cos_at, sin_at)


def _attn_core_kernel(sink_ref, qt_ref, kp_ref, kc_ref, vtp_ref, vtc_ref, ot_ref):
    L = A_CHUNK
    step = pl.program_id(1)
    key = lax.broadcasted_iota(jnp.int32, (L, L), 0)
    qry = lax.broadcasted_iota(jnp.int32, (L, L), 1)
    from_prev = key > qry
    first_bias = jnp.where(step > 0, 0.0, -jnp.inf).astype(F32)
    zero_q = jnp.zeros((A_HEAD_DIM, A_GROUP * L), BF16)
    units = [(u, g) for u in range(A_BLOCKS_PER_STEP) for g in range(A_KV_HEADS)]
    scores, probs_of = {}, {}

    def score(unit):
        u, g = unit
        col = slice((g // 2) * LANES, (g // 2 + 1) * LANES)
        if u == 0:
            kk = jnp.concatenate([kp_ref[:, col], kc_ref[:L, col]], axis=0)
        else:
            kk = kc_ref[(u - 1) * L:(u + 1) * L, col]
        q_cat = jnp.concatenate([qt_ref[(A_GROUP * g + j) * A_HEAD_DIM:(A_GROUP * g + j + 1) * A_HEAD_DIM,
                                        u * L:(u + 1) * L] for j in range(A_GROUP)], axis=1)
        q_ext = jnp.concatenate([q_cat, zero_q] if g % 2 == 0 else [zero_q, q_cat], axis=0)
        scores[unit] = _dot(kk, q_ext)

    def softmax(unit):
        u, g = unit
        s = scores.pop(unit)
        probs = []
        for j in range(A_GROUP):
            sink = sink_ref[A_GROUP * g + j]
            s_prev = s[:L, j * L:(j + 1) * L]
            if u == 0:
                s_prev = s_prev + first_bias
            s_own = s[L:, j * L:(j + 1) * L]
            sc = jnp.where(from_prev, s_prev, s_own)
            mx = jnp.maximum(jnp.max(sc, axis=0, keepdims=True), sink)
            p = jnp.exp(sc - mx)
            denom = jnp.sum(p, axis=0, keepdims=True) + jnp.exp(sink - mx)
            pn = p * (1.0 / denom)
            zero = jnp.zeros_like(pn)
            probs.append(jnp.concatenate([jnp.where(from_prev, pn, zero), jnp.where(from_prev, zero, pn)],
                                         axis=0).astype(BF16))
        probs_of[unit] = jnp.concatenate(probs, axis=1)

    def weighted_values(unit):
        u, g = unit
        rows = slice(g * A_HEAD_DIM, (g + 1) * A_HEAD_DIM)
        if u == 0:
            vt = jnp.concatenate([vtp_ref[rows, :], vtc_ref[rows, :L]], axis=1)
        else:
            vt = vtc_ref[rows, (u - 1) * L:(u + 1) * L]
        out = _dot(vt, probs_of.pop(unit))
        for j in range(A_GROUP):
            head = A_GROUP * g + j
            ot_ref[head * A_HEAD_DIM:(head + 1) * A_HEAD_DIM, u * L:(u + 1) * L] = (
                out[:, j * L:(j + 1) * L].astype(BF16))

    n = len(units)
    score(units[0])
    for i in range(n + 1):
        if i + 1 < n:
            score(units[i + 1])
        if i < n:
            softmax(units[i])
        if i >= 1:
            weighted_values(units[i - 1])


def _attn_core(qt, k, vt, sinks, *, batch, seq):
    L = A_CHUNK
    W = A_BLOCKS_PER_STEP * L
    assert seq % W == 0
    ns = seq // W
    cur = lambda b, i: (b * ns + i, 0)
    cur_t = lambda b, i: (0, b * ns + i)
    prev = lambda b, i: ((b * ns + i) * A_BLOCKS_PER_STEP - jnp.minimum(i, 1), 0)
    prev_t = lambda b, i: (0, (b * ns + i) * A_BLOCKS_PER_STEP - jnp.minimum(i, 1))
    return pl.pallas_call(
        _attn_core_kernel,
        grid=(batch, ns),
        in_specs=[pl.BlockSpec(memory_space=pltpu.SMEM),
                  pl.BlockSpec((A_Q_ROWS, W), cur_t),
                  pl.BlockSpec((L, A_KV_DIM), prev), pl.BlockSpec((W, A_KV_DIM), cur),
                  pl.BlockSpec((A_KV_DIM, L), prev_t), pl.BlockSpec((A_KV_DIM, W), cur_t)],
        out_specs=pl.BlockSpec((A_Q_ROWS, W), cur_t),
        out_shape=jax.ShapeDtypeStruct((A_Q_ROWS, batch * seq), BF16),
        compiler_params=_params(2),
        name="attn_core",
    )(sinks, qt, k, k, vt, vt)


M_QK_COLS = M_HEADS * M_QK_DIM
M_V_COLS = M_HEADS * M_V_DIM
M_T_ROWS = M_QK_COLS + 2 * M_V_COLS
M_PROJ_ROWS = 512
M_STATE_ROWS = M_V_DIM + BF16_ROWS


def _mlstm_proj_kernel(x_ref, nw_ref, wt_ref, wk_ref, wg_ref, bg_ref,
                       qt_ref, k_ref, vt_ref, ot_ref, src_ref, gt_ref):
    sub = x_ref.shape[0] // ROW_SPLIT
    H = M_HEADS
    lane = lax.broadcasted_iota(jnp.int32, (sub, LANES), 1)
    time = lax.broadcasted_iota(jnp.int32, (sub, LANES), 0) & (M_CHUNK - 1)
    for split in range(ROW_SPLIT):
        tok = slice(split * sub, (split + 1) * sub)
        h = _rms(x_ref[tok, :], nw_ref[...]).astype(BF16)
        k_ref[tok, :] = _dot(h, wk_ref[...]).astype(BF16)
        for c in range(M_T_ROWS // M_PROJ_ROWS):
            lo = c * M_PROJ_ROWS
            part = _dot_nt(wt_ref[lo:lo + M_PROJ_ROWS, :], h)
            if lo < M_QK_COLS:
                qt_ref[:, tok] = (part * (M_QK_DIM ** -0.5)).astype(BF16)
            elif lo < M_QK_COLS + M_V_COLS:
                vt_ref[lo - M_QK_COLS:lo - M_QK_COLS + M_PROJ_ROWS, tok] = part.astype(BF16)
            else:
                o_lo = lo - M_QK_COLS - M_V_COLS
                ot_ref[o_lo:o_lo + M_PROJ_ROWS, tok] = jax.nn.sigmoid(part)

        gates = _dot(h, wg_ref[...]) + bg_ref[...]
        gates = M_GATE_CAP * jnp.tanh(gates / M_GATE_CAP)
        b = jnp.where((lane >= H) & (lane < 2 * H), jax.nn.log_sigmoid(gates), 0.0)
        shift = 1
        while shift < M_CHUNK:
            b = b + jnp.where(time >= shift, pltpu.roll(b, shift, 0), 0.0)
            shift *= 2
        src_ref[tok, :] = gates - pltpu.roll(b, LANES - H, 1)
        gt_ref[:, tok] = jnp.where(lane < H, gates, b).T[:2 * H]


def _mlstm_proj(x, nw, wt, wk, wg, bg, *, tm):
    t = x.shape[0]
    assert (tm // ROW_SPLIT) % M_CHUNK == 0 and M_CHUNK & (M_CHUNK - 1) == 0
    return pl.pallas_call(
        _mlstm_proj_kernel,
        grid=(t // tm,),
        in_specs=[_rows(tm, D_MODEL), _resident((1, D_MODEL)), _resident((M_T_ROWS, D_MODEL)),
                  _resident((D_MODEL, M_QK_COLS)), _resident((D_MODEL, LANES)), _resident((1, LANES))],
        out_specs=[_cols(M_QK_COLS, tm), _rows(tm, M_QK_COLS), _cols(M_V_COLS, tm),
                   _cols(M_V_COLS, tm), _rows(tm, LANES), _cols(2 * M_HEADS, tm)],
        out_shape=[jax.ShapeDtypeStruct((M_QK_COLS, t), BF16),
                   jax.ShapeDtypeStruct((t, M_QK_COLS), BF16),
                   jax.ShapeDtypeStruct((M_V_COLS, t), BF16),
                   jax.ShapeDtypeStruct((M_V_COLS, t), F32),
                   jax.ShapeDtypeStruct((t, LANES), F32),
                   jax.ShapeDtypeStruct((2 * M_HEADS, t), F32)],
        compiler_params=_params(1),
        name="mlstm_proj",
    )(x, nw, wt, wk, wg, bg)


def _mlstm_core_kernel(qt_ref, k_ref, vt_ref, ot_ref, src_ref, gt_ref, nwt_ref, yt_ref, c_sc, m_sc):
    L = M_CHUNK
    H = M_HEADS

    @pl.when(pl.program_id(1) == 0)
    def _():
        c_sc[...] = jnp.zeros_like(c_sc)
        m_sc[...] = jnp.zeros_like(m_sc)

    src_i = lax.broadcasted_iota(jnp.int32, (L, L), 0)
    tgt_i = lax.broadcasted_iota(jnp.int32, (L, L), 1)
    causal = src_i <= tgt_i
    ones_rows = (lax.broadcasted_iota(jnp.int32, (BF16_ROWS, L), 0) == 0).astype(BF16)
    zero_q = jnp.zeros((M_QK_DIM, L), BF16)
    low = lax.broadcasted_iota(jnp.int32, (L, LANES), 1) < M_QK_DIM

    units = [(c, h) for c in range(M_CHUNKS_PER_STEP) for h in range(H)]
    st = {unit: dict() for unit in units}

    def score(unit):
        c, h = unit
        d = st[unit]
        tok = slice(c * L, (c + 1) * L)
        pair = h // 2
        k_pair = k_ref[tok, pair * LANES:(pair + 1) * LANES]
        d["k"] = jnp.where(low if h % 2 == 0 else jnp.logical_not(low), k_pair, jnp.zeros_like(k_pair))
        qt_h = qt_ref[h * M_QK_DIM:(h + 1) * M_QK_DIM, tok]
        d["qt"] = jnp.concatenate([qt_h, zero_q] if h % 2 == 0 else [zero_q, qt_h], axis=0)
        d["v"] = jnp.concatenate([vt_ref[h * M_V_DIM:(h + 1) * M_V_DIM, tok], ones_rows], axis=0)
        d["state"] = c_sc[h]
        d["s"] = _dot(d["k"], d["qt"])
        d["carry"] = _dot(d["state"].astype(BF16), d["qt"])

    def gate(unit):
        c, h = unit
        d = st[unit]
        tok = slice(c * L, (c + 1) * L)
        src_col = src_ref[tok, h:h + 1]
        d["b"] = gt_ref[H + h:H + h + 1, tok]
        d["m_prev"] = m_sc[h][:, :1]
        dmat = jnp.where(causal, src_col + d["b"], -jnp.inf)
        inter = d["b"] + d["m_prev"]
        d["m_row"] = jnp.maximum(inter, jnp.max(dmat, axis=0, keepdims=True))
        d["w_intra"] = jnp.exp(dmat - d["m_row"])
        d["w_inter"] = jnp.exp(inter - d["m_row"])

    def emit(unit):
        c, h = unit
        d = st[unit]
        tok = slice(c * L, (c + 1) * L)
        tot = _dot(d["v"], (d["s"] * d["w_intra"]).astype(BF16)) + d["w_inter"] * d["carry"]
        num = tot[:M_V_DIM]
        nq = tot[M_V_DIM:M_V_DIM + 1]
        h_out = num * (1.0 / jnp.maximum(jnp.abs(nq), jnp.exp(-d["m_row"])))
        h_out = h_out * lax.rsqrt(jnp.mean(h_out * h_out, axis=0, keepdims=True) + EPS)
        rows = slice(h * M_V_DIM, (h + 1) * M_V_DIM)
        yt_ref[rows, tok] = (ot_ref[rows, tok] * (h_out * nwt_ref[rows, :])).astype(BF16)

    def update(unit):
        c, h = unit
        d = st[unit]
        b_last = d["b"][:, L - 1:L]
        dec = b_last + (gt_ref[h:h + 1, c * L:(c + 1) * L] - d["b"])
        m_new = jnp.maximum(b_last + d["m_prev"], jnp.max(dec, axis=-1, keepdims=True))
        wk = jnp.exp(dec - m_new)
        keep = jnp.exp(b_last + d["m_prev"] - m_new)
        v_scaled = (d["v"].astype(F32) * wk).astype(BF16)
        c_sc[h] = keep * d["state"] + _dot(v_scaled, d["k"])
        m_sc[h] = jnp.broadcast_to(m_new, (1, LANES))
        d.clear()

    n = len(units)
    score(units[0])
    for i in range(n + 1):
        if i + 1 < n:
            score(units[i + 1])
        if i < n:
            gate(units[i])
            emit(units[i])
        if i >= 1:
            update(units[i - 1])


def _mlstm_core(qt, k, vt, ot, src, gt, nwt, *, batch, seq):
    L = M_CHUNK
    W = M_CHUNKS_PER_STEP * L
    assert seq % W == 0
    nb = seq // W
    cur = lambda b, i: (b * nb + i, 0)
    cur_t = lambda b, i: (0, b * nb + i)
    return pl.pallas_call(
        _mlstm_core_kernel,
        grid=(batch, nb),
        in_specs=[pl.BlockSpec((M_QK_COLS, W), cur_t), pl.BlockSpec((W, M_QK_COLS), cur),
                  pl.BlockSpec((M_V_COLS, W), cur_t), pl.BlockSpec((M_V_COLS, W), cur_t),
                  pl.BlockSpec((W, LANES), cur), pl.BlockSpec((2 * M_HEADS, W), cur_t), _resident((M_V_COLS, L))],
        out_specs=pl.BlockSpec((M_V_COLS, W), cur_t),
        out_shape=jax.ShapeDtypeStruct((M_V_COLS, batch * seq), BF16),
        scratch_shapes=[pltpu.VMEM((M_HEADS, M_STATE_ROWS, LANES), F32),
                        pltpu.VMEM((M_HEADS, 1, LANES), F32)],
        compiler_params=_params(2),
        name="mlstm_core",
    )(qt, k, vt, ot, src, gt, nwt)


R_QK_COLS = R_HEADS * R_QK_DIM
R_V_COLS = R_HEADS * R_V_DIM
R_PROJ_CHUNK = 512


def _ret_proj_kernel(x_ref, nw_ref, wq_ref, wv_ref, wg_ref, wkt_ref, cos_ref, sin_ref, cost_ref, sint_ref, qdec_ref, kdec_ref,
                     q_ref, kt_ref, v_ref, g_ref):
    sub = x_ref.shape[0] // ROW_SPLIT
    half = R_QK_DIM // 2
    for split in range(ROW_SPLIT):
        tok = slice(split * sub, (split + 1) * sub)
        h = _rms(x_ref[tok, :], nw_ref[...]).astype(BF16)
        cos = cos_ref[tok, :]
        sin = sin_ref[tok, :]
        for c in range(R_QK_COLS // R_PROJ_CHUNK):
            lo = c * R_PROJ_CHUNK
            t = _dot(h, wq_ref[:, lo:lo + R_PROJ_CHUNK])
            for hh in range(R_PROJ_CHUNK // R_QK_DIM):
                head = lo // R_QK_DIM + hh
                dec = qdec_ref[head, tok, :]
                x1 = t[:, hh * R_QK_DIM:hh * R_QK_DIM + half]
                x2 = t[:, hh * R_QK_DIM + half:(hh + 1) * R_QK_DIM]
                base = head * R_QK_DIM
                q_ref[tok, base:base + half] = ((x1 * cos - x2 * sin) * dec).astype(BF16)
                q_ref[tok, base + half:base + R_QK_DIM] = ((x2 * cos + x1 * sin) * dec).astype(BF16)
        for c in range(2 * R_V_COLS // R_PROJ_CHUNK):
            lo = c * R_PROJ_CHUNK
            if lo < R_V_COLS:
                v_ref[tok, lo:lo + R_PROJ_CHUNK] = _dot(h, wv_ref[:, lo:lo + R_PROJ_CHUNK]).astype(BF16)
            else:
                g_lo = lo - R_V_COLS
                t = _dot(h, wg_ref[:, g_lo:g_lo + R_PROJ_CHUNK])
                g_ref[tok, g_lo:g_lo + R_PROJ_CHUNK] = (t * jax.nn.sigmoid(t)).astype(BF16)
        cos_t = cost_ref[:, tok]
        sin_t = sint_ref[:, tok]
        for head in range(R_HEADS):
            base = head * R_QK_DIM
            kt = _dot_nt(wkt_ref[base:base + R_QK_DIM, :], h)
            dec = kdec_ref[head:head + 1, tok]
            x1 = kt[:half]
            x2 = kt[half:]
            kt_ref[base:base + half, tok] = ((x1 * cos_t - x2 * sin_t) * dec).astype(BF16)
            kt_ref[base + half:base + R_QK_DIM, tok] = ((x2 * cos_t + x1 * sin_t) * dec).astype(BF16)


def _ret_proj(x, nw, wq, w_all, wkt, cos_r, sin_r, cos_rt, sin_rt, qdec, kdec, *, tm):
    t = x.shape[0]
    assert 2 * R_QK_COLS == R_V_COLS
    col_block = lambda j: pl.BlockSpec((D_MODEL, R_V_COLS), lambda *_: (0, j), pipeline_mode=pl.Buffered(1))
    return pl.pallas_call(
        _ret_proj_kernel,
        grid=(t // tm,),
        in_specs=[_rows(tm, D_MODEL), _resident((1, D_MODEL)), _resident((D_MODEL, R_QK_COLS)),
                  col_block(1), col_block(2), _resident((R_QK_COLS, D_MODEL)),
                  _rows(tm, LANES), _rows(tm, LANES), _cols(LANES, tm), _cols(LANES, tm),
                  _resident((R_HEADS, tm, LANES)), _resident((SUBLANES, tm))],
        out_specs=[_rows(tm, R_QK_COLS), _cols(R_QK_COLS, tm), _rows(tm, R_V_COLS), _rows(tm, R_V_COLS)],
        out_shape=[jax.ShapeDtypeStruct((t, R_QK_COLS), BF16),
                   jax.ShapeDtypeStruct((R_QK_COLS, t), BF16),
                   jax.ShapeDtypeStruct((t, R_V_COLS), BF16),
                   jax.ShapeDtypeStruct((t, R_V_COLS), BF16)],
        compiler_params=_params(1),
        name="ret_proj",
    )(x, nw, wq, w_all, w_all, wkt, cos_r, sin_r, cos_rt, sin_rt, qdec, kdec)


def _ret_core_kernel(cdec_ref, q_ref, kt_ref, v_ref, g_ref, nw_ref, y_ref, s_sc):
    L = R_CHUNK

    @pl.when(pl.program_id(1) == 0)
    def _():
        s_sc[...] = jnp.zeros_like(s_sc)

    row = lax.broadcasted_iota(jnp.int32, (L, L), 0)
    col = lax.broadcasted_iota(jnp.int32, (L, L), 1)
    causal = row >= col
    nw = nw_ref[...]
    units = [(c, h) for c in range(R_CHUNKS_PER_STEP) for h in range(R_HEADS)]
    st = {unit: dict() for unit in units}

    def score(unit):
        c, h = unit
        d = st[unit]
        tok = slice(c * L, (c + 1) * L)
        d["q"] = q_ref[tok, h * R_QK_DIM:(h + 1) * R_QK_DIM]
        d["kt"] = kt_ref[h * R_QK_DIM:(h + 1) * R_QK_DIM, tok]
        d["v"] = v_ref[tok, h * R_V_DIM:(h + 1) * R_V_DIM]
        d["state"] = s_sc[h]
        d["scores"] = _dot(d["q"], d["kt"])
        d["carry"] = _dot(d["q"], d["state"].astype(BF16))

    def emit(unit):
        c, h = unit
        d = st[unit]
        tok = slice(c * L, (c + 1) * L)
        scores = jnp.where(causal, d["scores"], 0.0)
        y = _dot(scores.astype(BF16), d["v"]) + d["carry"]
        s_sc[h] = cdec_ref[h] * (d["state"] + _dot(d["kt"], d["v"]))
        d.clear()
        mu = jnp.mean(y, axis=-1, keepdims=True)
        yc = y - mu
        yn = yc * lax.rsqrt(jnp.mean(yc * yc, axis=-1, keepdims=True) + EPS)
        gate = g_ref[tok, h * R_V_DIM:(h + 1) * R_V_DIM]
        out = gate * (yn * nw[:, h * R_V_DIM:(h + 1) * R_V_DIM])
        y_ref[tok, h * R_V_DIM:(h + 1) * R_V_DIM] = out.astype(BF16)

    score(units[0])
    for i, unit in enumerate(units):
        if i + 1 < len(units):
            score(units[i + 1])
        emit(unit)


def _ret_core(q, kt, v, g, nw, cdec, *, batch, seq):
    L = R_CHUNK
    W = R_CHUNKS_PER_STEP * L
    assert seq % W == 0
    nb = seq // W
    cur = lambda b, i: (b * nb + i, 0)
    cur_t = lambda b, i: (0, b * nb + i)
    return pl.pallas_call(
        _ret_core_kernel,
        grid=(batch, nb),
        in_specs=[pl.BlockSpec(memory_space=pltpu.SMEM),
                  pl.BlockSpec((W, R_QK_COLS), cur), pl.BlockSpec((R_QK_COLS, W), cur_t),
                  pl.BlockSpec((W, R_V_COLS), cur), pl.BlockSpec((W, R_V_COLS), cur),
                  _resident((1, R_V_COLS))],
        out_specs=pl.BlockSpec((W, R_V_COLS), cur),
        out_shape=jax.ShapeDtypeStruct((batch * seq, R_V_COLS), BF16),
        scratch_shapes=[pltpu.VMEM((R_HEADS, R_QK_DIM, R_V_DIM), F32)],
        compiler_params=_params(2),
        name="ret_core",
    )(cdec, q, kt, v, g, nw)


def _rope_constants(tm):
    inv_a = 1.0 / (ROPE_THETA ** (jnp.arange(0, A_ROPE_DIM, 2, dtype=F32) / A_ROPE_DIM))
    inv_r = 1.0 / (R_THETA ** jnp.linspace(0.0, 1.0, R_QK_DIM // 2, dtype=F32))
    return (jnp.broadcast_to(inv_a[:, None], (A_ROPE_DIM // 2, tm)),
            jnp.broadcast_to(inv_r[:, None], (R_QK_DIM // 2, tm)))


def _retention_constants(tm):
    L = R_CHUNK
    log_gamma = jnp.log(1.0 - 2.0 ** (-5.0 - jnp.arange(R_HEADS, dtype=F32)))
    j = jnp.asarray(np.arange(tm) % L, F32)
    q_decay = jnp.exp((j + 1.0) * log_gamma[:, None])
    k_decay = jnp.exp(-(j + 1.0) * log_gamma[:, None]) * (R_QK_DIM ** -0.5)
    chunk_decay = jnp.exp(L * log_gamma)
    qdec = jnp.broadcast_to(q_decay[:, :, None], (R_HEADS, tm, LANES))
    kdec = jnp.pad(k_decay, ((0, SUBLANES - R_HEADS), (0, 0)))
    return qdec, kdec, chunk_decay


def _attn_weight_layout(w_qkv, b_qkv, tm):
    nq = A_Q_ROWS
    wt = jnp.concatenate([w_qkv[:, :nq], w_qkv[:, nq + A_KV_DIM:]], axis=1).T.astype(BF16)
    b_qv = jnp.concatenate([b_qkv[:nq], b_qkv[nq + A_KV_DIM:]])
    bt = jnp.broadcast_to(b_qv[:, None], (nq + A_KV_DIM, tm))
    wk = w_qkv[:, nq:nq + A_KV_DIM].astype(BF16)
    bk = b_qkv[nq:nq + A_KV_DIM].reshape(1, A_KV_DIM)
    return wt, bt, wk, bk


def _ret_weight_layout(w_in):
    w_all = w_in.astype(BF16)

    def split_pairs(w):
        w = w.reshape(D_MODEL, R_HEADS, R_QK_DIM // 2, 2)
        return jnp.swapaxes(w, 2, 3).reshape(D_MODEL, R_QK_COLS)

    return split_pairs(w_all[:, :R_QK_COLS]), w_all, split_pairs(w_all[:, R_QK_COLS:2 * R_QK_COLS]).T


def _mlstm_weight_layout(w_in, b_if):
    pad = LANES - 2 * M_HEADS
    gate_lo = 2 * M_QK_COLS + 2 * M_V_COLS
    wt = jnp.concatenate([w_in[:, :M_QK_COLS], w_in[:, 2 * M_QK_COLS:gate_lo]], axis=-1).T.astype(BF16)
    wk = w_in[:, M_QK_COLS:2 * M_QK_COLS].astype(BF16)
    wg = jnp.pad(w_in[:, gate_lo:], ((0, 0), (0, pad))).astype(BF16)
    return wt, wk, wg, jnp.pad(b_if, (0, pad)).reshape(1, LANES)


def kernel(x, positions, norm_w, final_norm_w, ffn_w_in, ffn_w_out, attn_w_qkv, attn_b_qkv, attn_w_o, attn_b_o,
           attn_sinks, mlstm_w_in, mlstm_b_if, mlstm_norm_w, mlstm_w_out, ret_w_in, ret_norm_w, ret_w_out):
    batch, seq, d = x.shape
    t = batch * seq
    tm = min(ROW_TILE, t)
    assert d == D_MODEL and t % tm == 0 and (tm // ROW_SPLIT) % R_CHUNK == 0
    assert seq % R_CHUNK == 0 and seq % M_CHUNK == 0 and seq % (A_CHUNK * A_BLOCKS_PER_STEP) == 0

    xf = x.reshape(t, d)
    inv_a, inv_r = _rope_constants(tm)
    cos_a, sin_a, cos_r, sin_r, cos_at, sin_at, cos_rt, sin_rt = _rope_tables(
        positions.astype(F32).reshape(1, t), inv_a, inv_r, tm)
    qdec, kdec, cdec = _retention_constants(tm)
    zero_bias = jnp.zeros((1, D_MODEL), F32)
    norm_w4 = norm_w.reshape(DEPTH, 3, 1, d)
    w_in_bf = ffn_w_in.astype(BF16)
    w_out_bf = ffn_w_out.astype(BF16)

    for i in range(DEPTH):
        kind, slot = i % N_MIXERS, i // N_MIXERS
        xf = _ffn(xf, norm_w4, w_in_bf, w_out_bf, i, 0, tm=tm)
        nw = norm_w[i, 1].reshape(1, d)
        if kind == 0:
            wt, bt, wk, bk = _attn_weight_layout(attn_w_qkv[slot], attn_b_qkv[slot], tm)
            qt, k, vt = _attn_proj(xf, nw, wt, bt, wk, bk, cos_a, sin_a, cos_at, sin_at, tm=tm)
            yt = _attn_core(qt, k, vt, attn_sinks[slot], batch=batch, seq=seq)
            mix = (yt, attn_w_o[slot].astype(BF16), attn_b_o[slot].reshape(1, d), True)
        elif kind == 1:
            wt, wk, wg, bg = _mlstm_weight_layout(mlstm_w_in[slot], mlstm_b_if[slot])
            qt, k, vt, ot, src, gt = _mlstm_proj(xf, nw, wt, wk, wg, bg, tm=tm)
            nwt = jnp.broadcast_to(mlstm_norm_w[slot][:, None], (M_V_COLS, M_CHUNK))
            yt = _mlstm_core(qt, k, vt, ot, src, gt, nwt, batch=batch, seq=seq)
            mix = (yt, mlstm_w_out[slot].astype(BF16), zero_bias, True)
        else:
            wq, w_all, wkt = _ret_weight_layout(ret_w_in[slot])
            q, kt, v, g = _ret_proj(xf, nw, wq, w_all, wkt, cos_r, sin_r, cos_rt, sin_rt, qdec, kdec, tm=tm)
            y = _ret_core(q, kt, v, g, ret_norm_w[slot].reshape(1, R_V_COLS), cdec, batch=batch, seq=seq)
            mix = (y, ret_w_out[slot].astype(BF16), zero_bias, False)
        last = i == DEPTH - 1
        xf = _ffn(xf, norm_w4, w_in_bf, w_out_bf, i, 1, tm=tm, mix=mix,
                  final_w=final_norm_w.reshape(1, d) if last else None)
    return xf.reshape(batch, seq, d)
```

```python
import functools

import jax
import jax.numpy as jnp
import numpy as np
from jax import lax
from jax.experimental import pallas as pl
from jax.experimental.pallas import tpu as pltpu

F32 = jnp.float32
BF16 = jnp.bfloat16

D_MODEL = 1024
DEPTH = 4
N_MIXERS = 3
EPS = 1e-6
D_FF = 2816
A_HEADS = 16
A_KV_HEADS = 4
A_GROUP = A_HEADS // A_KV_HEADS
A_HEAD_DIM = 64
A_WINDOW = 128
A_ROPE_DIM = 16
ROPE_THETA = 500000.0
M_HEADS = 8
M_QK_DIM = 64
M_V_DIM = 128
M_GATE_CAP = 15.0
R_HEADS = 4
R_QK_DIM = 256
R_V_DIM = 512
R_THETA = 10000.0

LANES = 128
SUBLANES = 8
BF16_ROWS = 16
VMEM_LIMIT = 56 * 1024 * 1024

ROW_TILE = 1024
ROW_SPLIT = 2
FF_CHUNK = 256
A_CHUNK = A_WINDOW
A_BLOCKS_PER_STEP = 16
M_CHUNK = 128
M_CHUNKS_PER_STEP = 8
R_CHUNK = 256
R_CHUNKS_PER_STEP = 4


def _params(n_axes, fuse_inputs=None):
    return pltpu.CompilerParams(
        dimension_semantics=("arbitrary",) * n_axes,
        vmem_limit_bytes=VMEM_LIMIT,
        allow_input_fusion=fuse_inputs,
    )


def _resident(shape):
    nd = len(shape)
    return pl.BlockSpec(shape, lambda *_: (0,) * nd, pipeline_mode=pl.Buffered(1))


def _resident_slice(lead, shape):
    nd = len(shape)
    return pl.BlockSpec((None,) * len(lead) + tuple(shape), lambda *_: tuple(lead) + (0,) * nd,
                        pipeline_mode=pl.Buffered(1))


def _rows(tm, width):
    return pl.BlockSpec((tm, width), lambda i: (i, 0))


def _cols(height, tm):
    return pl.BlockSpec((height, tm), lambda i: (0, i))


def _rms(x, w):
    ms = jnp.mean(x * x, axis=-1, keepdims=True)
    return x * lax.rsqrt(ms + EPS) * w


def _dot(a, b):
    return jnp.dot(a, b, preferred_element_type=F32)


def _dot_nt(a, b):
    return lax.dot_general(a, b, (((1,), (1,)), ((), ())), preferred_element_type=F32)


def _dot_tn(a, b):
    return lax.dot_general(a, b, (((0,), (0,)), ((), ())), preferred_element_type=F32)


def _tables_kernel(pos_ref, inv_a_ref, inv_r_ref, cos_a, sin_a, cos_r, sin_r, cos_at, sin_at, cos_rt, sin_rt):
    tm = pos_ref.shape[1]
    pos = pos_ref[...]
    ang_a = inv_a_ref[...] * pos
    ca = jnp.cos(ang_a)
    sa = jnp.sin(ang_a)
    cos_at[...] = ca
    sin_at[...] = sa
    rest = A_HEAD_DIM - A_ROPE_DIM
    cos_head = jnp.concatenate([ca, ca, jnp.ones((rest, tm), F32)], axis=0)
    sin_head = jnp.concatenate([-sa, sa, jnp.zeros((rest, tm), F32)], axis=0)
    cos_a[...] = jnp.concatenate([cos_head] * (LANES // A_HEAD_DIM), axis=0).T
    sin_a[...] = jnp.concatenate([sin_head] * (LANES // A_HEAD_DIM), axis=0).T
    ang_r = inv_r_ref[...] * pos
    cr = jnp.cos(ang_r)
    sr = jnp.sin(ang_r)
    cos_rt[...] = cr
    sin_rt[...] = sr
    cos_r[...] = cr.T
    sin_r[...] = sr.T


def _rope_tables(pos_row, inv_a, inv_r, tm):
    t = pos_row.shape[1]
    half = A_ROPE_DIM // 2
    tok = jax.ShapeDtypeStruct((t, LANES), F32)
    return pl.pallas_call(
        _tables_kernel,
        grid=(t // tm,),
        in_specs=[_cols(1, tm), _resident((half, tm)), _resident((LANES, tm))],
        out_specs=[_rows(tm, LANES)] * 4 + [_cols(half, tm), _cols(half, tm), _cols(LANES, tm), _cols(LANES, tm)],
        out_shape=[tok] * 4 + [jax.ShapeDtypeStruct((half, t), F32)] * 2 + [jax.ShapeDtypeStruct((LANES, t), F32)] * 2,
        compiler_params=_params(1),
        name="rope_tables",
    )(pos_row, inv_a, inv_r)


def _ffn_kernel(*refs, mix_in, mix_feature_major, final_norm):
    refs = list(refs)
    x_ref = refs.pop(0)
    if mix_in:
        y_ref, wo_ref, bo_ref = refs.pop(0), refs.pop(0), refs.pop(0)
    nw_ref, win_ref, wout_ref = refs.pop(0), refs.pop(0), refs.pop(0)
    fw_ref = refs.pop(0) if final_norm else None
    (o_ref,) = refs

    tm = x_ref.shape[0]
    sub = tm // ROW_SPLIT
    for part in range(ROW_SPLIT):
        rows = slice(part * sub, (part + 1) * sub)
        x = x_ref[rows, :]
        if mix_in:
            if mix_feature_major:
                mixed = _dot_tn(y_ref[:, rows], wo_ref[...])
            else:
                mixed = _dot(y_ref[rows, :], wo_ref[...])
            x = x + (mixed + bo_ref[...])
        h = _rms(x, nw_ref[...]).astype(BF16)
        acc = jnp.zeros_like(x)
        for c in range(D_FF // FF_CHUNK):
            lo = c * FF_CHUNK
            g = _dot(h, win_ref[:, lo:lo + FF_CHUNK])
            u = _dot(h, win_ref[:, D_FF + lo:D_FF + lo + FF_CHUNK])
            a = (g * jax.nn.sigmoid(g) * u).astype(BF16)
            acc = acc + _dot(a, wout_ref[lo:lo + FF_CHUNK, :])
        y = x + 0.5 * acc
        if final_norm:
            y = _rms(y, fw_ref[...])
        o_ref[rows, :] = y


def _ffn(x, norm_w, w_in, w_out, layer, which, *, tm, mix=None, final_w=None):
    t = x.shape[0]
    in_specs = [_rows(tm, D_MODEL)]
    args = [x]
    feature_major = False
    if mix is not None:
        y, w_o, b_o, feature_major = mix
        kdim = w_o.shape[0]
        assert y.shape == ((kdim, t) if feature_major else (t, kdim))
        in_specs += [_cols(kdim, tm) if feature_major else _rows(tm, kdim),
                     _resident((kdim, D_MODEL)), _resident((1, D_MODEL))]
        args += [y, w_o, b_o]
    in_specs += [_resident_slice((layer, 2 * which), (1, D_MODEL)),
                 _resident_slice((layer, which), (D_MODEL, 2 * D_FF)),
                 _resident_slice((layer, which), (D_FF, D_MODEL))]
    args += [norm_w, w_in, w_out]
    if final_w is not None:
        in_specs.append(_resident((1, D_MODEL)))
        args.append(final_w)
    return pl.pallas_call(
        functools.partial(_ffn_kernel, mix_in=mix is not None, mix_feature_major=feature_major,
                          final_norm=final_w is not None),
        grid=(t // tm,),
        in_specs=in_specs,
        out_specs=_rows(tm, D_MODEL),
        out_shape=jax.ShapeDtypeStruct((t, D_MODEL), F32),
        compiler_params=_params(1, fuse_inputs=[a is w_in or a is w_out for a in args]),
        name="ffn_mix" if mix is not None else "ffn",
    )(*args)


A_Q_ROWS = A_HEADS * A_HEAD_DIM
A_KV_DIM = A_KV_HEADS * A_HEAD_DIM
A_PROJ_ROWS = 256


def _attn_proj_kernel(x_ref, nw_ref, wt_ref, bt_ref, wk_ref, bk_ref, cos_ref, sin_ref, cost_ref, sint_ref,
                      qt_ref, k_ref, vt_ref):
    sub = x_ref.shape[0] // ROW_SPLIT
    half = A_ROPE_DIM // 2
    lane = lax.broadcasted_iota(jnp.int32, (sub, LANES), 1) & (A_HEAD_DIM - 1)
    upper = (lane >= half) & (lane < A_ROPE_DIM)
    q_scale = A_HEAD_DIM ** -0.5
    for split in range(ROW_SPLIT):
        tok = slice(split * sub, (split + 1) * sub)
        h = _rms(x_ref[tok, :], nw_ref[...]).astype(BF16)

        cos = cos_ref[tok, :]
        sin = sin_ref[tok, :]
        k = _dot(h, wk_ref[...]) + bk_ref[...]
        for j in range(A_KV_DIM // LANES):
            piece = k[:, j * LANES:(j + 1) * LANES]
            partner = jnp.where(upper, pltpu.roll(piece, half, 1), pltpu.roll(piece, LANES - half, 1))
            k_ref[tok, j * LANES:(j + 1) * LANES] = (piece * cos + partner * sin).astype(BF16)

        cos_t = cost_ref[:, tok]
        sin_t = sint_ref[:, tok]
        for c in range((A_Q_ROWS + A_KV_DIM) // A_PROJ_ROWS):
            lo = c * A_PROJ_ROWS
            part = _dot_nt(wt_ref[lo:lo + A_PROJ_ROWS, :], h) + bt_ref[lo:lo + A_PROJ_ROWS, tok]
            if lo < A_Q_ROWS:
                for j in range(A_PROJ_ROWS // A_HEAD_DIM):
                    base = j * A_HEAD_DIM
                    x1 = part[base:base + half]
                    x2 = part[base + half:base + 2 * half]
                    rot = jnp.concatenate([x1 * cos_t - x2 * sin_t, x2 * cos_t + x1 * sin_t,
                                           part[base + 2 * half:base + A_HEAD_DIM]], axis=0)
                    qt_ref[lo + base:lo + base + A_HEAD_DIM, tok] = (rot * q_scale).astype(BF16)
            else:
                vt_ref[:, tok] = part.astype(BF16)


def _attn_proj(x, nw, wt, bt, wk, bk, cos_a, sin_a, cos_at, sin_at, *, tm):
    t = x.shape[0]
    half = A_ROPE_DIM // 2
    rows = A_Q_ROWS + A_KV_DIM
    return pl.pallas_call(
        _attn_proj_kernel,
        grid=(t // tm,),
        in_specs=[_rows(tm, D_MODEL), _resident((1, D_MODEL)), _resident((rows, D_MODEL)), _resident((rows, tm)),
                  _resident((D_MODEL, A_KV_DIM)), _resident((1, A_KV_DIM)),
                  _rows(tm, LANES), _rows(tm, LANES), _cols(half, tm), _cols(half, tm)],
        out_specs=[_cols(A_Q_ROWS, tm), _rows(tm, A_KV_DIM), _cols(A_KV_DIM, tm)],
        out_shape=[jax.ShapeDtypeStruct((A_Q_ROWS, t), BF16),
                   jax.ShapeDtypeStruct((t, A_KV_DIM), BF16),
                   jax.ShapeDtypeStruct((A_KV_DIM, t), BF16)],
        compiler_params=_params(1),
        name="attn_proj",
    )(x, nw, wt, bt, wk, bk, cos_a, sin_a, cos_at, sin_at)


def _attn_core_kernel(sink_ref, qt_ref, kp_ref, kc_ref, vtp_ref, vtc_ref, ot_ref):
    L = A_CHUNK
    step = pl.program_id(1)
    key = lax.broadcasted_iota(jnp.int32, (L, L), 0)
    qry = lax.broadcasted_iota(jnp.int32, (L, L), 1)
    from_prev = key > qry
    first_bias = jnp.where(step > 0, 0.0, -jnp.inf).astype(F32)
    zero_q = jnp.zeros((A_HEAD_DIM, A_GROUP * L), BF16)
    units = [(u, g) for u in range(A_BLOCKS_PER_STEP) for g in range(A_KV_HEADS)]
    scores, probs_of = {}, {}

    def score(unit):
        u, g = unit
        col = slice((g // 2) * LANES, (g // 2 + 1) * LANES)
        if u == 0:
            kk = jnp.concatenate([kp_ref[:, col], kc_ref[:L, col]], axis=0)
        else:
            kk = kc_ref[(u - 1) * L:(u + 1) * L, col]
        q_cat = jnp.concatenate([qt_ref[(A_GROUP * g + j) * A_HEAD_DIM:(A_GROUP * g + j + 1) * A_HEAD_DIM,
                                        u * L:(u + 1) * L] for j in range(A_GROUP)], axis=1)
        q_ext = jnp.concatenate([q_cat, zero_q] if g % 2 == 0 else [zero_q, q_cat], axis=0)
        scores[unit] = _dot(kk, q_ext)

    def softmax(unit):
        u, g = unit
        s = scores.pop(unit)
        probs = []
        for j in range(A_GROUP):
            sink = sink_ref[A_GROUP * g + j]
            s_prev = s[:L, j * L:(j + 1) * L]
            if u == 0:
                s_prev = s_prev + first_bias
            s_own = s[L:, j * L:(j + 1) * L]
            sc = jnp.where(from_prev, s_prev, s_own)
            mx = jnp.maximum(jnp.max(sc, axis=0, keepdims=True), sink)
            p = jnp.exp(sc - mx)
            denom = jnp.sum(p, axis=0, keepdims=True) + jnp.exp(sink - mx)
            pn = p * (1.0 / denom)
            zero = jnp.zeros_like(pn)
            probs.append(jnp.concatenate([jnp.where(from_prev, pn, zero), jnp.where(from_prev, zero, pn)],
                                         axis=0).astype(BF16))
        probs_of[unit] = jnp.concatenate(probs, axis=1)

    def weighted_values(unit):
        u, g = unit
        rows = slice(g * A_HEAD_DIM, (g + 1) * A_HEAD_DIM)
        if u == 0:
            vt = jnp.concatenate([vtp_ref[rows, :], vtc_ref[rows, :L]], axis=1)
        else:
            vt = vtc_ref[rows, (u - 1) * L:(u + 1) * L]
        out = _dot(vt, probs_of.pop(unit))
        for j in range(A_GROUP):
            head = A_GROUP * g + j
            ot_ref[head * A_HEAD_DIM:(head + 1) * A_HEAD_DIM, u * L:(u + 1) * L] = (
                out[:, j * L:(j + 1) * L].astype(BF16))

    n = len(units)
    score(units[0])
    for i in range(n + 1):
        if i + 1 < n:
            score(units[i + 1])
        if i < n:
            softmax(units[i])
        if i >= 1:
            weighted_values(units[i - 1])


def _attn_core(qt, k, vt, sinks, *, batch, seq):
    L = A_CHUNK
    W = A_BLOCKS_PER_STEP * L
    assert seq % W == 0
    ns = seq // W
    cur = lambda b, i: (b * ns + i, 0)
    cur_t = lambda b, i: (0, b * ns + i)
    prev = lambda b, i: ((b * ns + i) * A_BLOCKS_PER_STEP - jnp.minimum(i, 1), 0)
    prev_t = lambda b, i: (0, (b * ns + i) * A_BLOCKS_PER_STEP - jnp.minimum(i, 1))
    return pl.pallas_call(
        _attn_core_kernel,
        grid=(batch, ns),
        in_specs=[pl.BlockSpec(memory_space=pltpu.SMEM),
                  pl.BlockSpec((A_Q_ROWS, W), cur_t),
                  pl.BlockSpec((L, A_KV_DIM), prev), pl.BlockSpec((W, A_KV_DIM), cur),
                  pl.BlockSpec((A_KV_DIM, L), prev_t), pl.BlockSpec((A_KV_DIM, W), cur_t)],
        out_specs=pl.BlockSpec((A_Q_ROWS, W), cur_t),
        out_shape=jax.ShapeDtypeStruct((A_Q_ROWS, batch * seq), BF16),
        compiler_params=_params(2),
        name="attn_core",
    )(sinks, qt, k, k, vt, vt)


M_QK_COLS = M_HEADS * M_QK_DIM
M_V_COLS = M_HEADS * M_V_DIM
M_T_ROWS = M_QK_COLS + 2 * M_V_COLS
M_PROJ_ROWS = 512
M_STATE_ROWS = M_V_DIM + BF16_ROWS


def _mlstm_proj_kernel(x_ref, nw_ref, wt_ref, wk_ref, wg_ref, bg_ref,
                       qt_ref, k_ref, vt_ref, ot_ref, src_ref, gt_ref):
    sub = x_ref.shape[0] // ROW_SPLIT
    H = M_HEADS
    lane = lax.broadcasted_iota(jnp.int32, (sub, LANES), 1)
    time = lax.broadcasted_iota(jnp.int32, (sub, LANES), 0) & (M_CHUNK - 1)
    for split in range(ROW_SPLIT):
        tok = slice(split * sub, (split + 1) * sub)
        h = _rms(x_ref[tok, :], nw_ref[...]).astype(BF16)
        k_ref[tok, :] = _dot(h, wk_ref[...]).astype(BF16)
        for c in range(M_T_ROWS // M_PROJ_ROWS):
            lo = c * M_PROJ_ROWS
            part = _dot_nt(wt_ref[lo:lo + M_PROJ_ROWS, :], h)
            if lo < M_QK_COLS:
                qt_ref[:, tok] = (part * (M_QK_DIM ** -0.5)).astype(BF16)
            elif lo < M_QK_COLS + M_V_COLS:
                vt_ref[lo - M_QK_COLS:lo - M_QK_COLS + M_PROJ_ROWS, tok] = part.astype(BF16)
            else:
                o_lo = lo - M_QK_COLS - M_V_COLS
                ot_ref[o_lo:o_lo + M_PROJ_ROWS, tok] = jax.nn.sigmoid(part)

        gates = _dot(h, wg_ref[...]) + bg_ref[...]
        gates = M_GATE_CAP * jnp.tanh(gates / M_GATE_CAP)
        b = jnp.where((lane >= H) & (lane < 2 * H), jax.nn.log_sigmoid(gates), 0.0)
        shift = 1
        while shift < M_CHUNK:
            b = b + jnp.where(time >= shift, pltpu.roll(b, shift, 0), 0.0)
            shift *= 2
        src_ref[tok, :] = gates - pltpu.roll(b, LANES - H, 1)
        gt_ref[:, tok] = jnp.where(lane < H, gates, b).T[:2 * H]


def _mlstm_proj(x, nw, wt, wk, wg, bg, *, tm):
    t = x.shape[0]
    assert (tm // ROW_SPLIT) % M_CHUNK == 0 and M_CHUNK & (M_CHUNK - 1) == 0
    return pl.pallas_call(
        _mlstm_proj_kernel,
        grid=(t // tm,),
        in_specs=[_rows(tm, D_MODEL), _resident((1, D_MODEL)), _resident((M_T_ROWS, D_MODEL)),
                  _resident((D_MODEL, M_QK_COLS)), _resident((D_MODEL, LANES)), _resident((1, LANES))],
        out_specs=[_cols(M_QK_COLS, tm), _rows(tm, M_QK_COLS), _cols(M_V_COLS, tm),
                   _cols(M_V_COLS, tm), _rows(tm, LANES), _cols(2 * M_HEADS, tm)],
        out_shape=[jax.ShapeDtypeStruct((M_QK_COLS, t), BF16),
                   jax.ShapeDtypeStruct((t, M_QK_COLS), BF16),
                   jax.ShapeDtypeStruct((M_V_COLS, t), BF16),
                   jax.ShapeDtypeStruct((M_V_COLS, t), F32),
                   jax.ShapeDtypeStruct((t, LANES), F32),
                   jax.ShapeDtypeStruct((2 * M_HEADS, t), F32)],
        compiler_params=_params(1),
        name="mlstm_proj",
    )(x, nw, wt, wk, wg, bg)


def _mlstm_core_kernel(qt_ref, k_ref, vt_ref, ot_ref, src_ref, gt_ref, nwt_ref, yt_ref, c_sc, m_sc):
    L = M_CHUNK
    H = M_HEADS

    @pl.when(pl.program_id(1) == 0)
    def _():
        c_sc[...] = jnp.zeros_like(c_sc)
        m_sc[...] = jnp.zeros_like(m_sc)

    src_i = lax.broadcasted_iota(jnp.int32, (L, L), 0)
    tgt_i = lax.broadcasted_iota(jnp.int32, (L, L), 1)
    causal = src_i <= tgt_i
    ones_rows = (lax.broadcasted_iota(jnp.int32, (BF16_ROWS, L), 0) == 0).astype(BF16)
    zero_q = jnp.zeros((M_QK_DIM, L), BF16)
    low = lax.broadcasted_iota(jnp.int32, (L, LANES), 1) < M_QK_DIM

    units = [(c, h) for c in range(M_CHUNKS_PER_STEP) for h in range(H)]
    st = {unit: dict() for unit in units}

    def score(unit):
        c, h = unit
        d = st[unit]
        tok = slice(c * L, (c + 1) * L)
        pair = h // 2
        k_pair = k_ref[tok, pair * LANES:(pair + 1) * LANES]
        d["k"] = jnp.where(low if h % 2 == 0 else jnp.logical_not(low), k_pair, jnp.zeros_like(k_pair))
        qt_h = qt_ref[h * M_QK_DIM:(h + 1) * M_QK_DIM, tok]
        d["qt"] = jnp.concatenate([qt_h, zero_q] if h % 2 == 0 else [zero_q, qt_h], axis=0)
        d["v"] = jnp.concatenate([vt_ref[h * M_V_DIM:(h + 1) * M_V_DIM, tok], ones_rows], axis=0)
        d["state"] = c_sc[h]
        d["s"] = _dot(d["k"], d["qt"])
        d["carry"] = _dot(d["state"].astype(BF16), d["qt"])

    def gate(unit):
        c, h = unit
        d = st[unit]
        tok = slice(c * L, (c + 1) * L)
        src_col = src_ref[tok, h:h + 1]
        d["b"] = gt_ref[H + h:H + h + 1, tok]
        d["m_prev"] = m_sc[h][:, :1]
        dmat = jnp.where(causal, src_col + d["b"], -jnp.inf)
        inter = d["b"] + d["m_prev"]
        d["m_row"] = jnp.maximum(inter, jnp.max(dmat, axis=0, keepdims=True))
        d["w_intra"] = jnp.exp(dmat - d["m_row"])
        d["w_inter"] = jnp.exp(inter - d["m_row"])

    def emit(unit):
        c, h = unit
        d = st[unit]
        tok = slice(c * L, (c + 1) * L)
        tot = _dot(d["v"], (d["s"] * d["w_intra"]).astype(BF16)) + d["w_inter"] * d["carry"]
        num = tot[:M_V_DIM]
        nq = tot[M_V_DIM:M_V_DIM + 1]
        h_out = num * (1.0 / jnp.maximum(jnp.abs(nq), jnp.exp(-d["m_row"])))
        h_out = h_out * lax.rsqrt(jnp.mean(h_out * h_out, axis=0, keepdims=True) + EPS)
        rows = slice(h * M_V_DIM, (h + 1) * M_V_DIM)
        yt_ref[rows, tok] = (ot_ref[rows, tok] * (h_out * nwt_ref[rows, :])).astype(BF16)

    def update(unit):
        c, h = unit
        d = st[unit]
        b_last = d["b"][:, L - 1:L]
        dec = b_last + (gt_ref[h:h + 1, c * L:(c + 1) * L] - d["b"])
        m_new = jnp.maximum(b_last + d["m_prev"], jnp.max(dec, axis=-1, keepdims=True))
        wk = jnp.exp(dec - m_new)
        keep = jnp.exp(b_last + d["m_prev"] - m_new)
        v_scaled = (d["v"].astype(F32) * wk).astype(BF16)
        c_sc[h] = keep * d["state"] + _dot(v_scaled, d["k"])
        m_sc[h] = jnp.broadcast_to(m_new, (1, LANES))
        d.clear()

    n = len(units)
    score(units[0])
    for i in range(n + 1):
        if i + 1 < n:
            score(units[i + 1])
        if i < n:
            gate(units[i])
            emit(units[i])
        if i >= 1:
            update(units[i - 1])


def _mlstm_core(qt, k, vt, ot, src, gt, nwt, *, batch, seq):
    L = M_CHUNK
    W = M_CHUNKS_PER_STEP * L
    assert seq % W == 0
    nb = seq // W
    cur = lambda b, i: (b * nb + i, 0)
    cur_t = lambda b, i: (0, b * nb + i)
    return pl.pallas_call(
        _mlstm_core_kernel,
        grid=(batch, nb),
        in_specs=[pl.BlockSpec((M_QK_COLS, W), cur_t), pl.BlockSpec((W, M_QK_COLS), cur),
                  pl.BlockSpec((M_V_COLS, W), cur_t), pl.BlockSpec((M_V_COLS, W), cur_t),
                  pl.BlockSpec((W, LANES), cur), pl.BlockSpec((2 * M_HEADS, W), cur_t), _resident((M_V_COLS, L))],
        out_specs=pl.BlockSpec((M_V_COLS, W), cur_t),
        out_shape=jax.ShapeDtypeStruct((M_V_COLS, batch * seq), BF16),
        scratch_shapes=[pltpu.VMEM((M_HEADS, M_STATE_ROWS, LANES), F32),
                        pltpu.VMEM((M_HEADS, 1, LANES), F32)],
        compiler_params=_params(2),
        name="mlstm_core",
    )(qt, k, vt, ot, src, gt, nwt)


R_QK_COLS = R_HEADS * R_QK_DIM
R_V_COLS = R_HEADS * R_V_DIM
R_PROJ_CHUNK = 512


def _ret_proj_kernel(x_ref, nw_ref, wq_ref, wv_ref, wg_ref, wkt_ref, cos_ref, sin_ref, cost_ref, sint_ref, qdec_ref, kdec_ref,
                     q_ref, kt_ref, v_ref, g_ref):
    sub = x_ref.shape[0] // ROW_SPLIT
    half = R_QK_DIM // 2
    for split in range(ROW_SPLIT):
        tok = slice(split * sub, (split + 1) * sub)
        h = _rms(x_ref[tok, :], nw_ref[...]).astype(BF16)
        cos = cos_ref[tok, :]
        sin = sin_ref[tok, :]
        for c in range(R_QK_COLS // R_PROJ_CHUNK):
            lo = c * R_PROJ_CHUNK
            t = _dot(h, wq_ref[:, lo:lo + R_PROJ_CHUNK])
            for hh in range(R_PROJ_CHUNK // R_QK_DIM):
                head = lo // R_QK_DIM + hh
                dec = qdec_ref[head, tok, :]
                x1 = t[:, hh * R_QK_DIM:hh * R_QK_DIM + half]
                x2 = t[:, hh * R_QK_DIM + half:(hh + 1) * R_QK_DIM]
                base = head * R_QK_DIM
                q_ref[tok, base:base + half] = ((x1 * cos - x2 * sin) * dec).astype(BF16)
                q_ref[tok, base + half:base + R_QK_DIM] = ((x2 * cos + x1 * sin) * dec).astype(BF16)
        for c in range(2 * R_V_COLS // R_PROJ_CHUNK):
            lo = c * R_PROJ_CHUNK
            if lo < R_V_COLS:
                v_ref[tok, lo:lo + R_PROJ_CHUNK] = _dot(h, wv_ref[:, lo:lo + R_PROJ_CHUNK]).astype(BF16)
            else:
                g_lo = lo - R_V_COLS
                t = _dot(h, wg_ref[:, g_lo:g_lo + R_PROJ_CHUNK])
                g_ref[tok, g_lo:g_lo + R_PROJ_CHUNK] = (t * jax.nn.sigmoid(t)).astype(BF16)
        cos_t = cost_ref[:, tok]
        sin_t = sint_ref[:, tok]
        for head in range(R_HEADS):
            base = head * R_QK_DIM
            kt = _dot_nt(wkt_ref[base:base + R_QK_DIM, :], h)
            dec = kdec_ref[head:head + 1, tok]
            x1 = kt[:half]
            x2 = kt[half:]
            kt_ref[base:base + half, tok] = ((x1 * cos_t - x2 * sin_t) * dec).astype(BF16)
            kt_ref[base + half:base + R_QK_DIM, tok] = ((x2 * cos_t + x1 * sin_t) * dec).astype(BF16)


def _ret_proj(x, nw, wq, w_all, wkt, cos_r, sin_r, cos_rt, sin_rt, qdec, kdec, *, tm):
    t = x.shape[0]
    assert 2 * R_QK_COLS == R_V_COLS
    col_block = lambda j: pl.BlockSpec((D_MODEL, R_V_COLS), lambda *_: (0, j), pipeline_mode=pl.Buffered(1))
    return pl.pallas_call(
        _ret_proj_kernel,
        grid=(t // tm,),
        in_specs=[_rows(tm, D_MODEL), _resident((1, D_MODEL)), _resident((D_MODEL, R_QK_COLS)),
                  col_block(1), col_block(2), _resident((R_QK_COLS, D_MODEL)),
                  _rows(tm, LANES), _rows(tm, LANES), _cols(LANES, tm), _cols(LANES, tm),
                  _resident((R_HEADS, tm, LANES)), _resident((SUBLANES, tm))],
        out_specs=[_rows(tm, R_QK_COLS), _cols(R_QK_COLS, tm), _rows(tm, R_V_COLS), _rows(tm, R_V_COLS)],
        out_shape=[jax.ShapeDtypeStruct((t, R_QK_COLS), BF16),
                   jax.ShapeDtypeStruct((R_QK_COLS, t), BF16),
                   jax.ShapeDtypeStruct((t, R_V_COLS), BF16),
                   jax.ShapeDtypeStruct((t, R_V_COLS), BF16)],
        compiler_params=_params(1),
        name="ret_proj",
    )(x, nw, wq, w_all, w_all, wkt, cos_r, sin_r, cos_rt, sin_rt, qdec, kdec)


def _ret_core_kernel(cdec_ref, q_ref, kt_ref, v_ref, g_ref, nw_ref, y_ref, s_sc):
    L = R_CHUNK

    @pl.when(pl.program_id(1) == 0)
    def _():
        s_sc[...] = jnp.zeros_like(s_sc)

    row = lax.broadcasted_iota(jnp.int32, (L, L), 0)
    col = lax.broadcasted_iota(jnp.int32, (L, L), 1)
    causal = row >= col
    nw = nw_ref[...]
    units = [(c, h) for c in range(R_CHUNKS_PER_STEP) for h in range(R_HEADS)]
    st = {unit: dict() for unit in units}

    def score(unit):
        c, h = unit
        d = st[unit]
        tok = slice(c * L, (c + 1) * L)
        d["q"] = q_ref[tok, h * R_QK_DIM:(h + 1) * R_QK_DIM]
        d["kt"] = kt_ref[h * R_QK_DIM:(h + 1) * R_QK_DIM, tok]
        d["v"] = v_ref[tok, h * R_V_DIM:(h + 1) * R_V_DIM]
        d["state"] = s_sc[h]
        d["scores"] = _dot(d["q"], d["kt"])
        d["carry"] = _dot(d["q"], d["state"].astype(BF16))

    def emit(unit):
        c, h = unit
        d = st[unit]
        tok = slice(c * L, (c + 1) * L)
        scores = jnp.where(causal, d["scores"], 0.0)
        y = _dot(scores.astype(BF16), d["v"]) + d["carry"]
        s_sc[h] = cdec_ref[h] * (d["state"] + _dot(d["kt"], d["v"]))
        d.clear()
        mu = jnp.mean(y, axis=-1, keepdims=True)
        yc = y - mu
        yn = yc * lax.rsqrt(jnp.mean(yc * yc, axis=-1, keepdims=True) + EPS)
        gate = g_ref[tok, h * R_V_DIM:(h + 1) * R_V_DIM]
        out = gate * (yn * nw[:, h * R_V_DIM:(h + 1) * R_V_DIM])
        y_ref[tok, h * R_V_DIM:(h + 1) * R_V_DIM] = out.astype(BF16)

    score(units[0])
    for i, unit in enumerate(units):
        if i + 1 < len(units):
            score(units[i + 1])
        emit(unit)


def _ret_core(q, kt, v, g, nw, cdec, *, batch, seq):
    L = R_CHUNK
    W = R_CHUNKS_PER_STEP * L
    assert seq % W == 0
    nb = seq // W
    cur = lambda b, i: (b * nb + i, 0)
    cur_t = lambda b, i: (0, b * nb + i)
    return pl.pallas_call(
        _ret_core_kernel,
        grid=(batch, nb),
        in_specs=[pl.BlockSpec(memory_space=pltpu.SMEM),
                  pl.BlockSpec((W, R_QK_COLS), cur), pl.BlockSpec((R_QK_COLS, W), cur_t),
                  pl.BlockSpec((W, R_V_COLS), cur), pl.BlockSpec((W, R_V_COLS), cur),
                  _resident((1, R_V_COLS))],
        out_specs=pl.BlockSpec((W, R_V_COLS), cur),
        out_shape=jax.ShapeDtypeStruct((batch * seq, R_V_COLS), BF16),
        scratch_shapes=[pltpu.VMEM((R_HEADS, R_QK_DIM, R_V_DIM), F32)],
        compiler_params=_params(2),
        name="ret_core",
    )(cdec, q, kt, v, g, nw)


def _rope_constants(tm):
    inv_a = 1.0 / (ROPE_THETA ** (jnp.arange(0, A_ROPE_DIM, 2, dtype=F32) / A_ROPE_DIM))
    inv_r = 1.0 / (R_THETA ** jnp.linspace(0.0, 1.0, R_QK_DIM // 2, dtype=F32))
    return (jnp.broadcast_to(inv_a[:, None], (A_ROPE_DIM // 2, tm)),
            jnp.broadcast_to(inv_r[:, None], (R_QK_DIM // 2, tm)))


def _retention_constants(tm):
    L = R_CHUNK
    log_gamma = jnp.log(1.0 - 2.0 ** (-5.0 - jnp.arange(R_HEADS, dtype=F32)))
    j = jnp.asarray(np.arange(tm) % L, F32)
    q_decay = jnp.exp((j + 1.0) * log_gamma[:, None])
    k_decay = jnp.exp(-(j + 1.0) * log_gamma[:, None]) * (R_QK_DIM ** -0.5)
    chunk_decay = jnp.exp(L * log_gamma)
    qdec = jnp.broadcast_to(q_decay[:, :, None], (R_HEADS, tm, LANES))
    kdec = jnp.pad(k_decay, ((0, SUBLANES - R_HEADS), (0, 0)))
    return qdec, kdec, chunk_decay


def _attn_weight_layout(w_qkv, b_qkv, tm):
    nq = A_Q_ROWS
    wt = jnp.concatenate([w_qkv[:, :nq], w_qkv[:, nq + A_KV_DIM:]], axis=1).T.astype(BF16)
    b_qv = jnp.concatenate([b_qkv[:nq], b_qkv[nq + A_KV_DIM:]])
    bt = jnp.broadcast_to(b_qv[:, None], (nq + A_KV_DIM, tm))
    wk = w_qkv[:, nq:nq + A_KV_DIM].astype(BF16)
    bk = b_qkv[nq:nq + A_KV_DIM].reshape(1, A_KV_DIM)
    return wt, bt, wk, bk


def _ret_weight_layout(w_in):
    w_all = w_in.astype(BF16)

    def split_pairs(w):
        w = w.reshape(D_MODEL, R_HEADS, R_QK_DIM // 2, 2)
        return jnp.swapaxes(w, 2, 3).reshape(D_MODEL, R_QK_COLS)

    return split_pairs(w_all[:, :R_QK_COLS]), w_all, split_pairs(w_all[:, R_QK_COLS:2 * R_QK_COLS]).T


def _mlstm_weight_layout(w_in, b_if):
    pad = LANES - 2 * M_HEADS
    gate_lo = 2 * M_QK_COLS + 2 * M_V_COLS
    wt = jnp.concatenate([w_in[:, :M_QK_COLS], w_in[:, 2 * M_QK_COLS:gate_lo]], axis=-1).T.astype(BF16)
    wk = w_in[:, M_QK_COLS:2 * M_QK_COLS].astype(BF16)
    wg = jnp.pad(w_in[:, gate_lo:], ((0, 0), (0, pad))).astype(BF16)
    return wt, wk, wg, jnp.pad(b_if, (0, pad)).reshape(1, LANES)


def kernel(x, positions, norm_w, final_norm_w, ffn_w_in, ffn_w_out, attn_w_qkv, attn_b_qkv, attn_w_o, attn_b_o,
           attn_sinks, mlstm_w_in, mlstm_b_if, mlstm_norm_w, mlstm_w_out, ret_w_in, ret_norm_w, ret_w_out):
    batch, seq, d = x.shape
    t = batch * seq
    tm = min(ROW_TILE, t)
    assert d == D_MODEL and t % tm == 0 and (tm // ROW_SPLIT) % R_CHUNK == 0
    assert seq % R_CHUNK == 0 and seq % M_CHUNK == 0 and seq % (A_CHUNK * A_BLOCKS_PER_STEP) == 0

    xf = x.reshape(t, d)
    inv_a, inv_r = _rope_constants(tm)
    cos_a, sin_a, cos_r, sin_r, cos_at, sin_at, cos_rt, sin_rt = _rope_tables(
        positions.astype(F32).reshape(1, t), inv_a, inv_r, tm)
    qdec, kdec, cdec = _retention_constants(tm)
    zero_bias = jnp.zeros((1, D_MODEL), F32)
    norm_w4 = norm_w.reshape(DEPTH, 3, 1, d)
    w_in_bf = ffn_w_in.astype(BF16)
    w_out_bf = ffn_w_out.astype(BF16)

    for i in range(DEPTH):
        kind, slot = i % N_MIXERS, i // N_MIXERS
        xf = _ffn(xf, norm_w4, w_in_bf, w_out_bf, i, 0, tm=tm)
        nw = norm_w[i, 1].reshape(1, d)
        if kind == 0:
            wt, bt, wk, bk = _attn_weight_layout(attn_w_qkv[slot], attn_b_qkv[slot], tm)
            qt, k, vt = _attn_proj(xf, nw, wt, bt, wk, bk, cos_a, sin_a, cos_at, sin_at, tm=tm)
            yt = _attn_core(qt, k, vt, attn_sinks[slot], batch=batch, seq=seq)
            mix = (yt, attn_w_o[slot].astype(BF16), attn_b_o[slot].reshape(1, d), True)
        elif kind == 1:
            wt, wk, wg, bg = _mlstm_weight_layout(mlstm_w_in[slot], mlstm_b_if[slot])
            qt, k, vt, ot, src, gt = _mlstm_proj(xf, nw, wt, wk, wg, bg, tm=tm)
            nwt = jnp.broadcast_to(mlstm_norm_w[slot][:, None], (M_V_COLS, M_CHUNK))
            yt = _mlstm_core(qt, k, vt, ot, src, gt, nwt, batch=batch, seq=seq)
            mix = (yt, mlstm_w_out[slot].astype(BF16), zero_bias, True)
        else:
            wq, w_all, wkt = _ret_weight_layout(ret_w_in[slot])
            q, kt, v, g = _ret_proj(xf, nw, wq, w_all, wkt, cos_r, sin_r, cos_rt, sin_rt, qdec, kdec, tm=tm)
            y = _ret_core(q, kt, v, g, ret_norm_w[slot].reshape(1, R_V_COLS), cdec, batch=batch, seq=seq)
            mix = (y, ret_w_out[slot].astype(BF16), zero_bias, False)
        last = i == DEPTH - 1
        xf = _ffn(xf, norm_w4, w_in_bf, w_out_bf, i, 1, tm=tm, mix=mix,
                  final_w=final_norm_w.reshape(1, d) if last else None)
    return xf.reshape(batch, seq, d)
```
